```python
import jax, jax.numpy as jnp
from jax import lax
import numpy as np

D_MODEL = 2048
BATCH = 4
SEQ = 4096
DEPTH = 1

GRID_W = 64
CTX_LEN = 256
D_INNER = D_MODEL
W_A = D_INNER // 2
W_B = D_INNER - W_A
HG_HEADS = 8
HG_DK = W_A // HG_HEADS
HG_DV = W_A // HG_HEADS
ML_HEADS = 4
ML_DH = W_B // ML_HEADS
CHUNK = 64
CONV_K = 3
N_IN = 5 * W_A + 5 * W_B + 4 * ML_HEADS
ALPHA = (2 * DEPTH) ** 0.25
BETA = (8 * DEPTH) ** -0.25
LN_EPS = 1e-5
NORM_EPS = 1e-6

kernel_name = "hymba_hgrn2_mlstm_bidir_dit_block"


def layer_norm(a, g, b):
    af = a.astype(jnp.float32)
    mu = jnp.mean(af, axis=-1, keepdims=True)
    var = jnp.mean(jnp.square(af - mu), axis=-1, keepdims=True)
    out = (af - mu) * lax.rsqrt(var + LN_EPS) * g.astype(jnp.float32) + b.astype(jnp.float32)
    return out.astype(a.dtype)


def modulate(a, shift, scale):
    af = a.astype(jnp.float32)
    mu = jnp.mean(af, axis=-1, keepdims=True)
    var = jnp.mean(jnp.square(af - mu), axis=-1, keepdims=True)
    n = (af - mu) * lax.rsqrt(var + LN_EPS)
    return (n * (1.0 + scale.astype(jnp.float32)) + shift.astype(jnp.float32)).astype(a.dtype)


def rms_norm(a):
    return a * lax.rsqrt(jnp.mean(jnp.square(a), axis=-1, keepdims=True) + NORM_EPS)


def head_layer_norm(a):
    mu = jnp.mean(a, axis=-1, keepdims=True)
    var = jnp.mean(jnp.square(a - mu), axis=-1, keepdims=True)
    return (a - mu) * lax.rsqrt(var + NORM_EPS)


def heads(a, n_heads):
    return a.reshape(a.shape[:2] + (n_heads, a.shape[-1] // n_heads))


def flip(a):
    return jnp.flip(a, axis=1)


def to_chunks(a):
    bsz, t = a.shape[:2]
    a = a.reshape((bsz, t // CHUNK, CHUNK) + a.shape[2:])
    return jnp.swapaxes(jnp.moveaxis(a, 1, 0), 2, 3)


def from_chunks(o):
    o = jnp.moveaxis(jnp.swapaxes(o, 2, 3), 0, 1)
    return o.reshape((o.shape[0], o.shape[1] * o.shape[2]) + o.shape[3:])


def hgrn2_scan(q, k, v, logf, s0):
    mask = jnp.tril(jnp.ones((CHUNK, CHUNK), dtype=bool))[:, :, None]

    def step(s, inp):
        qc, kc, vc, gc = inp
        b = jnp.cumsum(gc, axis=2)
        o_inter = jnp.einsum('bhtk,bhkv->bhtv', qc * jnp.exp(b), s)
        diff = b[:, :, :, None, :] - b[:, :, None, :, :]
        decay = jnp.exp(jnp.where(mask, diff, -jnp.inf))
        scores = jnp.einsum('bhtk,bhsk,bhtsk->bhts', qc, kc, decay)
        o = o_inter + jnp.einsum('bhts,bhsv->bhtv', scores, vc)
        b_last = b[:, :, -1]
        k_dec = kc * jnp.exp(b_last[:, :, None, :] - b)
        s_new = jnp.exp(b_last)[..., None] * s + jnp.einsum('bhsk,bhsv->bhkv', k_dec, vc)
        return s_new, o

    s_fin, o = lax.scan(step, s0, (to_chunks(q), to_chunks(k), to_chunks(v), to_chunks(logf)))
    return from_chunks(o), s_fin


def mlstm_scan(q, k, v, log_i, log_f, state):
    mask = jnp.tril(jnp.ones((CHUNK, CHUNK), dtype=bool))

    def step(carry, inp):
        c_mat, n_vec, m = carry
        qc, kc, vc, ic, fc = inp
        b = jnp.cumsum(fc, axis=-1)
        log_w = jnp.where(mask, b[..., :, None] - b[..., None, :] + ic[..., None, :], -jnp.inf)
        m_inter = b + m[..., None]
        m_t = jnp.maximum(m_inter, jnp.max(log_w, axis=-1))
        w_inter = jnp.exp(m_inter - m_t)
        w_qk = jnp.exp(log_w - m_t[..., None]) * jnp.einsum('bhtk,bhsk->bhts', qc, kc)
        num = (w_inter[..., None] * jnp.einsum('bhvk,bhtk->bhtv', c_mat, qc)
               + jnp.einsum('bhts,bhsv->bhtv', w_qk, vc))
        den = w_inter * jnp.einsum('bhk,bhtk->bht', n_vec, qc) + jnp.sum(w_qk, axis=-1)
        h = num / jnp.maximum(jnp.abs(den), jnp.exp(-m_t))[..., None]
        m_new = m_t[..., -1]
        w_s = jnp.exp(b[..., -1:] - b + ic - m_new[..., None])
        decay = jnp.exp(b[..., -1] + m - m_new)
        c_new = decay[..., None, None] * c_mat + jnp.einsum('bhsv,bhsk->bhvk', w_s[..., None] * vc, kc)
        n_new = decay[..., None] * n_vec + jnp.einsum('bhs,bhsk->bhk', w_s, kc)
        return (c_new, n_new, m_new), h

    st, h = lax.scan(step, state, (to_chunks(q), to_chunks(k), to_chunks(v), to_chunks(log_i), to_chunks(log_f)))
    return from_chunks(h), st


def short_conv(a, w, b, grid):
    ch = a.shape[-1]
    w = w.astype(jnp.float32)
    if grid:
        rows = a.shape[1] // GRID_W
        img = a.reshape(a.shape[0], rows, GRID_W, ch)
        out = lax.conv_general_dilated(img, w[:, :, None, :], (1, 1), 'SAME',
                                       dimension_numbers=('NHWC', 'HWIO', 'NHWC'), feature_group_count=ch)
        out = out.reshape(a.shape)
    else:
        out = lax.conv_general_dilated(a, w[1][:, None, :], (1,), 'SAME',
                                       dimension_numbers=('NWC', 'WIO', 'NWC'), feature_group_count=ch)
    return jax.nn.silu(out + b.astype(jnp.float32))


def hgrn2_gates(z, lb):
    f = lb + (1.0 - lb) * jax.nn.sigmoid(z)
    return jnp.log(f), 1.0 - f


def zero_states(bsz):
    hg = jnp.zeros((bsz, HG_HEADS, HG_DK, HG_DV), jnp.float32)
    ml = (jnp.zeros((bsz, ML_HEADS, ML_DH, ML_DH), jnp.float32),
          jnp.zeros((bsz, ML_HEADS, ML_DH), jnp.float32),
          jnp.zeros((bsz, ML_HEADS), jnp.float32))
    return (hg, hg, ml, ml)


def mix(u, grid, states, lb, conv_w_l, conv_b_l, gate_b_l, hg_norm_l, ml_norm_l):
    u = u.astype(jnp.float32)
    bsz, t = u.shape[:2]
    splits = [W_A, 2 * W_A, 3 * W_A, 4 * W_A, 5 * W_A,
              5 * W_A + 2 * W_B, 5 * W_A + 3 * W_B, 5 * W_A + 4 * W_B, 5 * W_A + 5 * W_B]
    a_q, a_ff, a_fb, a_i, a_z, b_qk, b_v, b_o, b_z, b_g = jnp.split(u, splits, axis=-1)
    hg_f0, hg_b0, ml_f0, ml_b0 = states

    q_a = heads(jax.nn.silu(a_q), HG_HEADS)
    v_a = heads(a_i, HG_HEADS)
    logf_f, k_f = hgrn2_gates(a_ff, lb[0])
    logf_b, k_b = hgrn2_gates(a_fb, lb[1])
    o_f, hg_f = hgrn2_scan(q_a, heads(k_f, HG_HEADS), v_a, heads(logf_f, HG_HEADS), hg_f0)
    o_b, hg_b = hgrn2_scan(flip(q_a), flip(heads(k_b, HG_HEADS)), flip(v_a), flip(heads(logf_b, HG_HEADS)), hg_b0)
    o_a = rms_norm(o_f + flip(o_b)) * hg_norm_l.astype(jnp.float32).reshape(HG_HEADS, HG_DV)
    y_a = o_a.reshape(bsz, t, W_A) * jax.nn.silu(a_z)

    qk = short_conv(b_qk, conv_w_l, conv_b_l, grid)
    q_m, k_m = jnp.split(qk, 2, axis=-1)
    q_m = heads(q_m, ML_HEADS)
    k_m = heads(k_m, ML_HEADS) * (ML_DH ** -0.5)
    v_m = heads(b_v, ML_HEADS)
    g = b_g.reshape(bsz, t, 4, ML_HEADS) + gate_b_l.astype(jnp.float32)
    log_i_f, log_i_b = g[:, :, 0], g[:, :, 1]
    log_f_f, log_f_b = jax.nn.log_sigmoid(g[:, :, 2]), jax.nn.log_sigmoid(g[:, :, 3])
    h_f, ml_f = mlstm_scan(q_m, k_m, v_m, log_i_f, log_f_f, ml_f0)
    h_b, ml_b = mlstm_scan(flip(q_m), flip(k_m), flip(v_m), flip(log_i_b), flip(log_f_b), ml_b0)
    h = head_layer_norm(h_f + flip(h_b)) * ml_norm_l.astype(jnp.float32).reshape(ML_HEADS, ML_DH)
    y_b = h.reshape(bsz, t, W_B) * jax.nn.sigmoid(b_o) * jax.nn.silu(b_z)

    return jnp.concatenate([y_a, y_b], axis=-1), (hg_f, hg_b, ml_f, ml_b)


def setup_inputs(seed: int = 0) -> dict:
    key = jax.random.key(seed)
    ks = jax.random.split(key, 20)
    f32 = jnp.float32
    x = jax.random.normal(ks[0], (BATCH, SEQ, D_MODEL), f32)
    c = jax.random.normal(ks[1], (BATCH, D_MODEL), f32)
    ctx = jax.random.normal(ks[2], (BATCH, CTX_LEN, D_MODEL), f32)
    c_ctx = jax.random.normal(ks[3], (D_MODEL,), f32)
    w_mod = jax.random.normal(ks[4], (DEPTH, D_MODEL, 3 * D_MODEL), f32) * (0.5 * D_MODEL ** -0.5)
    b_mod = jax.random.normal(ks[5], (DEPTH, 3 * D_MODEL), f32) * 0.02
    w_in = jax.random.normal(ks[6], (DEPTH, D_MODEL, N_IN), f32) * (D_MODEL ** -0.5)
    conv_w = jax.random.normal(ks[7], (DEPTH, CONV_K, CONV_K, 2 * W_B), f32) * (1.0 / CONV_K)
    conv_b = jax.random.normal(ks[8], (DEPTH, 2 * W_B), f32) * 0.02
    hg_lb = jax.random.normal(ks[9], (2, DEPTH + 1, W_A), f32) * 0.1
    ig_b = jax.random.normal(ks[10], (DEPTH, 2, ML_HEADS), f32) * 0.1
    fg_b = jnp.linspace(3.0, 6.0, ML_HEADS, dtype=f32)[None, None, :] + 0.1 * jax.random.normal(ks[11], (DEPTH, 2, ML_HEADS), f32)
    ml_gate_b = jnp.concatenate([ig_b, fg_b], axis=1)
    hg_norm_w = 1.0 + 0.02 * jax.random.normal(ks[12], (DEPTH, W_A), f32)
    ml_norm_w = 1.0 + 0.02 * jax.random.normal(ks[13], (DEPTH, W_B), f32)
    w_out = jax.random.normal(ks[14], (DEPTH, D_INNER, D_MODEL), f32) * (BETA * D_INNER ** -0.5)
    ln_g = 1.0 + 0.02 * jax.random.normal(ks[15], (DEPTH, D_MODEL), f32)
    ln_b = 0.02 * jax.random.normal(ks[16], (DEPTH, D_MODEL), f32)
    return {"x": x, "c": c, "ctx": ctx, "c_ctx": c_ctx, "w_mod": w_mod, "b_mod": b_mod, "w_in": w_in,
            "conv_w": conv_w, "conv_b": conv_b, "hg_lb": hg_lb, "ml_gate_b": ml_gate_b,
            "hg_norm_w": hg_norm_w, "ml_norm_w": ml_norm_w, "w_out": w_out, "ln_g": ln_g, "ln_b": ln_b}


def reference(x, c, ctx, c_ctx, w_mod, b_mod, w_in, conv_w, conv_b, hg_lb, ml_gate_b,
              hg_norm_w, ml_norm_w, w_out, ln_g, ln_b):
    lower = jnp.cumsum(jax.nn.softmax(hg_lb.astype(jnp.float32), axis=1), axis=1)
    states0 = zero_states(ctx.shape[0])
    for layer in range(DEPTH):
        mod_x = jax.nn.silu(c) @ w_mod[layer] + b_mod[layer]
        mod_c = jax.nn.silu(c_ctx) @ w_mod[layer] + b_mod[layer]
        shift_x, scale_x, gate_x = jnp.split(mod_x[:, None, :], 3, axis=-1)
        shift_c, scale_c, gate_c = jnp.split(mod_c, 3, axis=-1)
        lp = (lower[:, layer], conv_w[layer], conv_b[layer], ml_gate_b[layer], hg_norm_w[layer], ml_norm_w[layer])
        u_ctx = modulate(ctx, shift_c, scale_c) @ w_in[layer]
        y_ctx, ctx_states = mix(u_ctx, False, states0, *lp)
        u_x = modulate(x, shift_x, scale_x) @ w_in[layer]
        y_x, _ = mix(u_x, True, ctx_states, *lp)
        x = layer_norm(ALPHA * x + gate_x * (y_x.astype(x.dtype) @ w_out[layer]), ln_g[layer], ln_b[layer])
        if layer < DEPTH - 1:
            ctx = layer_norm(ALPHA * ctx + gate_c * (y_ctx.astype(ctx.dtype) @ w_out[layer]), ln_g[layer], ln_b[layer])
    return x
```

```python
import functools

import jax
import jax.numpy as jnp
from jax import lax
from jax.experimental import pallas as pl
from jax.experimental.pallas import tpu as pltpu

F32 = jnp.float32
BF16 = jnp.bfloat16

LN_EPS = 1e-5
NORM_EPS = 1e-6
DEPTH = 1
ALPHA = (2 * DEPTH) ** 0.25

HG_DK = 128
ML_DH = 256
GRID_W = 64
CONV_K = 3

LANES = 128
HG_CHUNK = 64
HG_SUB = 16
HG_HEADS_PER_STEP = 2
ML_CHUNK = 256
GATE_ROWS = 16
COL_BLOCK = 256
VMEM_LIMIT = 56 * 1024 * 1024

_NT = (((1,), (1,)), ((), ()))
_TN = (((0,), (0,)), ((), ()))


def _params(n_grid):
    return pltpu.CompilerParams(dimension_semantics=("arbitrary",) * n_grid,
                                vmem_limit_bytes=VMEM_LIMIT)


def _silu(a):
    return a * jax.nn.sigmoid(a)


def _log_sigmoid(a):
    return jnp.minimum(a, 0.0) - jnp.log1p(jnp.exp(-jnp.abs(a)))


def _split3(a):
    hi = a.astype(BF16)
    r1 = a - hi.astype(F32)
    mid = r1.astype(BF16)
    lo = (r1 - mid.astype(F32)).astype(BF16)
    return hi, mid, lo


def _tri_left(tri, a):
    return sum(jnp.dot(tri, t, preferred_element_type=F32) for t in _split3(a))


def _tri_right(a, tri):
    return sum(jnp.dot(t, tri, preferred_element_type=F32) for t in _split3(a))


def _tri_masks(n):
    r = lax.broadcasted_iota(jnp.int32, (n, n), 0)
    c = lax.broadcasted_iota(jnp.int32, (n, n), 1)
    return r >= c, r <= c


def _mod_kernel(c_ref, w_ref, b_ref, o_ref):
    s = _silu(c_ref[...])
    o_ref[...] = jnp.dot(s, w_ref[...], precision=lax.Precision.HIGHEST,
                         preferred_element_type=F32) + b_ref[...]


def _modulation(cc, w_mod, b_mod):
    rows, d = cc.shape
    n = w_mod.shape[1]
    tn = 512
    return pl.pallas_call(
        _mod_kernel,
        out_shape=jax.ShapeDtypeStruct((rows, n), F32),
        grid=(n // tn,),
        in_specs=[pl.BlockSpec((rows, d), lambda j: (0, 0)),
                  pl.BlockSpec((d, tn), lambda j: (0, j)),
                  pl.BlockSpec((1, tn), lambda j: (0, j))],
        out_specs=pl.BlockSpec((rows, tn), lambda j: (0, j)),
        compiler_params=_params(1),
        name="mod",
    )(cc, w_mod, b_mod.reshape(1, n))


def _inproj_kernel(x_ref, sh_ref, sc_ref, w_ref, wg_ref, wgt_ref, u_ref, g_ref, gt_ref, h_ref, *, sub):
    @pl.when(pl.program_id(1) == 0)
    def _():
        shift = sh_ref[...]
        scale1 = 1.0 + sc_ref[...]

        def body(r, carry):
            rows = pl.ds(pl.multiple_of(r * sub, sub), sub)
            xv = x_ref[rows, :]
            mu = jnp.mean(xv, axis=-1, keepdims=True)
            xc = xv - mu
            var = jnp.mean(xc * xc, axis=-1, keepdims=True)
            h_ref[rows, :] = (xc * lax.rsqrt(var + LN_EPS) * scale1 + shift).astype(BF16)
            return carry

        lax.fori_loop(0, x_ref.shape[0] // sub, body, 0)
        hb = h_ref[...]
        g_ref[...] = jnp.dot(hb, wg_ref[...], preferred_element_type=F32)
        gt_ref[...] = lax.dot_general(wgt_ref[...], hb, _NT, preferred_element_type=F32)

    u_ref[...] = jnp.dot(h_ref[...], w_ref[...], preferred_element_type=F32).astype(BF16)


def _inproj(x2, mod3, mod_row, w_main, w_g, w_gt, tm):
    rows, d = x2.shape
    nu = w_main.shape[1]
    tn = 512
    sub = min(256, tm)
    return pl.pallas_call(
        functools.partial(_inproj_kernel, sub=sub),
        out_shape=(jax.ShapeDtypeStruct((rows, nu), BF16),
                   jax.ShapeDtypeStruct((rows, LANES), F32),
                   jax.ShapeDtypeStruct((LANES, rows), F32)),
        grid=(rows // tm, nu // tn),
        in_specs=[pl.BlockSpec((tm, d), lambda i, j: (i, 0)),
                  pl.BlockSpec((None, 1, d), lambda i, j: (mod_row(i), 0, 0)),
                  pl.BlockSpec((None, 1, d), lambda i, j: (mod_row(i), 0, 1)),
                  pl.BlockSpec((d, tn), lambda i, j: (0, j)),
                  pl.BlockSpec((d, LANES), lambda i, j: (0, 0)),
                  pl.BlockSpec((LANES, d), lambda i, j: (0, 0))],
        out_specs=(pl.BlockSpec((tm, tn), lambda i, j: (i, j)),
                   pl.BlockSpec((tm, LANES), lambda i, j: (i, 0)),
                   pl.BlockSpec((LANES, tm), lambda i, j: (0, i))),
        scratch_shapes=[pltpu.VMEM((tm, d), BF16)],
        compiler_params=_params(2),
        name="inproj",
    )(x2, mod3, mod3, w_main, w_g, w_gt)


def _gateprep_kernel(g_ref, gt_ref, bias_ref, biast_ref, gc_ref, gr_ref, *, nh):
    c = g_ref.shape[0]
    lower, upper = _tri_masks(c)
    tril = jnp.where(lower, 1.0, 0.0).astype(BF16)
    triu = jnp.where(upper, 1.0, 0.0).astype(BF16)

    pre = g_ref[...] + bias_ref[...]
    lf = _log_sigmoid(pre)
    lane = lax.broadcasted_iota(jnp.int32, pre.shape, 1)
    fwd_f = (lane >= 2 * nh) & (lane < 3 * nh)
    bwd_f = (lane >= 3 * nh) & (lane < 4 * nh)
    b = jnp.where(fwd_f, _tri_left(tril, lf), jnp.where(bwd_f, _tri_left(triu, lf), 0.0))
    a = pre - pltpu.roll(b, LANES - 2 * nh, axis=1)
    gc_ref[...] = jnp.where(lane < 2 * nh, a, b)

    pre_t = gt_ref[...] + biast_ref[...]
    lf_t = _log_sigmoid(pre_t)
    row = lax.broadcasted_iota(jnp.int32, pre_t.shape, 0)
    fwd_r = (row >= 2 * nh) & (row < 3 * nh)
    bwd_r = (row >= 3 * nh) & (row < 4 * nh)
    b_t = jnp.where(fwd_r, _tri_right(lf_t, triu), jnp.where(bwd_r, _tri_right(lf_t, tril), 0.0))
    a_t = pre_t - pltpu.roll(b_t, GATE_ROWS - 2 * nh, axis=0)
    gr_ref[...] = jnp.where(row < 2 * nh, a_t, b_t)


def _gateprep(g, gt, bias, nh, bsz, t, chunk):
    rows = bsz * t
    bias_row = jnp.zeros((1, LANES), F32).at[0, :4 * nh].set(bias.reshape(-1))
    bias_col = jnp.zeros((GATE_ROWS, 1), F32).at[:4 * nh, 0].set(bias.reshape(-1))
    gc, gr = pl.pallas_call(
        functools.partial(_gateprep_kernel, nh=nh),
        out_shape=(jax.ShapeDtypeStruct((rows, LANES), F32),
                   jax.ShapeDtypeStruct((GATE_ROWS, rows), F32)),
        grid=(rows // chunk,),
        in_specs=[pl.BlockSpec((chunk, LANES), lambda i: (i, 0)),
                  pl.BlockSpec((GATE_ROWS, chunk), lambda i: (0, i)),
                  pl.BlockSpec((1, LANES), lambda i: (0, 0)),
                  pl.BlockSpec((GATE_ROWS, 1), lambda i: (0, 0))],
        out_specs=(pl.BlockSpec((chunk, LANES), lambda i: (i, 0)),
                   pl.BlockSpec((GATE_ROWS, chunk), lambda i: (0, i))),
        compiler_params=_params(1),
        name="gateprep",
    )(g, gt, bias_row, bias_col)
    gcol = gc[:, :4 * nh].reshape(bsz, t, 4, nh).transpose(0, 3, 1, 2)
    gcol = jnp.pad(gcol, ((0, 0), (0, 0), (0, 0), (0, 4)))
    grow = gr[:4 * nh].reshape(4, nh, bsz, t).transpose(2, 1, 0, 3)
    grow = jnp.pad(grow, ((0, 0), (0, 0), (0, 4), (0, 0)))
    return gcol, grow


def _conv_kernel(x_ref, w_ref, b_ref, o_ref, *, rows, width, taps):
    cb = x_ref.shape[-1]
    j = pl.program_id(1)
    scale = jnp.where(j >= pl.num_programs(1) // 2, ML_DH ** -0.5, 1.0).astype(F32)
    w = w_ref[...]
    bias = b_ref[...]
    col = lax.broadcasted_iota(jnp.int32, (width, cb), 0)
    first = col == 0
    last = col == width - 1

    def one_row(r):
        left = jnp.zeros((width, cb), F32)
        mid = jnp.zeros((width, cb), F32)
        right = jnp.zeros((width, cb), F32)
        for ki in taps:
            rr = r + (ki - 1)
            valid = jnp.where((rr >= 0) & (rr < rows), 1.0, 0.0).astype(F32)
            start = pl.multiple_of(jnp.clip(rr, 0, rows - 1) * width, width)
            xv = x_ref[pl.ds(start, width), :].astype(F32) * valid
            left = left + xv * w[3 * ki + 0:3 * ki + 1, :]
            mid = mid + xv * w[3 * ki + 1:3 * ki + 2, :]
            right = right + xv * w[3 * ki + 2:3 * ki + 3, :]
        out = (mid + jnp.where(first, 0.0, pltpu.roll(left, 1, axis=0))
               + jnp.where(last, 0.0, pltpu.roll(right, width - 1, axis=0)) + bias)
        dst = pl.multiple_of(r * width, width)
        o_ref[pl.ds(dst, width), :] = (_silu(out) * scale).astype(o_ref.dtype)

    def body(r, carry):
        one_row(r)
        return carry

    lax.fori_loop(0, rows, body, 0)


def _short_conv(u, col0, n_ch, conv_w, conv_b, grid):
    bsz, t, _ = u.shape
    cb = COL_BLOCK
    rows, width, taps = (t // GRID_W, GRID_W, (0, 1, 2)) if grid else (1, t, (1,))
    return pl.pallas_call(
        functools.partial(_conv_kernel, rows=rows, width=width, taps=taps),
        out_shape=jax.ShapeDtypeStruct((bsz, t, n_ch), BF16),
        grid=(bsz, n_ch // cb),
        in_specs=[pl.BlockSpec((None, t, cb), lambda b, j: (b, 0, col0 // cb + j)),
                  pl.BlockSpec((CONV_K * CONV_K, cb), lambda b, j: (0, j)),
                  pl.BlockSpec((1, cb), lambda b, j: (0, j))],
        out_specs=pl.BlockSpec((None, t, cb), lambda b, j: (b, 0, j)),
        compiler_params=_params(2),
        name="conv_grid" if grid else "conv_seq",
    )(u, conv_w.reshape(CONV_K * CONV_K, n_ch).astype(F32), conv_b.reshape(1, n_ch).astype(F32))


def _hg_step(q, z, v, lb, st_ref, b_sc, k_sc, fwd, need_out):
    c, kdim = q.shape
    lower, upper = _tri_masks(c)
    tri = jnp.where(lower if fwd else upper, 1.0, 0.0).astype(BF16)
    f = lb + (1.0 - lb) * jax.nn.sigmoid(z)
    logf = jnp.log(f)
    kk = 1.0 - f
    b = _tri_left(tri, logf)
    end = c - 1 if fwd else 0
    b_end = b[end:end + 1, :]
    st = st_ref[...]
    k_dec = kk * jnp.exp(b_end - b)
    st_ref[...] = st * jnp.exp(b_end) + lax.dot_general(
        v, k_dec.astype(BF16), _TN, preferred_element_type=F32)
    if not need_out:
        return None

    qe = q * jnp.exp(b)
    o = lax.dot_general(qe.astype(BF16), st.astype(BF16), _NT, preferred_element_type=F32)

    b_sc[...] = b
    k_sc[...] = kk
    n_sub = c // HG_SUB
    sub_row = lax.broadcasted_iota(jnp.int32, (HG_SUB, kdim), 0)
    lane = lax.broadcasted_iota(jnp.int32, (HG_SUB, LANES), 1)
    lane_c = lax.broadcasted_iota(jnp.int32, (HG_SUB, c), 1)
    ones = jnp.ones((kdim, LANES), BF16)
    blocks = []
    for i in range(n_sub):
        r0 = i * HG_SUB
        qb = q[r0:r0 + HG_SUB]
        bb = b[r0:r0 + HG_SUB]
        parts = []
        for s in range(HG_SUB):
            bs = b_sc[pl.ds(r0 + s, 1), :]
            ks = k_sc[pl.ds(r0 + s, 1), :]
            keep = (sub_row >= s) if fwd else (sub_row <= s)
            parts.append(jnp.where(keep, qb * ks * jnp.exp(bb - bs), 0.0).astype(BF16))
        red = jnp.dot(jnp.concatenate(parts, axis=0), ones, preferred_element_type=F32)
        diag = jnp.zeros((HG_SUB, LANES), F32)
        for s in range(HG_SUB):
            diag = jnp.where(lane == r0 + s, red[s * HG_SUB:(s + 1) * HG_SUB], diag)
        blk = diag[:, :c]
        if (i > 0) if fwd else (i < n_sub - 1):
            edge = r0 - 1 if fwd else r0 + HG_SUB
            beta = b_sc[pl.ds(edge, 1), :]
            qh = qb * jnp.exp(bb - beta)
            kh = kk * jnp.exp(jnp.minimum(beta - b, 0.0))
            off = lax.dot_general(qh.astype(BF16), kh.astype(BF16), _NT, preferred_element_type=F32)
            far = (lane_c < r0) if fwd else (lane_c >= r0 + HG_SUB)
            blk = jnp.where(far, off, blk)
        blocks.append(blk)
    a = jnp.concatenate(blocks, axis=0).astype(BF16)
    return o + jnp.dot(a, v, preferred_element_type=F32)


def _hgrn2_kernel(cq, cff, cfb, ci, aq, aff, afb, ai, az, lb_ref, nw_ref, y_ref,
                  acc, st_ref, b_sc, k_sc, *, hp, slots):
    c = HG_CHUNK
    ncc = cq.shape[0] // c
    nc = aq.shape[0] // c
    st_ref[...] = jnp.zeros_like(st_ref)
    nw = nw_ref[...]
    lbs = []
    for d in range(2):
        hg = lb_ref[d * slots:(d + 1) * slots, :]
        e = jnp.exp(hg - jnp.max(hg, axis=0, keepdims=True))
        lbs.append(e[0:1, :] / jnp.sum(e, axis=0, keepdims=True))

    def run(refs, ic_f, ic_b, mode):
        qr, ffr, fbr, ir, zr = refs
        for h in range(hp):
            ls = slice(h * HG_DK, (h + 1) * HG_DK)
            for d, ic in ((0, ic_f), (1, ic_b)):
                rows = pl.ds(pl.multiple_of(ic * c, c), c)
                q = _silu(qr[rows, ls].astype(F32))
                z = (ffr if d == 0 else fbr)[rows, ls].astype(F32)
                v = ir[rows, ls]
                slot = 2 * h + d
                o = _hg_step(q, z, v, lbs[d][:, ls], st_ref.at[slot], b_sc.at[slot], k_sc.at[slot],
                             fwd=(d == 0), need_out=(mode != "state"))
                if mode == "store":
                    acc[rows, ls] = o
                elif mode == "final":
                    tot = acc[rows, ls] + o
                    ms = jnp.mean(tot * tot, axis=-1, keepdims=True)
                    gate = _silu(zr[rows, ls].astype(F32))
                    y_ref[rows, ls] = (tot * lax.rsqrt(ms + NORM_EPS) * nw[:, ls] * gate).astype(y_ref.dtype)

    def loop(lo, hi, refs, n, mode):
        def body(i, carry):
            run(refs, i, n - 1 - i, mode)
            return carry
        lax.fori_loop(lo, hi, body, 0)

    loop(0, ncc, (cq, cff, cfb, ci, None), ncc, "state")
    loop(0, nc // 2, (aq, aff, afb, ai, az), nc, "store")
    loop(nc // 2, nc, (aq, aff, afb, ai, az), nc, "final")


def _hgrn2(u_x, u_c, w_a, hg_lb, hg_norm):
    bsz, t, _ = u_x.shape
    tc = u_c.shape[1]
    hp = HG_HEADS_PER_STEP
    cb = hp * HG_DK
    assert cb == COL_BLOCK and (t // HG_CHUNK) % 2 == 0
    seg = w_a // cb
    slots = hg_lb.shape[1]
    lb2 = hg_lb.reshape(2 * slots, w_a).astype(F32)

    def col(k):
        return lambda b, p: (b, 0, k * seg + p)

    ctx_specs = [pl.BlockSpec((None, tc, cb), col(k)) for k in (0, 1, 2, 3)]
    lat_specs = [pl.BlockSpec((None, t, cb), col(k)) for k in (0, 1, 2, 3, 4)]
    return pl.pallas_call(
        functools.partial(_hgrn2_kernel, hp=hp, slots=slots),
        out_shape=jax.ShapeDtypeStruct((bsz, t, w_a), BF16),
        grid=(bsz, seg),
        in_specs=ctx_specs + lat_specs + [
            pl.BlockSpec((2 * slots, cb), lambda b, p: (0, p)),
            pl.BlockSpec((1, cb), lambda b, p: (0, p))],
        out_specs=pl.BlockSpec((None, t, cb), lambda b, p: (b, 0, p)),
        scratch_shapes=[pltpu.VMEM((t, cb), F32),
                        pltpu.VMEM((2 * hp, HG_DK, HG_DK), F32),
                        pltpu.VMEM((2 * hp, HG_CHUNK, HG_DK), F32),
                        pltpu.VMEM((2 * hp, HG_CHUNK, HG_DK), F32)],
        compiler_params=_params(2),
        name="hgrn2",
    )(u_c, u_c, u_c, u_c, u_x, u_x, u_x, u_x, u_x, lb2, hg_norm.reshape(1, w_a).astype(F32))


def _ml_step(q, k, v, gcol, grow, ct_ref, n_ref, m_prev, fwd, need_out):
    c = q.shape[0]
    lower, upper = _tri_masks(c)
    mask = lower if fwd else upper
    ia, ib = (0, 2) if fwd else (1, 3)
    a_col = gcol[:, ia:ia + 1]
    b_col = gcol[:, ib:ib + 1]
    a_row = grow[ia:ia + 1, :]
    end = c - 1 if fwd else 0
    b_end = b_col[end:end + 1, :]
    neg = -jnp.inf
    a_max = jnp.max(jnp.where(mask, a_row, neg), axis=1, keepdims=True)
    m_t = b_col + jnp.maximum(m_prev, a_max)
    m_new = m_t[end:end + 1, :]
    w_s = jnp.exp(b_end + a_col - m_new)
    decay = jnp.exp(b_end + m_prev - m_new)
    ct = ct_ref[...]
    n = n_ref[...]
    kw = k.astype(F32) * w_s
    ct_ref[...] = decay * ct + lax.dot_general(kw.astype(BF16), v, _TN, preferred_element_type=F32)
    n_ref[...] = decay * n + jnp.sum(kw, axis=0, keepdims=True)
    if not need_out:
        return m_new, None

    w_inter = jnp.exp(b_col + m_prev - m_t)
    dm = jnp.exp(jnp.where(mask, (b_col - m_t) + a_row, neg))
    w_qk = dm * lax.dot_general(q, k, _NT, preferred_element_type=F32)
    num = (w_inter * jnp.dot(q, ct.astype(BF16), preferred_element_type=F32)
           + jnp.dot(w_qk.astype(BF16), v, preferred_element_type=F32))
    den = (w_inter * jnp.sum(q.astype(F32) * n, axis=1, keepdims=True)
           + jnp.sum(w_qk, axis=1, keepdims=True))
    return m_new, num / jnp.maximum(jnp.abs(den), jnp.exp(-m_t))


def _mlstm_kernel(cq, ck, cv, cgc, cgr, xq, xk, xv, xo, xz, xgc, xgr, nw_ref, y_ref,
                  acc, ct_ref, n_ref):
    c = ML_CHUNK
    ncc = cq.shape[0] // c
    nc = xq.shape[0] // c
    ct_ref[...] = jnp.zeros_like(ct_ref)
    n_ref[...] = jnp.zeros_like(n_ref)
    nw = nw_ref[...]

    def run(refs, ics, ms, mode):
        qr, kr, vr, gcr, grr = refs
        m_out = []
        for d in range(2):
            rows = pl.ds(pl.multiple_of(ics[d] * c, c), c)
            m_new, h = _ml_step(qr[rows, :], kr[rows, :], vr[rows, :], gcr[rows, :], grr[:, rows],
                                ct_ref.at[d], n_ref.at[d], ms[d], fwd=(d == 0),
                                need_out=(mode != "state"))
            m_out.append(m_new)
            if mode == "store":
                acc[rows, :] = h
            elif mode == "final":
                tot = acc[rows, :] + h
                mu = jnp.mean(tot, axis=-1, keepdims=True)
                tc_ = tot - mu
                var = jnp.mean(tc_ * tc_, axis=-1, keepdims=True)
                gate = jax.nn.sigmoid(xo[rows, :].astype(F32)) * _silu(xz[rows, :].astype(F32))
                y_ref[rows, :] = (tc_ * lax.rsqrt(var + NORM_EPS) * nw * gate).astype(y_ref.dtype)
        return tuple(m_out)

    def loop(lo, hi, refs, n, mode, ms):
        return lax.fori_loop(lo, hi, lambda i, m: run(refs, (i, n - 1 - i), m, mode), ms)

    ms = (jnp.zeros((1, 1), F32), jnp.zeros((1, 1), F32))
    ms = loop(0, ncc, (cq, ck, cv, cgc, cgr), ncc, "state", ms)
    ms = loop(0, nc // 2, (xq, xk, xv, xgc, xgr), nc, "store", ms)
    loop(nc // 2, nc, (xq, xk, xv, xgc, xgr), nc, "final", ms)


def _mlstm(qk_x, qk_c, u_x, u_c, v_col, o_col, z_col, gates_x, gates_c, ml_norm, w_b):
    bsz, t, _ = u_x.shape
    tc = u_c.shape[1]
    nh = w_b // ML_DH
    cb = ML_DH
    assert cb == COL_BLOCK and (t // ML_CHUNK) % 2 == 0 and tc % ML_CHUNK == 0

    def col(k):
        return lambda b, h: (b, 0, k + h)

    def seq_specs(n, with_gates):
        specs = [pl.BlockSpec((None, n, cb), col(0)),
                 pl.BlockSpec((None, n, cb), col(nh)),
                 pl.BlockSpec((None, n, cb), col(v_col // cb))]
        if with_gates:
            specs += [pl.BlockSpec((None, n, cb), col(o_col // cb)),
                      pl.BlockSpec((None, n, cb), col(z_col // cb))]
        specs += [pl.BlockSpec((None, None, n, 8), lambda b, h: (b, h, 0, 0)),
                  pl.BlockSpec((None, None, 8, n), lambda b, h: (b, h, 0, 0))]
        return specs

    return pl.pallas_call(
        _mlstm_kernel,
        out_shape=jax.ShapeDtypeStruct((bsz, t, w_b), BF16),
        grid=(bsz, nh),
        in_specs=seq_specs(tc, False) + seq_specs(t, True) + [pl.BlockSpec((1, cb), lambda b, h: (0, h))],
        out_specs=pl.BlockSpec((None, t, cb), lambda b, h: (b, 0, h)),
        scratch_shapes=[pltpu.VMEM((t, cb), F32),
                        pltpu.VMEM((2, ML_DH, ML_DH), F32),
                        pltpu.VMEM((2, 1, ML_DH), F32)],
        compiler_params=_params(2),
        name="mlstm",
    )(qk_c, qk_c, u_c, gates_c[0], gates_c[1],
      qk_x, qk_x, u_x, u_x, u_x, gates_x[0], gates_x[1],
      ml_norm.reshape(1, w_b).astype(F32))


def _outproj_kernel(ya_ref, yb_ref, x_ref, gate_ref, wa_ref, wb_ref, g_ref, b_ref, o_ref):
    y = (jnp.dot(ya_ref[...], wa_ref[...], preferred_element_type=F32)
         + jnp.dot(yb_ref[...], wb_ref[...], preferred_element_type=F32))
    r = ALPHA * x_ref[...] + gate_ref[...] * y
    mu = jnp.mean(r, axis=-1, keepdims=True)
    rc = r - mu
    var = jnp.mean(rc * rc, axis=-1, keepdims=True)
    o_ref[...] = rc * lax.rsqrt(var + LN_EPS) * g_ref[...] + b_ref[...]


def _outproj(y_a, y_b, x2, mod3, t, w_oa, w_ob, ln_g, ln_b):
    rows, d = x2.shape
    tm = 256
    wa, wb = y_a.shape[1], y_b.shape[1]
    return pl.pallas_call(
        _outproj_kernel,
        out_shape=jax.ShapeDtypeStruct((rows, d), F32),
        grid=(rows // tm,),
        in_specs=[pl.BlockSpec((tm, wa), lambda i: (i, 0)),
                  pl.BlockSpec((tm, wb), lambda i: (i, 0)),
                  pl.BlockSpec((tm, d), lambda i: (i, 0)),
                  pl.BlockSpec((None, 1, d), lambda i: ((i * tm) // t, 0, 2)),
                  pl.BlockSpec((wa, d), lambda i: (0, 0)),
                  pl.BlockSpec((wb, d), lambda i: (0, 0)),
                  pl.BlockSpec((1, d), lambda i: (0, 0)),
                  pl.BlockSpec((1, d), lambda i: (0, 0))],
        out_specs=pl.BlockSpec((tm, d), lambda i: (i, 0)),
        compiler_params=_params(1),
        name="outproj",
    )(y_a, y_b, x2, mod3, w_oa, w_ob, ln_g.reshape(1, d).astype(F32), ln_b.reshape(1, d).astype(F32))


def kernel(x, c, ctx, c_ctx, w_mod, b_mod, w_in, conv_w, conv_b, hg_lb, ml_gate_b, hg_norm_w,
           ml_norm_w, w_out, ln_g, ln_b):
    bsz, t, d = x.shape
    tc = ctx.shape[1]
    assert w_in.shape[0] == DEPTH and hg_lb.shape[1] == DEPTH + 1
    d_inner = w_out.shape[1]
    w_a = hg_lb.shape[2]
    w_b = d_inner - w_a
    nh = ml_gate_b.shape[2]
    nu = 5 * w_a + 5 * w_b
    assert nh * ML_DH == w_b and w_in.shape[2] == nu + 4 * nh and t % GRID_W == 0

    mod_rows = 8 * ((bsz + 1 + 7) // 8)
    cc = jnp.zeros((mod_rows, d), F32).at[:bsz].set(c).at[bsz].set(c_ctx)
    mod3 = _modulation(cc, w_mod[0], b_mod[0]).reshape(mod_rows, 1, 3 * d)

    w_main = w_in[0][:, :nu].astype(BF16)
    w_g = jnp.pad(w_in[0][:, nu:], ((0, 0), (0, LANES - 4 * nh))).astype(BF16)
    tm = min(1024, t)
    u_x, g_x, gt_x = _inproj(x.reshape(bsz * t, d), mod3, lambda i: (i * tm) // t, w_main, w_g, w_g.T, tm)
    u_c, g_c, gt_c = _inproj(ctx.reshape(bsz * tc, d), mod3, lambda i: bsz, w_main, w_g, w_g.T,
                             min(1024, bsz * tc))
    u_x = u_x.reshape(bsz, t, nu)
    u_c = u_c.reshape(bsz, tc, nu)

    gates_x = _gateprep(g_x, gt_x[:GATE_ROWS], ml_gate_b[0], nh, bsz, t, ML_CHUNK)
    gates_c = _gateprep(g_c, gt_c[:GATE_ROWS], ml_gate_b[0], nh, bsz, tc, ML_CHUNK)

    qk_col = 5 * w_a
    qk_x = _short_conv(u_x, qk_col, 2 * w_b, conv_w[0], conv_b[0], grid=True)
    qk_c = _short_conv(u_c, qk_col, 2 * w_b, conv_w[0], conv_b[0], grid=False)

    y_a = _hgrn2(u_x, u_c, w_a, hg_lb, hg_norm_w[0])
    y_b = _mlstm(qk_x, qk_c, u_x, u_c, qk_col + 2 * w_b, qk_col + 3 * w_b, qk_col + 4 * w_b,
                 gates_x, gates_c, ml_norm_w[0], w_b)

    w_o = w_out[0].astype(BF16)
    out = _outproj(y_a.reshape(bsz * t, w_a), y_b.reshape(bsz * t, w_b), x.reshape(bsz * t, d), mod3, t,
                   w_o[:w_a], w_o[w_a:], ln_g[0], ln_b[0])
    return out.reshape(bsz, t, d)
```

```python
import functools

import jax
import jax.numpy as jnp
from jax import lax
from jax.experimental import pallas as pl
from jax.experimental.pallas import tpu as pltpu

F32 = jnp.float32
BF16 = jnp.bfloat16

LN_EPS = 1e-5
NORM_EPS = 1e-6
DEPTH = 1
ALPHA = (2 * DEPTH) ** 0.25

HG_DK = 128
ML_DH = 256
GRID_W = 64
CONV_K = 3

LANES = 128
HG_CHUNK = 64
HG_SUB = 8
LOG2E = 1.4426950408889634
ML_CHUNK = 256
GATE_ROWS = 16
COL_BLOCK = 256
VMEM_LIMIT = 56 * 1024 * 1024

_NT = (((1,), (1,)), ((), ()))
_TN = (((0,), (0,)), ((), ()))


def _params(n_grid):
    return pltpu.CompilerParams(dimension_semantics=("arbitrary",) * n_grid,
                                vmem_limit_bytes=VMEM_LIMIT)


def _silu(a):
    return a * jax.nn.sigmoid(a)


def _log_sigmoid(a):
    return jnp.minimum(a, 0.0) - jnp.log1p(jnp.exp(-jnp.abs(a)))


def _split3(a):
    hi = a.astype(BF16)
    r1 = a - hi.astype(F32)
    mid = r1.astype(BF16)
    lo = (r1 - mid.astype(F32)).astype(BF16)
    return hi, mid, lo


def _tri_left(tri, a):
    return sum(jnp.dot(tri, t, preferred_element_type=F32) for t in _split3(a))


def _tri_right(a, tri):
    return sum(jnp.dot(t, tri, preferred_element_type=F32) for t in _split3(a))


def _tri_masks(n):
    r = lax.broadcasted_iota(jnp.int32, (n, n), 0)
    c = lax.broadcasted_iota(jnp.int32, (n, n), 1)
    return r >= c, r <= c


def _mod_kernel(c_ref, w_ref, b_ref, o_ref):
    s = _silu(c_ref[...])
    o_ref[...] = jnp.dot(s, w_ref[...], precision=lax.Precision.HIGHEST,
                         preferred_element_type=F32) + b_ref[...]


def _modulation(cc, w_mod, b_mod):
    rows, d = cc.shape
    n = w_mod.shape[1]
    tn = 512
    return pl.pallas_call(
        _mod_kernel,
        out_shape=jax.ShapeDtypeStruct((rows, n), F32),
        grid=(n // tn,),
        in_specs=[pl.BlockSpec((rows, d), lambda j: (0, 0)),
                  pl.BlockSpec((d, tn), lambda j: (0, j)),
                  pl.BlockSpec((1, tn), lambda j: (0, j))],
        out_specs=pl.BlockSpec((rows, tn), lambda j: (0, j)),
        compiler_params=_params(1),
        name="mod",
    )(cc, w_mod, b_mod.reshape(1, n))


def _inproj_kernel(x_ref, sh_ref, sc_ref, w_ref, wg_ref, wgt_ref, u_ref, g_ref, gt_ref, h_ref, *, sub):
    @pl.when(pl.program_id(1) == 0)
    def _():
        shift = sh_ref[...]
        scale1 = 1.0 + sc_ref[...]

        def body(r, carry):
            rows = pl.ds(pl.multiple_of(r * sub, sub), sub)
            xv = x_ref[rows, :]
            mu = jnp.mean(xv, axis=-1, keepdims=True)
            xc = xv - mu
            var = jnp.mean(xc * xc, axis=-1, keepdims=True)
            h_ref[rows, :] = (xc * lax.rsqrt(var + LN_EPS) * scale1 + shift).astype(BF16)
            return carry

        lax.fori_loop(0, x_ref.shape[0] // sub, body, 0)
        hb = h_ref[...]
        g_ref[...] = jnp.dot(hb, wg_ref[...], preferred_element_type=F32)
        gt_ref[...] = lax.dot_general(wgt_ref[...], hb, _NT, preferred_element_type=F32)

    u_ref[...] = jnp.dot(h_ref[...], w_ref[...], preferred_element_type=F32).astype(BF16)


def _inproj(x2, mod3, mod_row, w_main, w_g, w_gt, tm):
    rows, d = x2.shape
    nu = w_main.shape[1]
    tn = 512
    sub = min(256, tm)
    return pl.pallas_call(
        functools.partial(_inproj_kernel, sub=sub),
        out_shape=(jax.ShapeDtypeStruct((rows, nu), BF16),
                   jax.ShapeDtypeStruct((rows, LANES), F32),
                   jax.ShapeDtypeStruct((LANES, rows), F32)),
        grid=(rows // tm, nu // tn),
        in_specs=[pl.BlockSpec((tm, d), lambda i, j: (i, 0)),
                  pl.BlockSpec((None, 1, d), lambda i, j: (mod_row(i), 0, 0)),
                  pl.BlockSpec((None, 1, d), lambda i, j: (mod_row(i), 0, 1)),
                  pl.BlockSpec((d, tn), lambda i, j: (0, j)),
                  pl.BlockSpec((d, LANES), lambda i, j: (0, 0)),
                  pl.BlockSpec((LANES, d), lambda i, j: (0, 0))],
        out_specs=(pl.BlockSpec((tm, tn), lambda i, j: (i, j)),
                   pl.BlockSpec((tm, LANES), lambda i, j: (i, 0)),
                   pl.BlockSpec((LANES, tm), lambda i, j: (0, i))),
        scratch_shapes=[pltpu.VMEM((tm, d), BF16)],
        compiler_params=_params(2),
        name="inproj",
    )(x2, mod3, mod3, w_main, w_g, w_gt)


def _gateprep_kernel(g_ref, gt_ref, bias_ref, biast_ref, gc_ref, gr_ref, *, nh):
    c = g_ref.shape[0]
    lower, upper = _tri_masks(c)
    tril = jnp.where(lower, 1.0, 0.0).astype(BF16)
    triu = jnp.where(upper, 1.0, 0.0).astype(BF16)

    pre = g_ref[...] + bias_ref[...]
    lf = _log_sigmoid(pre)
    lane = lax.broadcasted_iota(jnp.int32, pre.shape, 1)
    fwd_f = (lane >= 2 * nh) & (lane < 3 * nh)
    bwd_f = (lane >= 3 * nh) & (lane < 4 * nh)
    b = jnp.where(fwd_f, _tri_left(tril, lf), jnp.where(bwd_f, _tri_left(triu, lf), 0.0))
    a = pre - pltpu.roll(b, LANES - 2 * nh, axis=1)
    gc_ref[...] = jnp.where(lane < 2 * nh, a, b)

    pre_t = gt_ref[...] + biast_ref[...]
    lf_t = _log_sigmoid(pre_t)
    row = lax.broadcasted_iota(jnp.int32, pre_t.shape, 0)
    fwd_r = (row >= 2 * nh) & (row < 3 * nh)
    bwd_r = (row >= 3 * nh) & (row < 4 * nh)
    b_t = jnp.where(fwd_r, _tri_right(lf_t, triu), jnp.where(bwd_r, _tri_right(lf_t, tril), 0.0))
    a_t = pre_t - pltpu.roll(b_t, GATE_ROWS - 2 * nh, axis=0)
    gr_ref[...] = jnp.where(row < 2 * nh, a_t, b_t)


def _gateprep(g, gt, bias, nh, bsz, t, chunk):
    rows = bsz * t
    bias_row = jnp.zeros((1, LANES), F32).at[0, :4 * nh].set(bias.reshape(-1))
    bias_col = jnp.zeros((GATE_ROWS, 1), F32).at[:4 * nh, 0].set(bias.reshape(-1))
    gc, gr = pl.pallas_call(
        functools.partial(_gateprep_kernel, nh=nh),
        out_shape=(jax.ShapeDtypeStruct((rows, LANES), F32),
                   jax.ShapeDtypeStruct((GATE_ROWS, rows), F32)),
        grid=(rows // chunk,),
        in_specs=[pl.BlockSpec((chunk, LANES), lambda i: (i, 0)),
                  pl.BlockSpec((GATE_ROWS, chunk), lambda i: (0, i)),
                  pl.BlockSpec((1, LANES), lambda i: (0, 0)),
                  pl.BlockSpec((GATE_ROWS, 1), lambda i: (0, 0))],
        out_specs=(pl.BlockSpec((chunk, LANES), lambda i: (i, 0)),
                   pl.BlockSpec((GATE_ROWS, chunk), lambda i: (0, i))),
        compiler_params=_params(1),
        name="gateprep",
    )(g, gt, bias_row, bias_col)
    gcol = gc[:, :4 * nh].reshape(bsz, t, 4, nh).transpose(0, 3, 1, 2)
    gcol = jnp.pad(gcol, ((0, 0), (0, 0), (0, 0), (0, 4)))
    grow = gr[:4 * nh].reshape(4, nh, bsz, t).transpose(2, 1, 0, 3)
    grow = jnp.pad(grow, ((0, 0), (0, 0), (0, 4), (0, 0)))
    return gcol, grow


def _conv_kernel(x_ref, w_ref, b_ref, o_ref, *, rows, width, taps):
    cb = x_ref.shape[-1]
    j = pl.program_id(1)
    scale = jnp.where(j >= pl.num_programs(1) // 2, ML_DH ** -0.5, 1.0).astype(F32)
    w = w_ref[...]
    bias = b_ref[...]
    col = lax.broadcasted_iota(jnp.int32, (width, cb), 0)
    first = col == 0
    last = col == width - 1

    def one_row(r):
        left = jnp.zeros((width, cb), F32)
        mid = jnp.zeros((width, cb), F32)
        right = jnp.zeros((width, cb), F32)
        for ki in taps:
            rr = r + (ki - 1)
            valid = jnp.where((rr >= 0) & (rr < rows), 1.0, 0.0).astype(F32)
            start = pl.multiple_of(jnp.clip(rr, 0, rows - 1) * width, width)
            xv = x_ref[pl.ds(start, width), :].astype(F32) * valid
            left = left + xv * w[3 * ki + 0:3 * ki + 1, :]
            mid = mid + xv * w[3 * ki + 1:3 * ki + 2, :]
            right = right + xv * w[3 * ki + 2:3 * ki + 3, :]
        out = (mid + jnp.where(first, 0.0, pltpu.roll(left, 1, axis=0))
               + jnp.where(last, 0.0, pltpu.roll(right, width - 1, axis=0)) + bias)
        dst = pl.multiple_of(r * width, width)
        o_ref[pl.ds(dst, width), :] = (_silu(out) * scale).astype(o_ref.dtype)

    def body(r, carry):
        one_row(r)
        return carry

    lax.fori_loop(0, rows, body, 0)


def _short_conv(u, col0, n_ch, conv_w, conv_b, grid):
    bsz, t, _ = u.shape
    cb = COL_BLOCK
    rows, width, taps = (t // GRID_W, GRID_W, (0, 1, 2)) if grid else (1, t, (1,))
    return pl.pallas_call(
        functools.partial(_conv_kernel, rows=rows, width=width, taps=taps),
        out_shape=jax.ShapeDtypeStruct((bsz, t, n_ch), BF16),
        grid=(bsz, n_ch // cb),
        in_specs=[pl.BlockSpec((None, t, cb), lambda b, j: (b, 0, col0 // cb + j)),
                  pl.BlockSpec((CONV_K * CONV_K, cb), lambda b, j: (0, j)),
                  pl.BlockSpec((1, cb), lambda b, j: (0, j))],
        out_specs=pl.BlockSpec((None, t, cb), lambda b, j: (b, 0, j)),
        compiler_params=_params(2),
        name="conv_grid" if grid else "conv_seq",
    )(u, conv_w.reshape(CONV_K * CONV_K, n_ch).astype(F32), conv_b.reshape(1, n_ch).astype(F32))


def _hg_prepare(qraw, z, lb, fwd):
    lower, upper = _tri_masks(z.shape[0])
    tri = jnp.where(lower if fwd else upper, 1.0, 0.0).astype(BF16)
    f = lb + (1.0 - lb) * jax.nn.sigmoid(z)
    b2 = _tri_left(tri, jnp.log(f) * LOG2E)
    c2 = b2 - jnp.log(1.0 - f) * LOG2E
    return _silu(qraw), b2, c2


def _block_diag(a):
    r, w = a.shape
    zero = jnp.zeros((r, w // 2), a.dtype)
    return jnp.concatenate([jnp.concatenate([a[:, :w // 2], zero], axis=1),
                            jnp.concatenate([zero, a[:, w // 2:]], axis=1)], axis=0)


def _hg_scores(q_ref, b_ref, c_ref, fwd):
    c, w = q_ref.shape
    sub = HG_SUB
    nb = c // sub
    assert 2 * c == LANES and c & (c - 1) == 0
    row = lax.broadcasted_iota(jnp.int32, (sub, LANES), 0)
    col = lax.broadcasted_iota(jnp.int32, (sub, LANES), 1) & (c - 1)
    src = col & (sub - 1)
    code = jnp.where((src <= row) if fwd else (src >= row), col, -1)
    kr = lax.broadcasted_iota(jnp.int32, (w, LANES), 0)
    kl = lax.broadcasted_iota(jnp.int32, (w, LANES), 1)
    sel = jnp.where((kr >= w // 2) == (kl >= c), 1.0, 0.0).astype(BF16)

    units = []
    for j in range(nb):
        r0 = j * sub
        qb = q_ref[r0:r0 + sub, :]
        bb = b_ref[r0:r0 + sub, :]
        for s in range(sub):
            units.append(qb * jnp.exp2(bb - c_ref[pl.ds(r0 + s, 1), :]))
    red = jnp.dot(jnp.concatenate(units, axis=0).astype(BF16), sel, preferred_element_type=F32)

    blocks = []
    for j in range(nb):
        r0 = j * sub
        blk = jnp.zeros((sub, LANES), F32)
        for s in range(sub):
            u0 = (j * sub + s) * sub
            blk = jnp.where(code == r0 + s, red[u0:u0 + sub], blk)
        lo, hi = (0, r0) if fwd else (r0 + sub, c)
        if hi > lo:
            beta = b_ref[pl.ds(r0 - 1 if fwd else r0 + sub, 1), :]
            pieces = [jnp.zeros((lo, w), F32)] if lo else []
            pieces.append(jnp.exp2(beta - c_ref[lo:hi, :]))
            if hi < c:
                pieces.append(jnp.zeros((c - hi, w), F32))
            khd = _block_diag(jnp.concatenate(pieces, axis=0).astype(BF16))
            qh = (q_ref[r0:r0 + sub, :] * jnp.exp2(b_ref[r0:r0 + sub, :] - beta)).astype(BF16)
            blk = blk + lax.dot_general(qh, khd, _NT, preferred_element_type=F32)
        blocks.append(blk)
    return jnp.concatenate(blocks, axis=0).astype(BF16)


def _hg_state_update(st_refs, v, k_dec, dec, qe=None):
    hw = v.shape[1] // 2
    outs = []
    for h, st_ref in enumerate(st_refs):
        ls = slice(h * hw, (h + 1) * hw)
        st = st_ref[...]
        st_ref[...] = st * dec[:, ls] + lax.dot_general(v[:, ls], k_dec[:, ls], _TN, preferred_element_type=F32)
        if qe is not None:
            outs.append(lax.dot_general(qe[:, ls], st.astype(BF16), _NT, preferred_element_type=F32))
    return outs


def _hgrn2_kernel(cq, cff, cfb, ci, aq, aff, afb, ai, az, lb_ref, nw_ref, y_ref,
                  acc_f, acc_b, st_ref, q_sc, b_sc, c_sc, a_sc, qe_sc, kd_sc, dec_sc, *, slots):
    c = HG_CHUNK
    ncc = cq.shape[0] // c
    nc = aq.shape[0] // c
    hw = HG_DK
    assert nc % 2 == 0 and nc >= 4
    st_ref[...] = jnp.zeros_like(st_ref)
    accs = (acc_f, acc_b)
    lbs = []
    for d in range(2):
        hg = lb_ref[d * slots:(d + 1) * slots, :]
        e = jnp.exp(hg - jnp.max(hg, axis=0, keepdims=True))
        lbs.append(e[0:1, :] / jnp.sum(e, axis=0, keepdims=True))

    def chunk_rows(d, k, n):
        return pl.ds(pl.multiple_of((k if d == 0 else n - 1 - k) * c, c), c)

    def states(d):
        return st_ref.at[2 * d], st_ref.at[2 * d + 1]

    def ctx_body(k, carry):
        for d in range(2):
            rows = chunk_rows(d, k, ncc)
            _, b2, c2 = _hg_prepare(cq[rows, :].astype(F32), (cff, cfb)[d][rows, :].astype(F32), lbs[d], d == 0)
            end = c - 1 if d == 0 else 0
            b_end = b2[end:end + 1, :]
            _hg_state_update(states(d), ci[rows, :], jnp.exp2(b_end - c2).astype(BF16), jnp.exp2(b_end))
        return carry

    def gates(d, k, slot):
        rows = chunk_rows(d, k, nc)
        q, b2, c2 = _hg_prepare(aq[rows, :].astype(F32), (aff, afb)[d][rows, :].astype(F32), lbs[d], d == 0)
        q_sc[slot, d] = q
        b_sc[slot, d] = b2
        c_sc[slot, d] = c2

    def scores(d, slot):
        qr, br, cr = q_sc.at[slot, d], b_sc.at[slot, d], c_sc.at[slot, d]
        end = c - 1 if d == 0 else 0
        b_end = br[end:end + 1, :]
        qe_sc[slot, d] = (qr[...] * jnp.exp2(br[...])).astype(BF16)
        kd_sc[slot, d] = jnp.exp2(b_end - cr[...]).astype(BF16)
        dec_sc[slot, d] = jnp.exp2(b_end)
        a_sc[slot, d] = _hg_scores(qr, br, cr, d == 0)

    def outputs(d, k, slot):
        rows = chunk_rows(d, k, nc)
        v = ai[rows, :]
        outs = _hg_state_update(states(d), v, kd_sc[slot, d], dec_sc[slot, d], qe_sc[slot, d])
        accs[d][rows, :] = (jnp.concatenate(outs, axis=1)
                            + jnp.dot(a_sc[slot, d], _block_diag(v), preferred_element_type=F32))

    def step(k, slot, do_out=True, do_gates=True, do_scores=True):
        for d in range(2):
            if do_out:
                outputs(d, k - 2, slot)
            if do_gates:
                gates(d, k, slot)
        for d in range(2):
            if do_scores:
                scores(d, 1 - slot)

    def pair_body(kk, carry):
        step(2 * kk, 0)
        step(2 * kk + 1, 1)
        return carry

    def final_body(i, carry):
        rows = pl.ds(pl.multiple_of(i * c, c), c)
        tot = acc_f[rows, :] + acc_b[rows, :]
        gate = _silu(az[rows, :].astype(F32)) * nw_ref[...]
        for h in range(2):
            ls = slice(h * hw, (h + 1) * hw)
            th = tot[:, ls]
            ms = jnp.mean(th * th, axis=-1, keepdims=True)
            y_ref[rows, ls] = (th * lax.rsqrt(ms + NORM_EPS) * gate[:, ls]).astype(y_ref.dtype)
        return carry

    lax.fori_loop(0, ncc, ctx_body, 0)
    step(0, 0, do_out=False, do_scores=False)
    step(1, 1, do_out=False)
    lax.fori_loop(1, nc // 2, pair_body, 0)
    step(nc, 0, do_gates=False)
    step(nc + 1, 1, do_gates=False, do_scores=False)
    lax.fori_loop(0, nc, final_body, 0)


def _hgrn2(u_x, u_c, w_a, hg_lb, hg_norm):
    bsz, t, _ = u_x.shape
    tc = u_c.shape[1]
    cb = 2 * HG_DK
    c = HG_CHUNK
    assert cb == COL_BLOCK
    seg = w_a // cb
    slots = hg_lb.shape[1]
    lb2 = hg_lb.reshape(2 * slots, w_a).astype(F32)

    def col(k):
        return lambda b, p: (b, 0, k * seg + p)

    ctx_specs = [pl.BlockSpec((None, tc, cb), col(k)) for k in (0, 1, 2, 3)]
    lat_specs = [pl.BlockSpec((None, t, cb), col(k)) for k in (0, 1, 2, 3, 4)]
    return pl.pallas_call(
        functools.partial(_hgrn2_kernel, slots=slots),
        out_shape=jax.ShapeDtypeStruct((bsz, t, w_a), BF16),
        grid=(bsz, seg),
        in_specs=ctx_specs + lat_specs + [
            pl.BlockSpec((2 * slots, cb), lambda b, p: (0, p)),
            pl.BlockSpec((1, cb), lambda b, p: (0, p))],
        out_specs=pl.BlockSpec((None, t, cb), lambda b, p: (b, 0, p)),
        scratch_shapes=[pltpu.VMEM((t, cb), F32),
                        pltpu.VMEM((t, cb), F32),
                        pltpu.VMEM((4, HG_DK, HG_DK), F32),
                        pltpu.VMEM((2, 2, c, cb), F32),
                        pltpu.VMEM((2, 2, c, cb), F32),
                        pltpu.VMEM((2, 2, c, cb), F32),
                        pltpu.VMEM((2, 2, c, 2 * c), BF16),
                        pltpu.VMEM((2, 2, c, cb), BF16),
                        pltpu.VMEM((2, 2, c, cb), BF16),
                        pltpu.VMEM((2, 2, 1, cb), F32)],
        compiler_params=_params(2),
        name="hgrn2",
    )(u_c, u_c, u_c, u_c, u_x, u_x, u_x, u_x, u_x, lb2, hg_norm.reshape(1, w_a).astype(F32))


def _ml_step(q, k, v, gcol, grow, ct_ref, n_ref, m_prev, fwd, need_out):
    c = q.shape[0]
    lower, upper = _tri_masks(c)
    mask = lower if fwd else upper
    ia, ib = (0, 2) if fwd else (1, 3)
    a_col = gcol[:, ia:ia + 1]
    b_col = gcol[:, ib:ib + 1]
    a_row = grow[ia:ia + 1, :]
    end = c - 1 if fwd else 0
    b_end = b_col[end:end + 1, :]
    neg = -jnp.inf
    a_max = jnp.max(jnp.where(mask, a_row, neg), axis=1, keepdims=True)
    m_t = b_col + jnp.maximum(m_prev, a_max)
    m_new = m_t[end:end + 1, :]
    w_s = jnp.exp(b_end + a_col - m_new)
    decay = jnp.exp(b_end + m_prev - m_new)
    ct = ct_ref[...]
    n = n_ref[...]
    kw = k.astype(F32) * w_s
    ct_ref[...] = decay * ct + lax.dot_general(kw.astype(BF16), v, _TN, preferred_element_type=F32)
    n_ref[...] = decay * n + jnp.sum(kw, axis=0, keepdims=True)
    if not need_out:
        return m_new, None

    w_inter = jnp.exp(b_col + m_prev - m_t)
    dm = jnp.exp(jnp.where(mask, (b_col - m_t) + a_row, neg))
    w_qk = dm * lax.dot_general(q, k, _NT, preferred_element_type=F32)
    num = (w_inter * jnp.dot(q, ct.astype(BF16), preferred_element_type=F32)
           + jnp.dot(w_qk.astype(BF16), v, preferred_element_type=F32))
    den = (w_inter * jnp.sum(q.astype(F32) * n, axis=1, keepdims=True)
           + jnp.sum(w_qk, axis=1, keepdims=True))
    return m_new, num / jnp.maximum(jnp.abs(den), jnp.exp(-m_t))


def _mlstm_kernel(cq, ck, cv, cgc, cgr, xq, xk, xv, xo, xz, xgc, xgr, nw_ref, y_ref,
                  acc, ct_ref, n_ref):
    c = ML_CHUNK
    ncc = cq.shape[0] // c
    nc = xq.shape[0] // c
    ct_ref[...] = jnp.zeros_like(ct_ref)
    n_ref[...] = jnp.zeros_like(n_ref)
    nw = nw_ref[...]

    def run(refs, ics, ms, mode):
        qr, kr, vr, gcr, grr = refs
        m_out = []
        for d in range(2):
            rows = pl.ds(pl.multiple_of(ics[d] * c, c), c)
            m_new, h = _ml_step(qr[rows, :], kr[rows, :], vr[rows, :], gcr[rows, :], grr[:, rows],
                                ct_ref.at[d], n_ref.at[d], ms[d], fwd=(d == 0),
                                need_out=(mode != "state"))
            m_out.append(m_new)
            if mode == "store":
                acc[rows, :] = h
            elif mode == "final":
                tot = acc[rows, :] + h
                mu = jnp.mean(tot, axis=-1, keepdims=True)
                tc_ = tot - mu
                var = jnp.mean(tc_ * tc_, axis=-1, keepdims=True)
                gate = jax.nn.sigmoid(xo[rows, :].astype(F32)) * _silu(xz[rows, :].astype(F32))
                y_ref[rows, :] = (tc_ * lax.rsqrt(var + NORM_EPS) * nw * gate).astype(y_ref.dtype)
        return tuple(m_out)

    def loop(lo, hi, refs, n, mode, ms):
        return lax.fori_loop(lo, hi, lambda i, m: run(refs, (i, n - 1 - i), m, mode), ms)

    ms = (jnp.zeros((1, 1), F32), jnp.zeros((1, 1), F32))
    ms = loop(0, ncc, (cq, ck, cv, cgc, cgr), ncc, "state", ms)
    ms = loop(0, nc // 2, (xq, xk, xv, xgc, xgr), nc, "store", ms)
    loop(nc // 2, nc, (xq, xk, xv, xgc, xgr), nc, "final", ms)


def _mlstm(qk_x, qk_c, u_x, u_c, v_col, o_col, z_col, gates_x, gates_c, ml_norm, w_b):
    bsz, t, _ = u_x.shape
    tc = u_c.shape[1]
    nh = w_b // ML_DH
    cb = ML_DH
    assert cb == COL_BLOCK and (t // ML_CHUNK) % 2 == 0 and tc % ML_CHUNK == 0

    def col(k):
        return lambda b, h: (b, 0, k + h)

    def seq_specs(n, with_gates):
        specs = [pl.BlockSpec((None, n, cb), col(0)),
                 pl.BlockSpec((None, n, cb), col(nh)),
                 pl.BlockSpec((None, n, cb), col(v_col // cb))]
        if with_gates:
            specs += [pl.BlockSpec((None, n, cb), col(o_col // cb)),
                      pl.BlockSpec((None, n, cb), col(z_col // cb))]
        specs += [pl.BlockSpec((None, None, n, 8), lambda b, h: (b, h, 0, 0)),
                  pl.BlockSpec((None, None, 8, n), lambda b, h: (b, h, 0, 0))]
        return specs

    return pl.pallas_call(
        _mlstm_kernel,
        out_shape=jax.ShapeDtypeStruct((bsz, t, w_b), BF16),
        grid=(bsz, nh),
        in_specs=seq_specs(tc, False) + seq_specs(t, True) + [pl.BlockSpec((1, cb), lambda b, h: (0, h))],
        out_specs=pl.BlockSpec((None, t, cb), lambda b, h: (b, 0, h)),
        scratch_shapes=[pltpu.VMEM((t, cb), F32),
                        pltpu.VMEM((2, ML_DH, ML_DH), F32),
                        pltpu.VMEM((2, 1, ML_DH), F32)],
        compiler_params=_params(2),
        name="mlstm",
    )(qk_c, qk_c, u_c, gates_c[0], gates_c[1],
      qk_x, qk_x, u_x, u_x, u_x, gates_x[0], gates_x[1],
      ml_norm.reshape(1, w_b).astype(F32))


def _outproj_kernel(ya_ref, yb_ref, x_ref, gate_ref, wa_ref, wb_ref, g_ref, b_ref, o_ref):
    y = (jnp.dot(ya_ref[...], wa_ref[...], preferred_element_type=F32)
         + jnp.dot(yb_ref[...], wb_ref[...], preferred_element_type=F32))
    r = ALPHA * x_ref[...] + gate_ref[...] * y
    mu = jnp.mean(r, axis=-1, keepdims=True)
    rc = r - mu
    var = jnp.mean(rc * rc, axis=-1, keepdims=True)
    o_ref[...] = rc * lax.rsqrt(var + LN_EPS) * g_ref[...] + b_ref[...]


def _outproj(y_a, y_b, x2, mod3, t, w_oa, w_ob, ln_g, ln_b):
    rows, d = x2.shape
    tm = 256
    wa, wb = y_a.shape[1], y_b.shape[1]
    return pl.pallas_call(
        _outproj_kernel,
        out_shape=jax.ShapeDtypeStruct((rows, d), F32),
        grid=(rows // tm,),
        in_specs=[pl.BlockSpec((tm, wa), lambda i: (i, 0)),
                  pl.BlockSpec((tm, wb), lambda i: (i, 0)),
                  pl.BlockSpec((tm, d), lambda i: (i, 0)),
                  pl.BlockSpec((None, 1, d), lambda i: ((i * tm) // t, 0, 2)),
                  pl.BlockSpec((wa, d), lambda i: (0, 0)),
                  pl.BlockSpec((wb, d), lambda i: (0, 0)),
                  pl.BlockSpec((1, d), lambda i: (0, 0)),
                  pl.BlockSpec((1, d), lambda i: (0, 0))],
        out_specs=pl.BlockSpec((tm, d), lambda i: (i, 0)),
        compiler_params=_params(1),
        name="outproj",
    )(y_a, y_b, x2, mod3, w_oa, w_ob, ln_g.reshape(1, d).astype(F32), ln_b.reshape(1, d).astype(F32))


def kernel(x, c, ctx, c_ctx, w_mod, b_mod, w_in, conv_w, conv_b, hg_lb, ml_gate_b, hg_norm_w,
           ml_norm_w, w_out, ln_g, ln_b):
    bsz, t, d = x.shape
    tc = ctx.shape[1]
    assert w_in.shape[0] == DEPTH and hg_lb.shape[1] == DEPTH + 1
    d_inner = w_out.shape[1]
    w_a = hg_lb.shape[2]
    w_b = d_inner - w_a
    nh = ml_gate_b.shape[2]
    nu = 5 * w_a + 5 * w_b
    assert nh * ML_DH == w_b and w_in.shape[2] == nu + 4 * nh and t % GRID_W == 0

    mod_rows = 8 * ((bsz + 1 + 7) // 8)
    cc = jnp.zeros((mod_rows, d), F32).at[:bsz].set(c).at[bsz].set(c_ctx)
    mod3 = _modulation(cc, w_mod[0], b_mod[0]).reshape(mod_rows, 1, 3 * d)

    w_main = w_in[0][:, :nu].astype(BF16)
    w_g = jnp.pad(w_in[0][:, nu:], ((0, 0), (0, LANES - 4 * nh))).astype(BF16)
    tm = min(1024, t)
    u_x, g_x, gt_x = _inproj(x.reshape(bsz * t, d), mod3, lambda i: (i * tm) // t, w_main, w_g, w_g.T, tm)
    u_c, g_c, gt_c = _inproj(ctx.reshape(bsz * tc, d), mod3, lambda i: bsz, w_main, w_g, w_g.T,
                             min(1024, bsz * tc))
    u_x = u_x.reshape(bsz, t, nu)
    u_c = u_c.reshape(bsz, tc, nu)

    gates_x = _gateprep(g_x, gt_x[:GATE_ROWS], ml_gate_b[0], nh, bsz, t, ML_CHUNK)
    gates_c = _gateprep(g_c, gt_c[:GATE_ROWS], ml_gate_b[0], nh, bsz, tc, ML_CHUNK)

    qk_col = 5 * w_a
    qk_x = _short_conv(u_x, qk_col, 2 * w_b, conv_w[0], conv_b[0], grid=True)
    qk_c = _short_conv(u_c, qk_col, 2 * w_b, conv_w[0], conv_b[0], grid=False)

    y_a = _hgrn2(u_x, u_c, w_a, hg_lb, hg_norm_w[0])
    y_b = _mlstm(qk_x, qk_c, u_x, u_c, qk_col + 2 * w_b, qk_col + 3 * w_b, qk_col + 4 * w_b,
                 gates_x, gates_c, ml_norm_w[0], w_b)

    w_o = w_out[0].astype(BF16)
    out = _outproj(y_a.reshape(bsz * t, w_a), y_b.reshape(bsz * t, w_b), x.reshape(bsz * t, d), mod3, t,
                   w_o[:w_a], w_o[w_a:], ln_g[0], ln_b[0])
    return out.reshape(bsz, t, d)
```

```python
import functools

import jax
import jax.numpy as jnp
from jax import lax
from jax.experimental import pallas as pl
from jax.experimental.pallas import tpu as pltpu

F32 = jnp.float32
BF16 = jnp.bfloat16

LN_EPS = 1e-5
NORM_EPS = 1e-6
DEPTH = 1
ALPHA = (2 * DEPTH) ** 0.25

HG_DK = 128
ML_DH = 256
GRID_W = 64
CONV_K = 3

LANES = 128
HG_CHUNK = 64
HG_SUB = 8
LOG2E = 1.4426950408889634
ML_CHUNK = 256
ML_PAD = 16
GATE_ROWS = 24
COL_BLOCK = 256
VMEM_LIMIT = 56 * 1024 * 1024

_NT = (((1,), (1,)), ((), ()))
_TN = (((0,), (0,)), ((), ()))


def _params(n_grid):
    return pltpu.CompilerParams(dimension_semantics=("arbitrary",) * n_grid,
                                vmem_limit_bytes=VMEM_LIMIT)


def _silu(a):
    return a * jax.nn.sigmoid(a)


def _log_sigmoid(a):
    return jnp.minimum(a, 0.0) - jnp.log1p(jnp.exp(-jnp.abs(a)))


def _split3(a):
    hi = a.astype(BF16)
    r1 = a - hi.astype(F32)
    mid = r1.astype(BF16)
    lo = (r1 - mid.astype(F32)).astype(BF16)
    return hi, mid, lo


def _tri_left(tri, a):
    return sum(jnp.dot(tri, t, preferred_element_type=F32) for t in _split3(a))


def _tri_right(a, tri):
    return sum(jnp.dot(t, tri, preferred_element_type=F32) for t in _split3(a))


def _tri_masks(n):
    r = lax.broadcasted_iota(jnp.int32, (n, n), 0)
    c = lax.broadcasted_iota(jnp.int32, (n, n), 1)
    return r >= c, r <= c


def _mod_kernel(c_ref, w_ref, b_ref, o_ref):
    s = _silu(c_ref[...])
    o_ref[...] = jnp.dot(s, w_ref[...], precision=lax.Precision.HIGHEST,
                         preferred_element_type=F32) + b_ref[...]


def _modulation(cc, w_mod, b_mod):
    rows, d = cc.shape
    n = w_mod.shape[1]
    tn = 512
    return pl.pallas_call(
        _mod_kernel,
        out_shape=jax.ShapeDtypeStruct((rows, n), F32),
        grid=(n // tn,),
        in_specs=[pl.BlockSpec((rows, d), lambda j: (0, 0)),
                  pl.BlockSpec((d, tn), lambda j: (0, j)),
                  pl.BlockSpec((1, tn), lambda j: (0, j))],
        out_specs=pl.BlockSpec((rows, tn), lambda j: (0, j)),
        compiler_params=_params(1),
        name="mod",
    )(cc, w_mod, b_mod.reshape(1, n))


def _wcast_kernel(wt_ref, o_ref, *, valid):
    w = wt_ref[...]
    if valid < w.shape[0]:
        w = jnp.where(lax.broadcasted_iota(jnp.int32, w.shape, 0) < valid, w, 0.0)
    o_ref[...] = w.T.astype(o_ref.dtype)


def _wcast(wt, col0, n_out, tn):
    n, d = wt.shape
    assert col0 % tn == 0 and n_out % tn == 0
    return pl.pallas_call(
        functools.partial(_wcast_kernel, valid=min(tn, n - col0)),
        out_shape=jax.ShapeDtypeStruct((d, n_out), BF16),
        grid=(n_out // tn,),
        in_specs=[pl.BlockSpec((tn, d), lambda j: (col0 // tn + j, 0))],
        out_specs=pl.BlockSpec((d, tn), lambda j: (0, j)),
        compiler_params=_params(1),
        name="wcast",
    )(wt)


def _inproj_kernel(x_ref, sh_ref, sc_ref, w_ref, wg_ref, u_ref, g_ref, h_ref, *, sub):
    @pl.when(pl.program_id(1) == 0)
    def _():
        shift = sh_ref[...]
        scale1 = 1.0 + sc_ref[...]

        def body(r, carry):
            rows = pl.ds(pl.multiple_of(r * sub, sub), sub)
            xv = x_ref[rows, :]
            mu = jnp.mean(xv, axis=-1, keepdims=True)
            xc = xv - mu
            var = jnp.mean(xc * xc, axis=-1, keepdims=True)
            h_ref[rows, :] = (xc * lax.rsqrt(var + LN_EPS) * scale1 + shift).astype(BF16)
            return carry

        lax.fori_loop(0, x_ref.shape[0] // sub, body, 0)
        g_ref[...] = jnp.dot(h_ref[...], wg_ref[...], preferred_element_type=F32)

    u_ref[...] = jnp.dot(h_ref[...], w_ref[...], preferred_element_type=F32).astype(BF16)


def _inproj(x2, mod3, mod_row, w_main, w_g, tm):
    rows, d = x2.shape
    nu = w_main.shape[1]
    tn = 512
    sub = min(256, tm)
    return pl.pallas_call(
        functools.partial(_inproj_kernel, sub=sub),
        out_shape=(jax.ShapeDtypeStruct((rows, nu), BF16),
                   jax.ShapeDtypeStruct((rows, LANES), F32)),
        grid=(rows // tm, nu // tn),
        in_specs=[pl.BlockSpec((tm, d), lambda i, j: (i, 0)),
                  pl.BlockSpec((None, 1, d), lambda i, j: (mod_row(i), 0, 0)),
                  pl.BlockSpec((None, 1, d), lambda i, j: (mod_row(i), 0, 1)),
                  pl.BlockSpec((d, tn), lambda i, j: (0, j)),
                  pl.BlockSpec((d, LANES), lambda i, j: (0, 0))],
        out_specs=(pl.BlockSpec((tm, tn), lambda i, j: (i, j)),
                   pl.BlockSpec((tm, LANES), lambda i, j: (i, 0))),
        scratch_shapes=[pltpu.VMEM((tm, d), BF16)],
        compiler_params=_params(2),
        name="inproj",
    )(x2, mod3, mod3, w_main, w_g)


def _gateprep_kernel(g_ref, bias_ref, gc_ref, gr_ref, *, nh):
    c = g_ref.shape[0]
    lower, upper = _tri_masks(c)
    tril = jnp.where(lower, 1.0, 0.0).astype(BF16)
    triu = jnp.where(upper, 1.0, 0.0).astype(BF16)

    pre = g_ref[...] + bias_ref[...]
    lf = _log_sigmoid(pre)
    lane = lax.broadcasted_iota(jnp.int32, pre.shape, 1)
    fwd_f = (lane >= 2 * nh) & (lane < 3 * nh)
    bwd_f = (lane >= 3 * nh) & (lane < 4 * nh)
    b = jnp.where(fwd_f, _tri_left(tril, lf), jnp.where(bwd_f, _tri_left(triu, lf), 0.0))
    gc_ref[...] = pre - pltpu.roll(b, LANES - 2 * nh, axis=1)

    pre_t = pre.T[:GATE_ROWS]
    lf_t = _log_sigmoid(pre_t)
    row = lax.broadcasted_iota(jnp.int32, pre_t.shape, 0)
    pos = lax.broadcasted_iota(jnp.int32, pre_t.shape, 1)
    fwd_r = (row >= 2 * nh) & (row < 3 * nh)
    bwd_r = (row >= 3 * nh) & (row < 4 * nh)
    b_t = jnp.where(fwd_r, _tri_right(lf_t, triu), jnp.where(bwd_r, _tri_right(lf_t, tril), 0.0))
    a_t = pre_t - pltpu.roll(b_t, GATE_ROWS - 2 * nh, axis=0)
    pmax, smax = a_t, a_t
    shift = 1
    while shift < c:
        pmax = jnp.maximum(pmax, jnp.where(pos >= shift, pltpu.roll(pmax, shift, axis=1), -jnp.inf))
        smax = jnp.maximum(smax, jnp.where(pos < c - shift, pltpu.roll(smax, c - shift, axis=1), -jnp.inf))
        shift *= 2
    amax_t = pltpu.roll(jnp.where(row < nh, pmax, smax), 4 * nh, axis=0)
    gr_ref[...] = jnp.where(row < 2 * nh, a_t, jnp.where(row < 4 * nh, b_t, amax_t))


def _gateprep(g, bias, nh, chunk):
    rows = g.shape[0]
    assert 6 * nh <= GATE_ROWS
    bias_row = jnp.zeros((1, LANES), F32).at[0, :4 * nh].set(bias.reshape(-1))
    return pl.pallas_call(
        functools.partial(_gateprep_kernel, nh=nh),
        out_shape=(jax.ShapeDtypeStruct((rows, LANES), F32),
                   jax.ShapeDtypeStruct((GATE_ROWS, rows), F32)),
        grid=(rows // chunk,),
        in_specs=[pl.BlockSpec((chunk, LANES), lambda i: (i, 0)),
                  pl.BlockSpec((1, LANES), lambda i: (0, 0))],
        out_specs=(pl.BlockSpec((chunk, LANES), lambda i: (i, 0)),
                   pl.BlockSpec((GATE_ROWS, chunk), lambda i: (0, i))),
        compiler_params=_params(1),
        name="gateprep",
    )(g, bias_row)


def _conv_kernel(x_ref, w_ref, b_ref, o_ref, *, rows, width, taps):
    cb = x_ref.shape[-1]
    j = pl.program_id(1)
    scale = jnp.where(j >= pl.num_programs(1) // 2, ML_DH ** -0.5, 1.0).astype(F32)
    w = w_ref[...]
    bias = b_ref[...]
    col = lax.broadcasted_iota(jnp.int32, (width, cb), 0)
    first = col == 0
    last = col == width - 1

    def one_row(r):
        left = jnp.zeros((width, cb), F32)
        mid = jnp.zeros((width, cb), F32)
        right = jnp.zeros((width, cb), F32)
        for ki in taps:
            rr = r + (ki - 1)
            valid = jnp.where((rr >= 0) & (rr < rows), 1.0, 0.0).astype(F32)
            start = pl.multiple_of(jnp.clip(rr, 0, rows - 1) * width, width)
            xv = x_ref[pl.ds(start, width), :].astype(F32) * valid
            left = left + xv * w[3 * ki + 0:3 * ki + 1, :]
            mid = mid + xv * w[3 * ki + 1:3 * ki + 2, :]
            right = right + xv * w[3 * ki + 2:3 * ki + 3, :]
        out = (mid + jnp.where(first, 0.0, pltpu.roll(left, 1, axis=0))
               + jnp.where(last, 0.0, pltpu.roll(right, width - 1, axis=0)) + bias)
        dst = pl.multiple_of(r * width, width)
        o_ref[pl.ds(dst, width), :] = (_silu(out) * scale).astype(o_ref.dtype)

    def body(r, carry):
        one_row(r)
        return carry

    lax.fori_loop(0, rows, body, 0)


def _short_conv(u, col0, n_ch, conv_w, conv_b, grid):
    bsz, t, _ = u.shape
    cb = COL_BLOCK
    rows, width, taps = (t // GRID_W, GRID_W, (0, 1, 2)) if grid else (1, t, (1,))
    return pl.pallas_call(
        functools.partial(_conv_kernel, rows=rows, width=width, taps=taps),
        out_shape=jax.ShapeDtypeStruct((bsz, t, n_ch), BF16),
        grid=(bsz, n_ch // cb),
        in_specs=[pl.BlockSpec((None, t, cb), lambda b, j: (b, 0, col0 // cb + j)),
                  pl.BlockSpec((CONV_K * CONV_K, cb), lambda b, j: (0, j)),
                  pl.BlockSpec((1, cb), lambda b, j: (0, j))],
        out_specs=pl.BlockSpec((None, t, cb), lambda b, j: (b, 0, j)),
        compiler_params=_params(2),
        name="conv_grid" if grid else "conv_seq",
    )(u, conv_w.reshape(CONV_K * CONV_K, n_ch).astype(F32), conv_b.reshape(1, n_ch).astype(F32))


def _hg_prepare(qraw, z, lb, fwd):
    lower, upper = _tri_masks(z.shape[0])
    tri = jnp.where(lower if fwd else upper, 1.0, 0.0).astype(BF16)
    f = lb + (1.0 - lb) * jax.nn.sigmoid(z)
    b2 = _tri_left(tri, jnp.log(f) * LOG2E)
    c2 = b2 - jnp.log(1.0 - f) * LOG2E
    return _silu(qraw), b2, c2


def _block_diag(a):
    r, w = a.shape
    zero = jnp.zeros((r, w // 2), a.dtype)
    return jnp.concatenate([jnp.concatenate([a[:, :w // 2], zero], axis=1),
                            jnp.concatenate([zero, a[:, w // 2:]], axis=1)], axis=0)


def _hg_scores(q_ref, b_ref, c_ref, fwd):
    c, w = q_ref.shape
    sub = HG_SUB
    nb = c // sub
    assert 2 * c == LANES and c & (c - 1) == 0
    row = lax.broadcasted_iota(jnp.int32, (sub, LANES), 0)
    col = lax.broadcasted_iota(jnp.int32, (sub, LANES), 1) & (c - 1)
    src = col & (sub - 1)
    code = jnp.where((src <= row) if fwd else (src >= row), col, -1)
    kr = lax.broadcasted_iota(jnp.int32, (w, LANES), 0)
    kl = lax.broadcasted_iota(jnp.int32, (w, LANES), 1)
    sel = jnp.where((kr >= w // 2) == (kl >= c), 1.0, 0.0).astype(BF16)

    units = []
    for j in range(nb):
        r0 = j * sub
        qb = q_ref[r0:r0 + sub, :]
        bb = b_ref[r0:r0 + sub, :]
        for s in range(sub):
            units.append(qb * jnp.exp2(bb - c_ref[pl.ds(r0 + s, 1), :]))
    red = jnp.dot(jnp.concatenate(units, axis=0).astype(BF16), sel, preferred_element_type=F32)

    blocks = []
    for j in range(nb):
        r0 = j * sub
        blk = jnp.zeros((sub, LANES), F32)
        for s in range(sub):
            u0 = (j * sub + s) * sub
            blk = jnp.where(code == r0 + s, red[u0:u0 + sub], blk)
        lo, hi = (0, r0) if fwd else (r0 + sub, c)
        if hi > lo:
            beta = b_ref[pl.ds(r0 - 1 if fwd else r0 + sub, 1), :]
            pieces = [jnp.zeros((lo, w), F32)] if lo else []
            pieces.append(jnp.exp2(beta - c_ref[lo:hi, :]))
            if hi < c:
                pieces.append(jnp.zeros((c - hi, w), F32))
            khd = _block_diag(jnp.concatenate(pieces, axis=0).astype(BF16))
            qh = (q_ref[r0:r0 + sub, :] * jnp.exp2(b_ref[r0:r0 + sub, :] - beta)).astype(BF16)
            blk = blk + lax.dot_general(qh, khd, _NT, preferred_element_type=F32)
        blocks.append(blk)
    return jnp.concatenate(blocks, axis=0).astype(BF16)


def _hg_state_update(st_refs, v, k_dec, dec, qe=None):
    hw = v.shape[1] // 2
    outs = []
    for h, st_ref in enumerate(st_refs):
        ls = slice(h * hw, (h + 1) * hw)
        st = st_ref[...]
        st_ref[...] = st * dec[:, ls] + lax.dot_general(v[:, ls], k_dec[:, ls], _TN, preferred_element_type=F32)
        if qe is not None:
            outs.append(lax.dot_general(qe[:, ls], st.astype(BF16), _NT, preferred_element_type=F32))
    return outs


def _hgrn2_kernel(cq, cff, cfb, ci, aq, aff, afb, ai, az, lb_ref, nw_ref, y_ref,
                  acc_f, acc_b, st_ref, q_sc, b_sc, c_sc, a_sc, qe_sc, kd_sc, dec_sc, *, slots):
    c = HG_CHUNK
    ncc = cq.shape[0] // c
    nc = aq.shape[0] // c
    hw = HG_DK
    assert nc % 2 == 0 and nc >= 4
    st_ref[...] = jnp.zeros_like(st_ref)
    accs = (acc_f, acc_b)
    lbs = []
    for d in range(2):
        hg = lb_ref[d * slots:(d + 1) * slots, :]
        e = jnp.exp(hg - jnp.max(hg, axis=0, keepdims=True))
        lbs.append(e[0:1, :] / jnp.sum(e, axis=0, keepdims=True))

    def chunk_rows(d, k, n):
        return pl.ds(pl.multiple_of((k if d == 0 else n - 1 - k) * c, c), c)

    def states(d):
        return st_ref.at[2 * d], st_ref.at[2 * d + 1]

    def ctx_body(k, carry):
        for d in range(2):
            rows = chunk_rows(d, k, ncc)
            _, b2, c2 = _hg_prepare(cq[rows, :].astype(F32), (cff, cfb)[d][rows, :].astype(F32), lbs[d], d == 0)
            end = c - 1 if d == 0 else 0
            b_end = b2[end:end + 1, :]
            _hg_state_update(states(d), ci[rows, :], jnp.exp2(b_end - c2).astype(BF16), jnp.exp2(b_end))
        return carry

    def gates(d, k, slot):
        rows = chunk_rows(d, k, nc)
        q, b2, c2 = _hg_prepare(aq[rows, :].astype(F32), (aff, afb)[d][rows, :].astype(F32), lbs[d], d == 0)
        q_sc[slot, d] = q
        b_sc[slot, d] = b2
        c_sc[slot, d] = c2

    def scores(d, slot):
        qr, br, cr = q_sc.at[slot, d], b_sc.at[slot, d], c_sc.at[slot, d]
        end = c - 1 if d == 0 else 0
        b_end = br[end:end + 1, :]
        qe_sc[slot, d] = (qr[...] * jnp.exp2(br[...])).astype(BF16)
        kd_sc[slot, d] = jnp.exp2(b_end - cr[...]).astype(BF16)
        dec_sc[slot, d] = jnp.exp2(b_end)
        a_sc[slot, d] = _hg_scores(qr, br, cr, d == 0)

    def outputs(d, k, slot):
        rows = chunk_rows(d, k, nc)
        v = ai[rows, :]
        outs = _hg_state_update(states(d), v, kd_sc[slot, d], dec_sc[slot, d], qe_sc[slot, d])
        accs[d][rows, :] = (jnp.concatenate(outs, axis=1)
                            + jnp.dot(a_sc[slot, d], _block_diag(v), preferred_element_type=F32))

    def step(k, slot, do_out=True, do_gates=True, do_scores=True):
        for d in range(2):
            if do_out:
                outputs(d, k - 2, slot)
            if do_gates:
                gates(d, k, slot)
        for d in range(2):
            if do_scores:
                scores(d, 1 - slot)

    def pair_body(kk, carry):
        step(2 * kk, 0)
        step(2 * kk + 1, 1)
        return carry

    def final_body(i, carry):
        rows = pl.ds(pl.multiple_of(i * c, c), c)
        tot = acc_f[rows, :] + acc_b[rows, :]
        gate = _silu(az[rows, :].astype(F32)) * nw_ref[...]
        for h in range(2):
            ls = slice(h * hw, (h + 1) * hw)
            th = tot[:, ls]
            ms = jnp.mean(th * th, axis=-1, keepdims=True)
            y_ref[rows, ls] = (th * lax.rsqrt(ms + NORM_EPS) * gate[:, ls]).astype(y_ref.dtype)
        return carry

    lax.fori_loop(0, ncc, ctx_body, 0, unroll=2)
    step(0, 0, do_out=False, do_scores=False)
    step(1, 1, do_out=False)
    lax.fori_loop(1, nc // 2, pair_body, 0)
    step(nc, 0, do_gates=False)
    step(nc + 1, 1, do_gates=False, do_scores=False)
    lax.fori_loop(0, nc, final_body, 0, unroll=4)


def _hgrn2(u_x, u_c, w_a, hg_lb, hg_norm):
    bsz, t, _ = u_x.shape
    tc = u_c.shape[1]
    cb = 2 * HG_DK
    c = HG_CHUNK
    assert cb == COL_BLOCK
    seg = w_a // cb
    slots = hg_lb.shape[1]
    lb2 = hg_lb.reshape(2 * slots, w_a).astype(F32)

    def col(k):
        return lambda b, p: (b, 0, k * seg + p)

    ctx_specs = [pl.BlockSpec((None, tc, cb), col(k)) for k in (0, 1, 2, 3)]
    lat_specs = [pl.BlockSpec((None, t, cb), col(k)) for k in (0, 1, 2, 3, 4)]
    return pl.pallas_call(
        functools.partial(_hgrn2_kernel, slots=slots),
        out_shape=jax.ShapeDtypeStruct((bsz, t, w_a), BF16),
        grid=(bsz, seg),
        in_specs=ctx_specs + lat_specs + [
            pl.BlockSpec((2 * slots, cb), lambda b, p: (0, p)),
            pl.BlockSpec((1, cb), lambda b, p: (0, p))],
        out_specs=pl.BlockSpec((None, t, cb), lambda b, p: (b, 0, p)),
        scratch_shapes=[pltpu.VMEM((t, cb), F32),
                        pltpu.VMEM((t, cb), F32),
                        pltpu.VMEM((4, HG_DK, HG_DK), F32),
                        pltpu.VMEM((2, 2, c, cb), F32),
                        pltpu.VMEM((2, 2, c, cb), F32),
                        pltpu.VMEM((2, 2, c, cb), F32),
                        pltpu.VMEM((2, 2, c, 2 * c), BF16),
                        pltpu.VMEM((2, 2, c, cb), BF16),
                        pltpu.VMEM((2, 2, c, cb), BF16),
                        pltpu.VMEM((2, 2, 1, cb), F32)],
        compiler_params=_params(2),
        name="hgrn2",
    )(u_c, u_c, u_c, u_c, u_x, u_x, u_x, u_x, u_x, lb2, hg_norm.reshape(1, w_a).astype(F32))


def _ml_gates(a_row, b_row, amax_row, m_prev, fwd):
    end = a_row.shape[1] - 1 if fwd else 0
    b_end = b_row[:, end:end + 1]
    big_m = jnp.maximum(m_prev, amax_row)
    m_row = b_row + big_m
    m_new = m_row[:, end:end + 1]
    w_s = jnp.exp(b_end + a_row - m_new)
    decay = jnp.exp(b_end + m_prev - m_new)
    return big_m, m_row, m_new, w_s, decay


def _ml_state_lhs(v, w_s):
    vt = v.astype(F32).T
    lhs = jnp.concatenate([vt * w_s, jnp.broadcast_to(w_s, (ML_PAD, vt.shape[1]))], axis=0)
    return vt, lhs.astype(BF16)


def _mlstm_kernel(cq, ck, cv, cgc, cgr, xq, xk, xv, xo, xz, xgc, xgr, nw_ref, y_ref,
                  acc_f, acc_b, st_ref, wt_sc, vt_sc, lhs_sc, row_sc):
    c = ML_CHUNK
    ncc = cq.shape[0] // c
    nc = xq.shape[0] // c
    dh = xq.shape[1]
    assert nc % 2 == 0
    head = pl.program_id(1)
    nh = pl.num_programs(1)
    accs = (acc_f, acc_b)
    st_ref[...] = jnp.zeros_like(st_ref)
    lane = lax.broadcasted_iota(jnp.int32, (c, LANES), 1)
    lower, upper = _tri_masks(c)

    def chunk_rows(d, k, n):
        return pl.ds(pl.multiple_of((k if d == 0 else n - 1 - k) * c, c), c)

    def gate_rows(grr, d, rows):
        slot = d * nh + head
        return (grr[pl.ds(slot, 1), rows], grr[pl.ds(2 * nh + slot, 1), rows],
                grr[pl.ds(4 * nh + slot, 1), rows])

    def ctx_body(k, ms):
        out = []
        for d in range(2):
            rows = chunk_rows(d, k, ncc)
            _, _, m_new, w_s, decay = _ml_gates(*gate_rows(cgr, d, rows), ms[d], d == 0)
            _, lhs = _ml_state_lhs(cv[rows, :], w_s)
            st_ref[d] = decay * st_ref[d] + jnp.dot(lhs, ck[rows, :], preferred_element_type=F32)
            out.append(m_new)
        return tuple(out)

    def score_stage(d, k, slot, m_prev):
        rows = chunk_rows(d, k, nc)
        big_m, m_row, m_new, w_s, decay = _ml_gates(*gate_rows(xgr, d, rows), m_prev, d == 0)
        vt, lhs = _ml_state_lhs(xv[rows, :], w_s)
        lhs_sc[slot, d] = lhs
        vt_sc[slot, d] = vt.astype(BF16)
        a_col = jnp.sum(jnp.where(lane == d * nh + head, xgc[rows, :], 0.0), axis=1, keepdims=True)
        mask = upper if d == 0 else lower
        w_t = (jnp.exp(jnp.where(mask, a_col - big_m, -jnp.inf))
               * lax.dot_general(xk[rows, :], xq[rows, :], _NT, preferred_element_type=F32))
        wt_sc[slot, d] = w_t
        row_sc[slot, d, 0:1, :] = jnp.sum(w_t, axis=0, keepdims=True)
        row_sc[slot, d, 1:2, :] = jnp.exp(m_prev - big_m)
        row_sc[slot, d, 2:3, :] = jnp.exp(-m_row)
        row_sc[slot, d, 3:4, :] = jnp.broadcast_to(decay, (1, c))
        return m_new

    def output_stage(d, k, slot):
        rows = chunk_rows(d, k, nc)
        st = st_ref[d]
        r = lax.dot_general(st.astype(BF16), xq[rows, :], _NT, preferred_element_type=F32)
        w_inter = row_sc[slot, d, 1:2, :]
        den = w_inter * r[dh:dh + 1, :] + row_sc[slot, d, 0:1, :]
        inv = 1.0 / jnp.maximum(jnp.abs(den), row_sc[slot, d, 2:3, :])
        accs[d][:, rows] = (r[:dh, :] * (w_inter * inv)
                            + jnp.dot(vt_sc[slot, d], (wt_sc[slot, d] * inv).astype(BF16),
                                      preferred_element_type=F32))
        st_ref[d] = (row_sc[slot, d, 3:4, 0:1] * st
                     + jnp.dot(lhs_sc[slot, d], xk[rows, :], preferred_element_type=F32))

    def pair_body(kk, ms):
        for slot in range(2):
            k = 2 * kk + slot
            nxt = jnp.minimum(k + 1, nc - 1)
            for d in range(2):
                output_stage(d, k, slot)
            ms = tuple(score_stage(d, nxt, 1 - slot, ms[d]) for d in range(2))
        return ms

    def final_body(i, carry):
        rows = pl.ds(pl.multiple_of(i * c, c), c)
        tot = acc_f[:, rows] + acc_b[:, rows]
        mu = jnp.mean(tot, axis=0, keepdims=True)
        tc_ = tot - mu
        var = jnp.mean(tc_ * tc_, axis=0, keepdims=True)
        normed = (tc_ * lax.rsqrt(var + NORM_EPS)).T * nw_ref[...]
        gate = jax.nn.sigmoid(xo[rows, :].astype(F32)) * _silu(xz[rows, :].astype(F32))
        y_ref[rows, :] = (normed * gate).astype(y_ref.dtype)
        return carry

    ms = (jnp.zeros((1, 1), F32), jnp.zeros((1, 1), F32))
    ms = lax.fori_loop(0, ncc, ctx_body, ms)
    ms = tuple(score_stage(d, 0, 0, ms[d]) for d in range(2))
    lax.fori_loop(0, nc // 2, pair_body, ms)
    lax.fori_loop(0, nc, final_body, 0)


def _mlstm(qk_x, qk_c, u_x, u_c, v_col, o_col, z_col, gates_x, gates_c, ml_norm, w_b):
    bsz, t, _ = u_x.shape
    tc = u_c.shape[1]
    nh = w_b // ML_DH
    cb = ML_DH
    assert cb == COL_BLOCK and t % (2 * ML_CHUNK) == 0 and tc % ML_CHUNK == 0

    def col(k):
        return lambda b, h: (b, 0, k + h)

    def seq_specs(n, with_gates):
        specs = [pl.BlockSpec((None, n, cb), col(0)),
                 pl.BlockSpec((None, n, cb), col(nh)),
                 pl.BlockSpec((None, n, cb), col(v_col // cb))]
        if with_gates:
            specs += [pl.BlockSpec((None, n, cb), col(o_col // cb)),
                      pl.BlockSpec((None, n, cb), col(z_col // cb))]
        specs += [pl.BlockSpec((n, LANES), lambda b, h: (b, 0)),
                  pl.BlockSpec((GATE_ROWS, n), lambda b, h: (0, b))]
        return specs

    return pl.pallas_call(
        _mlstm_kernel,
        out_shape=jax.ShapeDtypeStruct((bsz, t, w_b), BF16),
        grid=(bsz, nh),
        in_specs=seq_specs(tc, False) + seq_specs(t, True) + [pl.BlockSpec((1, cb), lambda b, h: (0, h))],
        out_specs=pl.BlockSpec((None, t, cb), lambda b, h: (b, 0, h)),
        scratch_shapes=[pltpu.VMEM((cb, t), F32),
                        pltpu.VMEM((cb, t), F32),
                        pltpu.VMEM((2, ML_DH + ML_PAD, ML_DH), F32),
                        pltpu.VMEM((2, 2, ML_CHUNK, ML_CHUNK), F32),
                        pltpu.VMEM((2, 2, ML_DH, ML_CHUNK), BF16),
                        pltpu.VMEM((2, 2, ML_DH + ML_PAD, ML_CHUNK), BF16),
                        pltpu.VMEM((2, 2, 8, ML_CHUNK), F32)],
        compiler_params=_params(2),
        name="mlstm",
    )(qk_c, qk_c, u_c, gates_c[0], gates_c[1],
      qk_x, qk_x, u_x, u_x, u_x, gates_x[0], gates_x[1],
      ml_norm.reshape(1, w_b).astype(F32))


def _outproj_kernel(ya_ref, yb_ref, x_ref, gate_ref, wa_ref, wb_ref, g_ref, b_ref, o_ref):
    y = (jnp.dot(ya_ref[...], wa_ref[...], preferred_element_type=F32)
         + jnp.dot(yb_ref[...], wb_ref[...], preferred_element_type=F32))
    r = ALPHA * x_ref[...] + gate_ref[...] * y
    mu = jnp.mean(r, axis=-1, keepdims=True)
    rc = r - mu
    var = jnp.mean(rc * rc, axis=-1, keepdims=True)
    o_ref[...] = rc * lax.rsqrt(var + LN_EPS) * g_ref[...] + b_ref[...]


def _outproj(y_a, y_b, x2, mod3, t, w_oa, w_ob, ln_g, ln_b):
    rows, d = x2.shape
    tm = 256
    wa, wb = y_a.shape[1], y_b.shape[1]
    return pl.pallas_call(
        _outproj_kernel,
        out_shape=jax.ShapeDtypeStruct((rows, d), F32),
        grid=(rows // tm,),
        in_specs=[pl.BlockSpec((tm, wa), lambda i: (i, 0)),
                  pl.BlockSpec((tm, wb), lambda i: (i, 0)),
                  pl.BlockSpec((tm, d), lambda i: (i, 0)),
                  pl.BlockSpec((None, 1, d), lambda i: ((i * tm) // t, 0, 2)),
                  pl.BlockSpec((wa, d), lambda i: (0, 0)),
                  pl.BlockSpec((wb, d), lambda i: (0, 0)),
                  pl.BlockSpec((1, d), lambda i: (0, 0)),
                  pl.BlockSpec((1, d), lambda i: (0, 0))],
        out_specs=pl.BlockSpec((tm, d), lambda i: (i, 0)),
        compiler_params=_params(1),
        name="outproj",
    )(y_a, y_b, x2, mod3, w_oa, w_ob, ln_g.reshape(1, d).astype(F32), ln_b.reshape(1, d).astype(F32))


def kernel(x, c, ctx, c_ctx, w_mod, b_mod, w_in, conv_w, conv_b, hg_lb, ml_gate_b, hg_norm_w,
           ml_norm_w, w_out, ln_g, ln_b):
    bsz, t, d = x.shape
    tc = ctx.shape[1]
    assert w_in.shape[0] == DEPTH and hg_lb.shape[1] == DEPTH + 1
    d_inner = w_out.shape[1]
    w_a = hg_lb.shape[2]
    w_b = d_inner - w_a
    nh = ml_gate_b.shape[2]
    nu = 5 * w_a + 5 * w_b
    assert nh * ML_DH == w_b and w_in.shape[2] == nu + 4 * nh and t % GRID_W == 0

    mod_rows = 8 * ((bsz + 1 + 7) // 8)
    cc = jnp.zeros((mod_rows, d), F32).at[:bsz].set(c).at[bsz].set(c_ctx)
    mod3 = _modulation(cc, w_mod[0], b_mod[0]).reshape(mod_rows, 1, 3 * d)

    w_in_t = w_in[0].T
    w_main = _wcast(w_in_t, 0, nu, 512)
    w_g = _wcast(w_in_t, nu, LANES, LANES)
    tm = min(1024, t)
    u_x, g_x = _inproj(x.reshape(bsz * t, d), mod3, lambda i: (i * tm) // t, w_main, w_g, tm)
    u_c, g_c = _inproj(ctx.reshape(bsz * tc, d), mod3, lambda i: bsz, w_main, w_g, min(1024, bsz * tc))
    u_x = u_x.reshape(bsz, t, nu)
    u_c = u_c.reshape(bsz, tc, nu)

    gates_x = _gateprep(g_x, ml_gate_b[0], nh, ML_CHUNK)
    gates_c = _gateprep(g_c, ml_gate_b[0], nh, ML_CHUNK)

    qk_col = 5 * w_a
    qk_x = _short_conv(u_x, qk_col, 2 * w_b, conv_w[0], conv_b[0], grid=True)
    qk_c = _short_conv(u_c, qk_col, 2 * w_b, conv_w[0], conv_b[0], grid=False)

    y_a = _hgrn2(u_x, u_c, w_a, hg_lb, hg_norm_w[0])
    y_b = _mlstm(qk_x, qk_c, u_x, u_c, qk_col + 2 * w_b, qk_col + 3 * w_b, qk_col + 4 * w_b,
                 gates_x, gates_c, ml_norm_w[0], w_b)

    w_o = w_out[0].astype(BF16)
    out = _outproj(y_a.reshape(bsz * t, w_a), y_b.reshape(bsz * t, w_b), x.reshape(bsz * t, d), mod3, t,
                   w_o[:w_a], w_o[w_a:], ln_g[0], ln_b[0])
    return out.reshape(bsz, t, d)
```

```python
import functools

import jax
import jax.numpy as jnp
from jax import lax
from jax.experimental import pallas as pl
from jax.experimental.pallas import tpu as pltpu

F32 = jnp.float32
BF16 = jnp.bfloat16

LN_EPS = 1e-5
NORM_EPS = 1e-6
DEPTH = 1
ALPHA = (2 * DEPTH) ** 0.25

HG_DK = 128
ML_DH = 256
GRID_W = 64
CONV_K = 3

LANES = 128
HG_CHUNK = 64
HG_SUB = 8
LOG2E = 1.4426950408889634
ML_CHUNK = 256
ML_PAD = 16
GATE_ROWS = 24
COL_BLOCK = 256
VMEM_LIMIT = 56 * 1024 * 1024

_NT = (((1,), (1,)), ((), ()))
_TN = (((0,), (0,)), ((), ()))


def _params(n_grid):
    return pltpu.CompilerParams(dimension_semantics=("arbitrary",) * n_grid,
                                vmem_limit_bytes=VMEM_LIMIT)


def _silu(a):
    return a * jax.nn.sigmoid(a)


def _log_sigmoid(a):
    return jnp.minimum(a, 0.0) - jnp.log1p(jnp.exp(-jnp.abs(a)))


def _split3(a):
    hi = a.astype(BF16)
    r1 = a - hi.astype(F32)
    mid = r1.astype(BF16)
    lo = (r1 - mid.astype(F32)).astype(BF16)
    return hi, mid, lo


def _tri_left(tri, a):
    return sum(jnp.dot(tri, t, preferred_element_type=F32) for t in _split3(a))


def _tri_right(a, tri):
    return sum(jnp.dot(t, tri, preferred_element_type=F32) for t in _split3(a))


def _tri_masks(n):
    r = lax.broadcasted_iota(jnp.int32, (n, n), 0)
    c = lax.broadcasted_iota(jnp.int32, (n, n), 1)
    return r >= c, r <= c


def _mod_kernel(c_ref, w_ref, b_ref, o_ref):
    s = _silu(c_ref[...])
    o_ref[...] = jnp.dot(s, w_ref[...], precision=lax.Precision.HIGHEST,
                         preferred_element_type=F32) + b_ref[...]


def _modulation(cc, w_mod, b_mod):
    rows, d = cc.shape
    n = w_mod.shape[1]
    tn = 512
    return pl.pallas_call(
        _mod_kernel,
        out_shape=jax.ShapeDtypeStruct((rows, n), F32),
        grid=(n // tn,),
        in_specs=[pl.BlockSpec((rows, d), lambda j: (0, 0)),
                  pl.BlockSpec((d, tn), lambda j: (0, j)),
                  pl.BlockSpec((1, tn), lambda j: (0, j))],
        out_specs=pl.BlockSpec((rows, tn), lambda j: (0, j)),
        compiler_params=_params(1),
        name="mod",
    )(cc, w_mod, b_mod.reshape(1, n))


def _wcast_kernel(wt_ref, o_ref, *, valid):
    w = wt_ref[...]
    if valid < w.shape[0]:
        w = jnp.where(lax.broadcasted_iota(jnp.int32, w.shape, 0) < valid, w, 0.0)
    o_ref[...] = w.T.astype(o_ref.dtype)


def _wcast(wt, col0, n_out, tn):
    n, d = wt.shape
    assert col0 % tn == 0 and n_out % tn == 0
    return pl.pallas_call(
        functools.partial(_wcast_kernel, valid=min(tn, n - col0)),
        out_shape=jax.ShapeDtypeStruct((d, n_out), BF16),
        grid=(n_out // tn,),
        in_specs=[pl.BlockSpec((tn, d), lambda j: (col0 // tn + j, 0))],
        out_specs=pl.BlockSpec((d, tn), lambda j: (0, j)),
        compiler_params=_params(1),
        name="wcast",
    )(wt)


def _inproj_kernel(x_ref, sh_ref, sc_ref, w_ref, wg_ref, u_ref, g_ref, h_ref, *, sub):
    @pl.when(pl.program_id(1) == 0)
    def _():
        shift = sh_ref[...]
        scale1 = 1.0 + sc_ref[...]

        def body(r, carry):
            rows = pl.ds(pl.multiple_of(r * sub, sub), sub)
            xv = x_ref[rows, :]
            mu = jnp.mean(xv, axis=-1, keepdims=True)
            xc = xv - mu
            var = jnp.mean(xc * xc, axis=-1, keepdims=True)
            h_ref[rows, :] = (xc * lax.rsqrt(var + LN_EPS) * scale1 + shift).astype(BF16)
            return carry

        lax.fori_loop(0, x_ref.shape[0] // sub, body, 0)
        g_ref[...] = jnp.dot(h_ref[...], wg_ref[...], preferred_element_type=F32)

    u_ref[...] = jnp.dot(h_ref[...], w_ref[...], preferred_element_type=F32).astype(BF16)


def _inproj(x2, mod3, mod_row, w_main, w_g, tm):
    rows, d = x2.shape
    nu = w_main.shape[1]
    tn = 1024 if nu % 1024 == 0 else 512
    assert nu % tn == 0 and rows % tm == 0
    sub = min(256, tm)
    return pl.pallas_call(
        functools.partial(_inproj_kernel, sub=sub),
        out_shape=(jax.ShapeDtypeStruct((rows, nu), BF16),
                   jax.ShapeDtypeStruct((rows, LANES), F32)),
        grid=(rows // tm, nu // tn),
        in_specs=[pl.BlockSpec((tm, d), lambda i, j: (i, 0)),
                  pl.BlockSpec((None, 1, d), lambda i, j: (mod_row(i), 0, 0)),
                  pl.BlockSpec((None, 1, d), lambda i, j: (mod_row(i), 0, 1)),
                  pl.BlockSpec((d, tn), lambda i, j: (0, j)),
                  pl.BlockSpec((d, LANES), lambda i, j: (0, 0))],
        out_specs=(pl.BlockSpec((tm, tn), lambda i, j: (i, j)),
                   pl.BlockSpec((tm, LANES), lambda i, j: (i, 0))),
        scratch_shapes=[pltpu.VMEM((tm, d), BF16)],
        compiler_params=_params(2),
        name="inproj",
    )(x2, mod3, mod3, w_main, w_g)


def _gateprep_kernel(g_ref, bias_ref, gc_ref, gr_ref, *, nh, c):
    lower, upper = _tri_masks(c)
    tril = jnp.where(lower, 1.0, 0.0).astype(BF16)
    triu = jnp.where(upper, 1.0, 0.0).astype(BF16)
    lane = lax.broadcasted_iota(jnp.int32, (c, LANES), 1)
    fwd_f = (lane >= 2 * nh) & (lane < 3 * nh)
    bwd_f = (lane >= 3 * nh) & (lane < 4 * nh)
    row = lax.broadcasted_iota(jnp.int32, (GATE_ROWS, c), 0)
    pos = lax.broadcasted_iota(jnp.int32, (GATE_ROWS, c), 1)
    fwd_r = (row >= 2 * nh) & (row < 3 * nh)
    bwd_r = (row >= 3 * nh) & (row < 4 * nh)

    for r0 in range(0, g_ref.shape[0], c):
        pre = g_ref[r0:r0 + c, :] + bias_ref[...]
        lf = _log_sigmoid(pre)
        b = jnp.where(fwd_f, _tri_left(tril, lf), jnp.where(bwd_f, _tri_left(triu, lf), 0.0))
        gc_ref[r0:r0 + c, :] = pre - pltpu.roll(b, LANES - 2 * nh, axis=1)

        pre_t = pre.T[:GATE_ROWS]
        lf_t = _log_sigmoid(pre_t)
        b_t = jnp.where(fwd_r, _tri_right(lf_t, triu), jnp.where(bwd_r, _tri_right(lf_t, tril), 0.0))
        a_t = pre_t - pltpu.roll(b_t, GATE_ROWS - 2 * nh, axis=0)
        pmax, smax = a_t, a_t
        shift = 1
        while shift < c:
            pmax = jnp.maximum(pmax, jnp.where(pos >= shift, pltpu.roll(pmax, shift, axis=1), -jnp.inf))
            smax = jnp.maximum(smax, jnp.where(pos < c - shift, pltpu.roll(smax, c - shift, axis=1), -jnp.inf))
            shift *= 2
        amax_t = pltpu.roll(jnp.where(row < nh, pmax, smax), 4 * nh, axis=0)
        gr_ref[:, r0:r0 + c] = jnp.where(row < 2 * nh, a_t, jnp.where(row < 4 * nh, b_t, amax_t))


def _gateprep(g, bias, nh, chunk):
    rows = g.shape[0]
    assert 6 * nh <= GATE_ROWS
    blk = 4 * chunk if rows % (4 * chunk) == 0 else chunk
    bias_row = jnp.zeros((1, LANES), F32).at[0, :4 * nh].set(bias.reshape(-1))
    return pl.pallas_call(
        functools.partial(_gateprep_kernel, nh=nh, c=chunk),
        out_shape=(jax.ShapeDtypeStruct((rows, LANES), F32),
                   jax.ShapeDtypeStruct((GATE_ROWS, rows), F32)),
        grid=(rows // blk,),
        in_specs=[pl.BlockSpec((blk, LANES), lambda i: (i, 0)),
                  pl.BlockSpec((1, LANES), lambda i: (0, 0))],
        out_specs=(pl.BlockSpec((blk, LANES), lambda i: (i, 0)),
                   pl.BlockSpec((GATE_ROWS, blk), lambda i: (0, i))),
        compiler_params=_params(1),
        name="gateprep",
    )(g, bias_row)


def _conv_kernel(x_ref, w_ref, b_ref, o_ref, *, rows, width, taps):
    cb = x_ref.shape[-1]
    j = pl.program_id(1)
    scale = jnp.where(j >= pl.num_programs(1) // 2, ML_DH ** -0.5, 1.0).astype(F32)
    w = w_ref[...]
    bias = b_ref[...]
    col = lax.broadcasted_iota(jnp.int32, (width, cb), 0)
    first = col == 0
    last = col == width - 1

    def one_row(r):
        left = mid = right = None
        for ki in taps:
            rr = r + (ki - 1)
            valid = jnp.where((rr >= 0) & (rr < rows), 1.0, 0.0).astype(F32)
            wk = w[3 * ki:3 * ki + 3, :] * valid
            start = pl.multiple_of(jnp.clip(rr, 0, rows - 1) * width, width)
            xv = x_ref[pl.ds(start, width), :].astype(F32)
            terms = [xv * wk[kj:kj + 1, :] for kj in range(3)]
            left, mid, right = terms if left is None else (left + terms[0], mid + terms[1], right + terms[2])
        out = (mid + jnp.where(first, 0.0, pltpu.roll(left, 1, axis=0))
               + jnp.where(last, 0.0, pltpu.roll(right, width - 1, axis=0)) + bias)
        dst = pl.multiple_of(r * width, width)
        o_ref[pl.ds(dst, width), :] = (_silu(out) * scale).astype(o_ref.dtype)

    def body(r, carry):
        one_row(r)
        return carry

    lax.fori_loop(0, rows, body, 0)


def _short_conv(u, col0, n_ch, conv_w, conv_b, grid):
    bsz, t, _ = u.shape
    cb = COL_BLOCK
    rows, width, taps = (t // GRID_W, GRID_W, (0, 1, 2)) if grid else (1, t, (1,))
    return pl.pallas_call(
        functools.partial(_conv_kernel, rows=rows, width=width, taps=taps),
        out_shape=jax.ShapeDtypeStruct((bsz, t, n_ch), BF16),
        grid=(bsz, n_ch // cb),
        in_specs=[pl.BlockSpec((None, t, cb), lambda b, j: (b, 0, col0 // cb + j)),
                  pl.BlockSpec((CONV_K * CONV_K, cb), lambda b, j: (0, j)),
                  pl.BlockSpec((1, cb), lambda b, j: (0, j))],
        out_specs=pl.BlockSpec((None, t, cb), lambda b, j: (b, 0, j)),
        compiler_params=_params(2),
        name="conv_grid" if grid else "conv_seq",
    )(u, conv_w.reshape(CONV_K * CONV_K, n_ch).astype(F32), conv_b.reshape(1, n_ch).astype(F32))


def _hg_prepare(qraw, z, lb, fwd):
    lower, upper = _tri_masks(z.shape[0])
    tri = jnp.where(lower if fwd else upper, 1.0, 0.0).astype(BF16)
    f = lb + (1.0 - lb) * jax.nn.sigmoid(z)
    b2 = _tri_left(tri, jnp.log(f) * LOG2E)
    c2 = b2 - jnp.log(1.0 - f) * LOG2E
    return _silu(qraw), b2, c2


def _block_diag(a):
    r, w = a.shape
    zero = jnp.zeros((r, w // 2), a.dtype)
    return jnp.concatenate([jnp.concatenate([a[:, :w // 2], zero], axis=1),
                            jnp.concatenate([zero, a[:, w // 2:]], axis=1)], axis=0)


def _hg_scores(q_ref, b_ref, c_ref, fwd):
    c, w = q_ref.shape
    sub = HG_SUB
    nb = c // sub
    assert 2 * c == LANES and c & (c - 1) == 0
    row = lax.broadcasted_iota(jnp.int32, (sub, LANES), 0)
    col = lax.broadcasted_iota(jnp.int32, (sub, LANES), 1) & (c - 1)
    src = col & (sub - 1)
    code = jnp.where((src <= row) if fwd else (src >= row), col, -1)
    kr = lax.broadcasted_iota(jnp.int32, (w, LANES), 0)
    kl = lax.broadcasted_iota(jnp.int32, (w, LANES), 1)
    sel = jnp.where((kr >= w // 2) == (kl >= c), 1.0, 0.0).astype(BF16)

    units = []
    for j in range(nb):
        r0 = j * sub
        qb = q_ref[r0:r0 + sub, :]
        bb = b_ref[r0:r0 + sub, :]
        cb = c_ref[r0:r0 + sub, :]
        for s in range(sub):
            units.append(qb * jnp.exp2(bb - cb[s:s + 1, :]))
    red = jnp.dot(jnp.concatenate(units, axis=0).astype(BF16), sel, preferred_element_type=F32)

    blocks = []
    for j in range(nb):
        r0 = j * sub
        blk = jnp.zeros((sub, LANES), F32)
        for s in range(sub):
            u0 = (j * sub + s) * sub
            blk = jnp.where(code == r0 + s, red[u0:u0 + sub], blk)
        lo, hi = (0, r0) if fwd else (r0 + sub, c)
        if hi > lo:
            beta = b_ref[pl.ds(r0 - 1 if fwd else r0 + sub, 1), :]
            pieces = [jnp.zeros((lo, w), F32)] if lo else []
            pieces.append(jnp.exp2(beta - c_ref[lo:hi, :]))
            if hi < c:
                pieces.append(jnp.zeros((c - hi, w), F32))
            khd = _block_diag(jnp.concatenate(pieces, axis=0).astype(BF16))
            qh = (q_ref[r0:r0 + sub, :] * jnp.exp2(b_ref[r0:r0 + sub, :] - beta)).astype(BF16)
            blk = blk + lax.dot_general(qh, khd, _NT, preferred_element_type=F32)
        blocks.append(blk)
    return jnp.concatenate(blocks, axis=0).astype(BF16)


def _hg_state_update(st_refs, v, k_dec, dec, qe=None):
    hw = v.shape[1] // 2
    outs = []
    for h, st_ref in enumerate(st_refs):
        ls = slice(h * hw, (h + 1) * hw)
        st = st_ref[...]
        st_ref[...] = st * dec[:, ls] + lax.dot_general(v[:, ls], k_dec[:, ls], _TN, preferred_element_type=F32)
        if qe is not None:
            outs.append(lax.dot_general(qe[:, ls], st.astype(BF16), _NT, preferred_element_type=F32))
    return outs


def _hgrn2_kernel(cq, cff, cfb, ci, aq, aff, afb, ai, az, lb_ref, nw_ref, y_ref,
                  acc_f, acc_b, st_ref, q_sc, b_sc, c_sc, a_sc, qe_sc, kd_sc, dec_sc, *, slots):
    c = HG_CHUNK
    ncc = cq.shape[0] // c
    nc = aq.shape[0] // c
    hw = HG_DK
    assert nc % 2 == 0 and nc >= 4
    st_ref[...] = jnp.zeros_like(st_ref)
    accs = (acc_f, acc_b)
    lbs = []
    for d in range(2):
        hg = lb_ref[d * slots:(d + 1) * slots, :]
        e = jnp.exp(hg - jnp.max(hg, axis=0, keepdims=True))
        lbs.append(e[0:1, :] / jnp.sum(e, axis=0, keepdims=True))

    def chunk_rows(d, k, n):
        return pl.ds(pl.multiple_of((k if d == 0 else n - 1 - k) * c, c), c)

    def states(d):
        return st_ref.at[2 * d], st_ref.at[2 * d + 1]

    def ctx_body(k, carry):
        for d in range(2):
            rows = chunk_rows(d, k, ncc)
            _, b2, c2 = _hg_prepare(cq[rows, :].astype(F32), (cff, cfb)[d][rows, :].astype(F32), lbs[d], d == 0)
            end = c - 1 if d == 0 else 0
            b_end = b2[end:end + 1, :]
            _hg_state_update(states(d), ci[rows, :], jnp.exp2(b_end - c2).astype(BF16), jnp.exp2(b_end))
        return carry

    def gates(d, k, slot):
        rows = chunk_rows(d, k, nc)
        q, b2, c2 = _hg_prepare(aq[rows, :].astype(F32), (aff, afb)[d][rows, :].astype(F32), lbs[d], d == 0)
        q_sc[slot, d] = q
        b_sc[slot, d] = b2
        c_sc[slot, d] = c2

    def scores(d, slot):
        qr, br, cr = q_sc.at[slot, d], b_sc.at[slot, d], c_sc.at[slot, d]
        end = c - 1 if d == 0 else 0
        b_end = br[end:end + 1, :]
        qe_sc[slot, d] = (qr[...] * jnp.exp2(br[...])).astype(BF16)
        kd_sc[slot, d] = jnp.exp2(b_end - cr[...]).astype(BF16)
        dec_sc[slot, d] = jnp.exp2(b_end)
        a_sc[slot, d] = _hg_scores(qr, br, cr, d == 0)

    def outputs(d, k, slot):
        rows = chunk_rows(d, k, nc)
        v = ai[rows, :]
        outs = _hg_state_update(states(d), v, kd_sc[slot, d], dec_sc[slot, d], qe_sc[slot, d])
        accs[d][rows, :] = (jnp.concatenate(outs, axis=1)
                            + jnp.dot(a_sc[slot, d], _block_diag(v), preferred_element_type=F32))

    def step(k, slot, do_out=True, do_gates=True, do_scores=True):
        for d in range(2):
            if do_out:
                outputs(d, k - 2, slot)
            if do_gates:
                gates(d, k, slot)
        for d in range(2):
            if do_scores:
                scores(d, 1 - slot)

    def pair_body(kk, carry):
        step(2 * kk, 0)
        step(2 * kk + 1, 1)
        return carry

    def final_body(i, carry):
        rows = pl.ds(pl.multiple_of(i * c, c), c)
        tot = acc_f[rows, :] + acc_b[rows, :]
        gate = _silu(az[rows, :].astype(F32)) * nw_ref[...]
        for h in range(2):
            ls = slice(h * hw, (h + 1) * hw)
            th = tot[:, ls]
            ms = jnp.mean(th * th, axis=-1, keepdims=True)
            y_ref[rows, ls] = (th * lax.rsqrt(ms + NORM_EPS) * gate[:, ls]).astype(y_ref.dtype)
        return carry

    lax.fori_loop(0, ncc, ctx_body, 0, unroll=2)
    step(0, 0, do_out=False, do_scores=False)
    step(1, 1, do_out=False)
    lax.fori_loop(1, nc // 2, pair_body, 0)
    step(nc, 0, do_gates=False)
    step(nc + 1, 1, do_gates=False, do_scores=False)
    lax.fori_loop(0, nc, final_body, 0, unroll=4)


def _hgrn2(u_x, u_c, w_a, hg_lb, hg_norm):
    bsz, t, _ = u_x.shape
    tc = u_c.shape[1]
    cb = 2 * HG_DK
    c = HG_CHUNK
    assert cb == COL_BLOCK
    seg = w_a // cb
    slots = hg_lb.shape[1]
    lb2 = hg_lb.reshape(2 * slots, w_a).astype(F32)

    def col(k):
        return lambda b, p: (b, 0, k * seg + p)

    ctx_specs = [pl.BlockSpec((None, tc, cb), col(k)) for k in (0, 1, 2, 3)]
    lat_specs = [pl.BlockSpec((None, t, cb), col(k)) for k in (0, 1, 2, 3, 4)]
    return pl.pallas_call(
        functools.partial(_hgrn2_kernel, slots=slots),
        out_shape=jax.ShapeDtypeStruct((bsz, t, w_a), BF16),
        grid=(bsz, seg),
        in_specs=ctx_specs + lat_specs + [
            pl.BlockSpec((2 * slots, cb), lambda b, p: (0, p)),
            pl.BlockSpec((1, cb), lambda b, p: (0, p))],
        out_specs=pl.BlockSpec((None, t, cb), lambda b, p: (b, 0, p)),
        scratch_shapes=[pltpu.VMEM((t, cb), F32),
                        pltpu.VMEM((t, cb), F32),
                        pltpu.VMEM((4, HG_DK, HG_DK), F32),
                        pltpu.VMEM((2, 2, c, cb), F32),
                        pltpu.VMEM((2, 2, c, cb), F32),
                        pltpu.VMEM((2, 2, c, cb), F32),
                        pltpu.VMEM((2, 2, c, 2 * c), BF16),
                        pltpu.VMEM((2, 2, c, cb), BF16),
                        pltpu.VMEM((2, 2, c, cb), BF16),
                        pltpu.VMEM((2, 2, 1, cb), F32)],
        compiler_params=_params(2),
        name="hgrn2",
    )(u_c, u_c, u_c, u_c, u_x, u_x, u_x, u_x, u_x, lb2, hg_norm.reshape(1, w_a).astype(F32))


def _ml_gates(a_row, b_row, amax_row, m_prev, fwd):
    end = a_row.shape[1] - 1 if fwd else 0
    b_end = b_row[:, end:end + 1]
    big_m = jnp.maximum(m_prev, amax_row)
    m_row = b_row + big_m
    m_new = m_row[:, end:end + 1]
    w_s = jnp.exp(b_end + a_row - m_new)
    decay = jnp.exp(b_end + m_prev - m_new)
    return big_m, m_row, m_new, w_s, decay


def _ml_state_lhs(v, w_s):
    vt = v.astype(F32).T
    lhs = jnp.concatenate([vt * w_s, jnp.broadcast_to(w_s, (ML_PAD, vt.shape[1]))], axis=0)
    return vt, lhs.astype(BF16)


def _mlstm_kernel(cq, ck, cv, cgc, cgr, xq, xk, xv, xo, xz, xgc, xgr, nw_ref, y_ref,
                  acc_f, acc_b, st_ref, wt_sc, vt_sc, lhs_sc, row_sc):
    c = ML_CHUNK
    ncc = cq.shape[0] // c
    nc = xq.shape[0] // c
    dh = xq.shape[1]
    assert nc % 2 == 0
    head = pl.program_id(1)
    nh = pl.num_programs(1)
    accs = (acc_f, acc_b)
    st_ref[...] = jnp.zeros_like(st_ref)
    lane = lax.broadcasted_iota(jnp.int32, (c, LANES), 1)
    lower, upper = _tri_masks(c)

    def chunk_rows(d, k, n):
        return pl.ds(pl.multiple_of((k if d == 0 else n - 1 - k) * c, c), c)

    def gate_rows(grr, d, rows):
        slot = d * nh + head
        return (grr[pl.ds(slot, 1), rows], grr[pl.ds(2 * nh + slot, 1), rows],
                grr[pl.ds(4 * nh + slot, 1), rows])

    def ctx_body(k, ms):
        out = []
        for d in range(2):
            rows = chunk_rows(d, k, ncc)
            _, _, m_new, w_s, decay = _ml_gates(*gate_rows(cgr, d, rows), ms[d], d == 0)
            _, lhs = _ml_state_lhs(cv[rows, :], w_s)
            st_ref[d] = decay * st_ref[d] + jnp.dot(lhs, ck[rows, :], preferred_element_type=F32)
            out.append(m_new)
        return tuple(out)

    def score_stage(d, k, slot, m_prev):
        rows = chunk_rows(d, k, nc)
        big_m, m_row, m_new, w_s, decay = _ml_gates(*gate_rows(xgr, d, rows), m_prev, d == 0)
        vt, lhs = _ml_state_lhs(xv[rows, :], w_s)
        lhs_sc[slot, d] = lhs
        vt_sc[slot, d] = vt.astype(BF16)
        a_col = jnp.sum(jnp.where(lane == d * nh + head, xgc[rows, :], 0.0), axis=1, keepdims=True)
        mask = upper if d == 0 else lower
        w_t = (jnp.exp(jnp.where(mask, a_col - big_m, -jnp.inf))
               * lax.dot_general(xk[rows, :], xq[rows, :], _NT, preferred_element_type=F32))
        wt_sc[slot, d] = w_t
        row_sc[slot, d, 0:1, :] = jnp.sum(w_t, axis=0, keepdims=True)
        row_sc[slot, d, 1:2, :] = jnp.exp(m_prev - big_m)
        row_sc[slot, d, 2:3, :] = jnp.exp(-m_row)
        row_sc[slot, d, 3:4, :] = jnp.broadcast_to(decay, (1, c))
        return m_new

    def output_stage(d, k, slot):
        rows = chunk_rows(d, k, nc)
        st = st_ref[d]
        r = lax.dot_general(st.astype(BF16), xq[rows, :], _NT, preferred_element_type=F32)
        w_inter = row_sc[slot, d, 1:2, :]
        den = w_inter * r[dh:dh + 1, :] + row_sc[slot, d, 0:1, :]
        inv = 1.0 / jnp.maximum(jnp.abs(den), row_sc[slot, d, 2:3, :])
        accs[d][:, rows] = (r[:dh, :] * (w_inter * inv)
                            + jnp.dot(vt_sc[slot, d], (wt_sc[slot, d] * inv).astype(BF16),
                                      preferred_element_type=F32))
        st_ref[d] = (row_sc[slot, d, 3:4, 0:1] * st
                     + jnp.dot(lhs_sc[slot, d], xk[rows, :], preferred_element_type=F32))

    def pair_body(kk, ms):
        for slot in range(2):
            k = 2 * kk + slot
            nxt = jnp.minimum(k + 1, nc - 1)
            for d in range(2):
                output_stage(d, k, slot)
            ms = tuple(score_stage(d, nxt, 1 - slot, ms[d]) for d in range(2))
        return ms

    def final_body(i, carry):
        rows = pl.ds(pl.multiple_of(i * c, c), c)
        tot = acc_f[:, rows] + acc_b[:, rows]
        mu = jnp.mean(tot, axis=0, keepdims=True)
        tc_ = tot - mu
        var = jnp.mean(tc_ * tc_, axis=0, keepdims=True)
        normed = (tc_ * lax.rsqrt(var + NORM_EPS)).T * nw_ref[...]
        gate = jax.nn.sigmoid(xo[rows, :].astype(F32)) * _silu(xz[rows, :].astype(F32))
        y_ref[rows, :] = (normed * gate).astype(y_ref.dtype)
        return carry

    ms = (jnp.zeros((1, 1), F32), jnp.zeros((1, 1), F32))
    ms = lax.fori_loop(0, ncc, ctx_body, ms)
    ms = tuple(score_stage(d, 0, 0, ms[d]) for d in range(2))
    lax.fori_loop(0, nc // 2, pair_body, ms)
    lax.fori_loop(0, nc, final_body, 0)


def _mlstm(qk_x, qk_c, u_x, u_c, v_col, o_col, z_col, gates_x, gates_c, ml_norm, w_b):
    bsz, t, _ = u_x.shape
    tc = u_c.shape[1]
    nh = w_b // ML_DH
    cb = ML_DH
    assert cb == COL_BLOCK and t % (2 * ML_CHUNK) == 0 and tc % ML_CHUNK == 0

    def col(k):
        return lambda b, h: (b, 0, k + h)

    def seq_specs(n, with_gates):
        specs = [pl.BlockSpec((None, n, cb), col(0)),
                 pl.BlockSpec((None, n, cb), col(nh)),
                 pl.BlockSpec((None, n, cb), col(v_col // cb))]
        if with_gates:
            specs += [pl.BlockSpec((None, n, cb), col(o_col // cb)),
                      pl.BlockSpec((None, n, cb), col(z_col // cb))]
        specs += [pl.BlockSpec((n, LANES), lambda b, h: (b, 0)),
                  pl.BlockSpec((GATE_ROWS, n), lambda b, h: (0, b))]
        return specs

    return pl.pallas_call(
        _mlstm_kernel,
        out_shape=jax.ShapeDtypeStruct((bsz, t, w_b), BF16),
        grid=(bsz, nh),
        in_specs=seq_specs(tc, False) + seq_specs(t, True) + [pl.BlockSpec((1, cb), lambda b, h: (0, h))],
        out_specs=pl.BlockSpec((None, t, cb), lambda b, h: (b, 0, h)),
        scratch_shapes=[pltpu.VMEM((cb, t), F32),
                        pltpu.VMEM((cb, t), F32),
                        pltpu.VMEM((2, ML_DH + ML_PAD, ML_DH), F32),
                        pltpu.VMEM((2, 2, ML_CHUNK, ML_CHUNK), F32),
                        pltpu.VMEM((2, 2, ML_DH, ML_CHUNK), BF16),
                        pltpu.VMEM((2, 2, ML_DH + ML_PAD, ML_CHUNK), BF16),
                        pltpu.VMEM((2, 2, 8, ML_CHUNK), F32)],
        compiler_params=_params(2),
        name="mlstm",
    )(qk_c, qk_c, u_c, gates_c[0], gates_c[1],
      qk_x, qk_x, u_x, u_x, u_x, gates_x[0], gates_x[1],
      ml_norm.reshape(1, w_b).astype(F32))


def _outproj_kernel(ya_ref, yb_ref, x_ref, gate_ref, wa_ref, wb_ref, g_ref, b_ref, o_ref, *, sub):
    for r0 in range(0, x_ref.shape[0], sub):
        rows = slice(r0, r0 + sub)
        y = (jnp.dot(ya_ref[rows, :], wa_ref[...], preferred_element_type=F32)
             + jnp.dot(yb_ref[rows, :], wb_ref[...], preferred_element_type=F32))
        r = ALPHA * x_ref[rows, :] + gate_ref[...] * y
        mu = jnp.mean(r, axis=-1, keepdims=True)
        rc = r - mu
        var = jnp.mean(rc * rc, axis=-1, keepdims=True)
        o_ref[rows, :] = rc * lax.rsqrt(var + LN_EPS) * g_ref[...] + b_ref[...]


def _outproj(y_a, y_b, x2, mod3, t, w_oa, w_ob, ln_g, ln_b):
    rows, d = x2.shape
    tm = 512
    wa, wb = y_a.shape[1], y_b.shape[1]
    return pl.pallas_call(
        functools.partial(_outproj_kernel, sub=256),
        out_shape=jax.ShapeDtypeStruct((rows, d), F32),
        grid=(rows // tm,),
        in_specs=[pl.BlockSpec((tm, wa), lambda i: (i, 0)),
                  pl.BlockSpec((tm, wb), lambda i: (i, 0)),
                  pl.BlockSpec((tm, d), lambda i: (i, 0)),
                  pl.BlockSpec((None, 1, d), lambda i: ((i * tm) // t, 0, 2)),
                  pl.BlockSpec((wa, d), lambda i: (0, 0)),
                  pl.BlockSpec((wb, d), lambda i: (0, 0)),
                  pl.BlockSpec((1, d), lambda i: (0, 0)),
                  pl.BlockSpec((1, d), lambda i: (0, 0))],
        out_specs=pl.BlockSpec((tm, d), lambda i: (i, 0)),
        compiler_params=_params(1),
        name="outproj",
    )(y_a, y_b, x2, mod3, w_oa, w_ob, ln_g.reshape(1, d).astype(F32), ln_b.reshape(1, d).astype(F32))


def kernel(x, c, ctx, c_ctx, w_mod, b_mod, w_in, conv_w, conv_b, hg_lb, ml_gate_b, hg_norm_w,
           ml_norm_w, w_out, ln_g, ln_b):
    bsz, t, d = x.shape
    tc = ctx.shape[1]
    assert w_in.shape[0] == DEPTH and hg_lb.shape[1] == DEPTH + 1
    d_inner = w_out.shape[1]
    w_a = hg_lb.shape[2]
    w_b = d_inner - w_a
    nh = ml_gate_b.shape[2]
    nu = 5 * w_a + 5 * w_b
    assert nh * ML_DH == w_b and w_in.shape[2] == nu + 4 * nh and t % GRID_W == 0

    mod_rows = 8 * ((bsz + 1 + 7) // 8)
    cc = jnp.zeros((mod_rows, d), F32).at[:bsz].set(c).at[bsz].set(c_ctx)
    mod3 = _modulation(cc, w_mod[0], b_mod[0]).reshape(mod_rows, 1, 3 * d)

    w_in_t = w_in[0].T
    w_main = _wcast(w_in_t, 0, nu, 512)
    w_g = _wcast(w_in_t, nu, LANES, LANES)
    tm = min(1024, t)
    u_x, g_x = _inproj(x.reshape(bsz * t, d), mod3, lambda i: (i * tm) // t, w_main, w_g, tm)
    u_c, g_c = _inproj(ctx.reshape(bsz * tc, d), mod3, lambda i: bsz, w_main, w_g, min(1024, bsz * tc))
    u_x = u_x.reshape(bsz, t, nu)
    u_c = u_c.reshape(bsz, tc, nu)

    gates_x = _gateprep(g_x, ml_gate_b[0], nh, ML_CHUNK)
    gates_c = _gateprep(g_c, ml_gate_b[0], nh, ML_CHUNK)

    qk_col = 5 * w_a
    qk_x = _short_conv(u_x, qk_col, 2 * w_b, conv_w[0], conv_b[0], grid=True)
    qk_c = _short_conv(u_c, qk_col, 2 * w_b, conv_w[0], conv_b[0], grid=False)

    y_a = _hgrn2(u_x, u_c, w_a, hg_lb, hg_norm_w[0])
    y_b = _mlstm(qk_x, qk_c, u_x, u_c, qk_col + 2 * w_b, qk_col + 3 * w_b, qk_col + 4 * w_b,
                 gates_x, gates_c, ml_norm_w[0], w_b)

    w_o = w_out[0].astype(BF16)
    out = _outproj(y_a.reshape(bsz * t, w_a), y_b.reshape(bsz * t, w_b), x.reshape(bsz * t, d), mod3, t,
                   w_o[:w_a], w_o[w_a:], ln_g[0], ln_b[0])
    return out.reshape(bsz, t, d)
```

```python
import functools

import jax
import jax.numpy as jnp
from jax import lax
from jax.experimental import pallas as pl
from jax.experimental.pallas import tpu as pltpu

F32 = jnp.float32
BF16 = jnp.bfloat16

LN_EPS = 1e-5
NORM_EPS = 1e-6
DEPTH = 1
ALPHA = (2 * DEPTH) ** 0.25

HG_DK = 128
ML_DH = 256
GRID_W = 64
CONV_K = 3

LANES = 128
HG_CHUNK = 64
HG_SUB = 8
HG_SAFE_LOG2 = 64.0
LOG2E = 1.4426950408889634
ML_CHUNK = 256
ML_PAD = 16
GATE_ROWS = 24
COL_BLOCK = 256
VMEM_LIMIT = 56 * 1024 * 1024

_NT = (((1,), (1,)), ((), ()))
_TN = (((0,), (0,)), ((), ()))


def _params(n_grid):
    return pltpu.CompilerParams(dimension_semantics=("arbitrary",) * n_grid,
                                vmem_limit_bytes=VMEM_LIMIT)


def _silu(a):
    return a * jax.nn.sigmoid(a)


def _log_sigmoid(a):
    return jnp.minimum(a, 0.0) - jnp.log1p(jnp.exp(-jnp.abs(a)))


def _split3(a):
    hi = a.astype(BF16)
    r1 = a - hi.astype(F32)
    mid = r1.astype(BF16)
    lo = (r1 - mid.astype(F32)).astype(BF16)
    return hi, mid, lo


def _tri_left(tri, a):
    return sum(jnp.dot(tri, t, preferred_element_type=F32) for t in _split3(a))


def _tri_right(a, tri):
    return sum(jnp.dot(t, tri, preferred_element_type=F32) for t in _split3(a))


def _tri_masks(n):
    r = lax.broadcasted_iota(jnp.int32, (n, n), 0)
    c = lax.broadcasted_iota(jnp.int32, (n, n), 1)
    return r >= c, r <= c


def _mod_kernel(c_ref, w_ref, b_ref, o_ref):
    s = _silu(c_ref[...])
    o_ref[...] = jnp.dot(s, w_ref[...], precision=lax.Precision.HIGHEST,
                         preferred_element_type=F32) + b_ref[...]


def _modulation(cc, w_mod, b_mod):
    rows, d = cc.shape
    n = w_mod.shape[1]
    tn = 512
    return pl.pallas_call(
        _mod_kernel,
        out_shape=jax.ShapeDtypeStruct((rows, n), F32),
        grid=(n // tn,),
        in_specs=[pl.BlockSpec((rows, d), lambda j: (0, 0)),
                  pl.BlockSpec((d, tn), lambda j: (0, j)),
                  pl.BlockSpec((1, tn), lambda j: (0, j))],
        out_specs=pl.BlockSpec((rows, tn), lambda j: (0, j)),
        compiler_params=_params(1),
        name="mod",
    )(cc, w_mod, b_mod.reshape(1, n))


def _wcast_kernel(wt_ref, o_ref, *, valid):
    w = wt_ref[...]
    if valid < w.shape[0]:
        w = jnp.where(lax.broadcasted_iota(jnp.int32, w.shape, 0) < valid, w, 0.0)
    o_ref[...] = w.T.astype(o_ref.dtype)


def _wcast(wt, col0, n_out, tn):
    n, d = wt.shape
    assert col0 % tn == 0 and n_out % tn == 0
    return pl.pallas_call(
        functools.partial(_wcast_kernel, valid=min(tn, n - col0)),
        out_shape=jax.ShapeDtypeStruct((d, n_out), BF16),
        grid=(n_out // tn,),
        in_specs=[pl.BlockSpec((tn, d), lambda j: (col0 // tn + j, 0))],
        out_specs=pl.BlockSpec((d, tn), lambda j: (0, j)),
        compiler_params=_params(1),
        name="wcast",
    )(wt)


def _inproj_kernel(x_ref, sh_ref, sc_ref, w_ref, wg_ref, u_ref, g_ref, h_ref, *, sub):
    @pl.when(pl.program_id(1) == 0)
    def _():
        shift = sh_ref[...]
        scale1 = 1.0 + sc_ref[...]

        def body(r, carry):
            rows = pl.ds(pl.multiple_of(r * sub, sub), sub)
            xv = x_ref[rows, :]
            mu = jnp.mean(xv, axis=-1, keepdims=True)
            xc = xv - mu
            var = jnp.mean(xc * xc, axis=-1, keepdims=True)
            h_ref[rows, :] = (xc * lax.rsqrt(var + LN_EPS) * scale1 + shift).astype(BF16)
            return carry

        lax.fori_loop(0, x_ref.shape[0] // sub, body, 0)
        g_ref[...] = jnp.dot(h_ref[...], wg_ref[...], preferred_element_type=F32)

    u_ref[...] = jnp.dot(h_ref[...], w_ref[...], preferred_element_type=F32).astype(BF16)


def _inproj(x2, mod3, mod_row, w_main, w_g, tm):
    rows, d = x2.shape
    nu = w_main.shape[1]
    tn = 1024 if nu % 1024 == 0 else 512
    assert nu % tn == 0 and rows % tm == 0
    sub = min(256, tm)
    return pl.pallas_call(
        functools.partial(_inproj_kernel, sub=sub),
        out_shape=(jax.ShapeDtypeStruct((rows, nu), BF16),
                   jax.ShapeDtypeStruct((rows, LANES), F32)),
        grid=(rows // tm, nu // tn),
        in_specs=[pl.BlockSpec((tm, d), lambda i, j: (i, 0)),
                  pl.BlockSpec((None, 1, d), lambda i, j: (mod_row(i), 0, 0)),
                  pl.BlockSpec((None, 1, d), lambda i, j: (mod_row(i), 0, 1)),
                  pl.BlockSpec((d, tn), lambda i, j: (0, j)),
                  pl.BlockSpec((d, LANES), lambda i, j: (0, 0))],
        out_specs=(pl.BlockSpec((tm, tn), lambda i, j: (i, j)),
                   pl.BlockSpec((tm, LANES), lambda i, j: (i, 0))),
        scratch_shapes=[pltpu.VMEM((tm, d), BF16)],
        compiler_params=_params(2),
        name="inproj",
    )(x2, mod3, mod3, w_main, w_g)


def _gateprep_kernel(g_ref, bias_ref, gc_ref, gr_ref, *, nh, c):
    lower, upper = _tri_masks(c)
    tril = jnp.where(lower, 1.0, 0.0).astype(BF16)
    triu = jnp.where(upper, 1.0, 0.0).astype(BF16)
    lane = lax.broadcasted_iota(jnp.int32, (c, LANES), 1)
    fwd_f = (lane >= 2 * nh) & (lane < 3 * nh)
    bwd_f = (lane >= 3 * nh) & (lane < 4 * nh)
    row = lax.broadcasted_iota(jnp.int32, (GATE_ROWS, c), 0)
    pos = lax.broadcasted_iota(jnp.int32, (GATE_ROWS, c), 1)
    fwd_r = (row >= 2 * nh) & (row < 3 * nh)
    bwd_r = (row >= 3 * nh) & (row < 4 * nh)

    for r0 in range(0, g_ref.shape[0], c):
        pre = g_ref[r0:r0 + c, :] + bias_ref[...]
        lf = _log_sigmoid(pre)
        b = jnp.where(fwd_f, _tri_left(tril, lf), jnp.where(bwd_f, _tri_left(triu, lf), 0.0))
        gc_ref[r0:r0 + c, :] = pre - pltpu.roll(b, LANES - 2 * nh, axis=1)

        pre_t = pre.T[:GATE_ROWS]
        lf_t = _log_sigmoid(pre_t)
        b_t = jnp.where(fwd_r, _tri_right(lf_t, triu), jnp.where(bwd_r, _tri_right(lf_t, tril), 0.0))
        a_t = pre_t - pltpu.roll(b_t, GATE_ROWS - 2 * nh, axis=0)
        pmax, smax = a_t, a_t
        shift = 1
        while shift < c:
            pmax = jnp.maximum(pmax, jnp.where(pos >= shift, pltpu.roll(pmax, shift, axis=1), -jnp.inf))
            smax = jnp.maximum(smax, jnp.where(pos < c - shift, pltpu.roll(smax, c - shift, axis=1), -jnp.inf))
            shift *= 2
        amax_t = pltpu.roll(jnp.where(row < nh, pmax, smax), 4 * nh, axis=0)
        gr_ref[:, r0:r0 + c] = jnp.where(row < 2 * nh, a_t, jnp.where(row < 4 * nh, b_t, amax_t))


def _gateprep(g, bias, nh, chunk):
    rows = g.shape[0]
    assert 6 * nh <= GATE_ROWS
    blk = 4 * chunk if rows % (4 * chunk) == 0 else chunk
    bias_row = jnp.zeros((1, LANES), F32).at[0, :4 * nh].set(bias.reshape(-1))
    return pl.pallas_call(
        functools.partial(_gateprep_kernel, nh=nh, c=chunk),
        out_shape=(jax.ShapeDtypeStruct((rows, LANES), F32),
                   jax.ShapeDtypeStruct((GATE_ROWS, rows), F32)),
        grid=(rows // blk,),
        in_specs=[pl.BlockSpec((blk, LANES), lambda i: (i, 0)),
                  pl.BlockSpec((1, LANES), lambda i: (0, 0))],
        out_specs=(pl.BlockSpec((blk, LANES), lambda i: (i, 0)),
                   pl.BlockSpec((GATE_ROWS, blk), lambda i: (0, i))),
        compiler_params=_params(1),
        name="gateprep",
    )(g, bias_row)


def _conv_kernel(x_ref, w_ref, b_ref, o_ref, *, rows, width, taps):
    cb = x_ref.shape[-1]
    j = pl.program_id(1)
    scale = jnp.where(j >= pl.num_programs(1) // 2, ML_DH ** -0.5, 1.0).astype(F32)
    w = w_ref[...]
    bias = b_ref[...]
    col = lax.broadcasted_iota(jnp.int32, (width, cb), 0)
    first = col == 0
    last = col == width - 1

    def one_row(r):
        left = mid = right = None
        for ki in taps:
            rr = r + (ki - 1)
            valid = jnp.where((rr >= 0) & (rr < rows), 1.0, 0.0).astype(F32)
            wk = w[3 * ki:3 * ki + 3, :] * valid
            start = pl.multiple_of(jnp.clip(rr, 0, rows - 1) * width, width)
            xv = x_ref[pl.ds(start, width), :].astype(F32)
            terms = [xv * wk[kj:kj + 1, :] for kj in range(3)]
            left, mid, right = terms if left is None else (left + terms[0], mid + terms[1], right + terms[2])
        out = (mid + jnp.where(first, 0.0, pltpu.roll(left, 1, axis=0))
               + jnp.where(last, 0.0, pltpu.roll(right, width - 1, axis=0)) + bias)
        dst = pl.multiple_of(r * width, width)
        o_ref[pl.ds(dst, width), :] = (_silu(out) * scale).astype(o_ref.dtype)

    def body(r, carry):
        one_row(r)
        return carry

    lax.fori_loop(0, rows, body, 0)


def _short_conv(u, col0, n_ch, conv_w, conv_b, grid):
    bsz, t, _ = u.shape
    cb = COL_BLOCK
    rows, width, taps = (t // GRID_W, GRID_W, (0, 1, 2)) if grid else (1, t, (1,))
    return pl.pallas_call(
        functools.partial(_conv_kernel, rows=rows, width=width, taps=taps),
        out_shape=jax.ShapeDtypeStruct((bsz, t, n_ch), BF16),
        grid=(bsz, n_ch // cb),
        in_specs=[pl.BlockSpec((None, t, cb), lambda b, j: (b, 0, col0 // cb + j)),
                  pl.BlockSpec((CONV_K * CONV_K, cb), lambda b, j: (0, j)),
                  pl.BlockSpec((1, cb), lambda b, j: (0, j))],
        out_specs=pl.BlockSpec((None, t, cb), lambda b, j: (b, 0, j)),
        compiler_params=_params(2),
        name="conv_grid" if grid else "conv_seq",
    )(u, conv_w.reshape(CONV_K * CONV_K, n_ch).astype(F32), conv_b.reshape(1, n_ch).astype(F32))


def _hg_prepare(qraw, z, lb, fwd):
    lower, upper = _tri_masks(z.shape[0])
    tri = jnp.where(lower if fwd else upper, 1.0, 0.0).astype(BF16)
    f = lb + (1.0 - lb) * jax.nn.sigmoid(z)
    b2 = _tri_left(tri, jnp.log(f) * LOG2E)
    c2 = b2 - jnp.log(1.0 - f) * LOG2E
    return _silu(qraw), b2, c2


def _block_diag(a):
    r, w = a.shape
    zero = jnp.zeros((r, w // 2), a.dtype)
    return jnp.concatenate([jnp.concatenate([a[:, :w // 2], zero], axis=1),
                            jnp.concatenate([zero, a[:, w // 2:]], axis=1)], axis=0)


def _hg_scores(q_ref, b_ref, c_ref, fwd):
    c, w = q_ref.shape
    sub = HG_SUB
    nb = c // sub
    assert 2 * c == LANES and c & (c - 1) == 0
    row = lax.broadcasted_iota(jnp.int32, (sub, LANES), 0)
    col = lax.broadcasted_iota(jnp.int32, (sub, LANES), 1) & (c - 1)
    src = col & (sub - 1)
    code = jnp.where((src <= row) if fwd else (src >= row), col, -1)
    kr = lax.broadcasted_iota(jnp.int32, (w, LANES), 0)
    kl = lax.broadcasted_iota(jnp.int32, (w, LANES), 1)
    sel = jnp.where((kr >= w // 2) == (kl >= c), 1.0, 0.0).astype(BF16)

    units = []
    for j in range(nb):
        r0 = j * sub
        qb = q_ref[r0:r0 + sub, :]
        bb = b_ref[r0:r0 + sub, :]
        cb = c_ref[r0:r0 + sub, :]
        for s in range(sub):
            units.append(qb * jnp.exp2(bb - cb[s:s + 1, :]))
    red = jnp.dot(jnp.concatenate(units, axis=0).astype(BF16), sel, preferred_element_type=F32)

    blocks = []
    for j in range(nb):
        r0 = j * sub
        blk = jnp.zeros((sub, LANES), F32)
        for s in range(sub):
            u0 = (j * sub + s) * sub
            blk = jnp.where(code == r0 + s, red[u0:u0 + sub], blk)
        lo, hi = (0, r0) if fwd else (r0 + sub, c)
        if hi > lo:
            beta = b_ref[pl.ds(r0 - 1 if fwd else r0 + sub, 1), :]
            pieces = [jnp.zeros((lo, w), F32)] if lo else []
            pieces.append(jnp.exp2(beta - c_ref[lo:hi, :]))
            if hi < c:
                pieces.append(jnp.zeros((c - hi, w), F32))
            kh = jnp.concatenate(pieces, axis=0).astype(BF16)
            qh = q_ref[r0:r0 + sub, :] * jnp.exp2(b_ref[r0:r0 + sub, :] - beta)
            off = lax.dot_general(_block_diag(qh).astype(BF16), kh, _NT, preferred_element_type=F32)
            blk = blk + jnp.concatenate([off[:sub], off[sub:]], axis=1)
        blocks.append(blk)
    return jnp.concatenate(blocks, axis=0).astype(BF16)


def _hg_scores_factored(q_ref, b_ref, c_ref, fwd):
    c, w = q_ref.shape
    sub = HG_SUB
    nb = c // sub
    row = lax.broadcasted_iota(jnp.int32, (sub, LANES), 0)
    col = lax.broadcasted_iota(jnp.int32, (sub, LANES), 1) & (c - 1)
    blocks = []
    for j in range(nb):
        r0 = j * sub
        lo, hi = (0, r0 + sub) if fwd else (r0, c)
        edge = r0 - 1 if fwd else r0 + sub
        beta = b_ref[pl.ds(edge, 1), :] if 0 <= edge < c else jnp.zeros((1, w), F32)
        pieces = [jnp.zeros((lo, w), F32)] if lo else []
        pieces.append(jnp.exp2(beta - c_ref[lo:hi, :]))
        if hi < c:
            pieces.append(jnp.zeros((c - hi, w), F32))
        kh = jnp.concatenate(pieces, axis=0).astype(BF16)
        qh = q_ref[r0:r0 + sub, :] * jnp.exp2(b_ref[r0:r0 + sub, :] - beta)
        off = lax.dot_general(_block_diag(qh).astype(BF16), kh, _NT, preferred_element_type=F32)
        blk = jnp.concatenate([off[:sub], off[sub:]], axis=1)
        causal = (col <= row + r0) if fwd else (col >= row + r0)
        blocks.append(jnp.where(causal, blk, 0.0))
    return jnp.concatenate(blocks, axis=0).astype(BF16)


def _hg_state_update(st_refs, v, k_dec, dec, qe=None):
    hw = v.shape[1] // 2
    outs = []
    for h, st_ref in enumerate(st_refs):
        ls = slice(h * hw, (h + 1) * hw)
        st = st_ref[...]
        st_ref[...] = st * dec[:, ls] + lax.dot_general(v[:, ls], k_dec[:, ls], _TN, preferred_element_type=F32)
        if qe is not None:
            outs.append(lax.dot_general(qe[:, ls], st.astype(BF16), _NT, preferred_element_type=F32))
    return outs


def _hgrn2_kernel(cq, cff, cfb, ci, aq, aff, afb, ai, az, lb_ref, nw_ref, y_ref,
                  acc_f, acc_b, st_ref, q_sc, b_sc, c_sc, a_sc, qe_sc, kd_sc, dec_sc, *, slots):
    c = HG_CHUNK
    ncc = cq.shape[0] // c
    nc = aq.shape[0] // c
    hw = HG_DK
    assert nc % 4 == 0
    st_ref[...] = jnp.zeros_like(st_ref)
    accs = (acc_f, acc_b)
    lbs = []
    for d in range(2):
        hg = lb_ref[d * slots:(d + 1) * slots, :]
        e = jnp.exp(hg - jnp.max(hg, axis=0, keepdims=True))
        lbs.append(e[0:1, :] / jnp.sum(e, axis=0, keepdims=True))

    def chunk_rows(d, k, n):
        return pl.ds(pl.multiple_of((k if d == 0 else n - 1 - k) * c, c), c)

    def states(d):
        return st_ref.at[2 * d], st_ref.at[2 * d + 1]

    def ctx_body(k, carry):
        for d in range(2):
            rows = chunk_rows(d, k, ncc)
            _, b2, c2 = _hg_prepare(cq[rows, :].astype(F32), (cff, cfb)[d][rows, :].astype(F32), lbs[d], d == 0)
            end = c - 1 if d == 0 else 0
            b_end = b2[end:end + 1, :]
            _hg_state_update(states(d), ci[rows, :], jnp.exp2(b_end - c2).astype(BF16), jnp.exp2(b_end))
        return carry

    def gates(d, k, slot):
        rows = chunk_rows(d, k, nc)
        q, b2, c2 = _hg_prepare(aq[rows, :].astype(F32), (aff, afb)[d][rows, :].astype(F32), lbs[d], d == 0)
        q_sc[slot, d] = q
        b_sc[slot, d] = b2
        c_sc[slot, d] = c2

    def scores(d, slot, exact):
        qr, br, cr = q_sc.at[slot, d], b_sc.at[slot, d], c_sc.at[slot, d]
        end = c - 1 if d == 0 else 0
        b_end = br[end:end + 1, :]
        qe_sc[slot, d] = (qr[...] * jnp.exp2(br[...])).astype(BF16)
        kd_sc[slot, d] = jnp.exp2(b_end - cr[...]).astype(BF16)
        dec_sc[slot, d] = jnp.exp2(b_end)
        a_sc[slot, d] = (_hg_scores if exact else _hg_scores_factored)(qr, br, cr, d == 0)

    def outputs(d, k, slot):
        rows = chunk_rows(d, k, nc)
        v = ai[rows, :]
        outs = _hg_state_update(states(d), v, kd_sc[slot, d], dec_sc[slot, d], qe_sc[slot, d])
        accs[d][rows, :] = (jnp.concatenate(outs, axis=1)
                            + jnp.dot(a_sc[slot, d], _block_diag(v), preferred_element_type=F32))

    def step(k, slot, exact, do_out=True, do_gates=True, do_scores=True):
        for d in range(2):
            if do_out:
                outputs(d, k - 2, slot)
            if do_gates:
                gates(d, k, slot)
        for d in range(2):
            if do_scores:
                scores(d, 1 - slot, exact)

    def scan(exact):
        def quad_body(kq, carry):
            for i in range(4):
                step(4 * kq + i, i % 2, exact)
            return carry

        step(0, 0, exact, do_out=False, do_scores=False)
        step(1, 1, exact, do_out=False)
        step(2, 0, exact)
        step(3, 1, exact)
        lax.fori_loop(1, nc // 4, quad_body, 0)
        step(nc, 0, exact, do_gates=False)
        step(nc + 1, 1, exact, do_gates=False, do_scores=False)

    def final_body(i, carry):
        rows = pl.ds(pl.multiple_of(i * c, c), c)
        tot = acc_f[rows, :] + acc_b[rows, :]
        gate = _silu(az[rows, :].astype(F32)) * nw_ref[...]
        for h in range(2):
            ls = slice(h * hw, (h + 1) * hw)
            th = tot[:, ls]
            ms = jnp.mean(th * th, axis=-1, keepdims=True)
            y_ref[rows, ls] = (th * lax.rsqrt(ms + NORM_EPS) * gate[:, ls]).astype(y_ref.dtype)
        return carry

    lax.fori_loop(0, ncc, ctx_body, 0, unroll=2)
    depth = HG_SUB * jnp.max(-jnp.log(jnp.minimum(lbs[0], lbs[1])) * LOG2E)
    pl.when(depth <= HG_SAFE_LOG2)(lambda: scan(False))
    pl.when(jnp.logical_not(depth <= HG_SAFE_LOG2))(lambda: scan(True))
    lax.fori_loop(0, nc, final_body, 0, unroll=4)


def _hgrn2(u_x, u_c, w_a, hg_lb, hg_norm):
    bsz, t, _ = u_x.shape
    tc = u_c.shape[1]
    cb = 2 * HG_DK
    c = HG_CHUNK
    assert cb == COL_BLOCK
    seg = w_a // cb
    slots = hg_lb.shape[1]
    lb2 = hg_lb.reshape(2 * slots, w_a).astype(F32)

    def col(k):
        return lambda b, p: (b, 0, k * seg + p)

    ctx_specs = [pl.BlockSpec((None, tc, cb), col(k)) for k in (0, 1, 2, 3)]
    lat_specs = [pl.BlockSpec((None, t, cb), col(k)) for k in (0, 1, 2, 3, 4)]
    return pl.pallas_call(
        functools.partial(_hgrn2_kernel, slots=slots),
        out_shape=jax.ShapeDtypeStruct((bsz, t, w_a), BF16),
        grid=(bsz, seg),
        in_specs=ctx_specs + lat_specs + [
            pl.BlockSpec((2 * slots, cb), lambda b, p: (0, p)),
            pl.BlockSpec((1, cb), lambda b, p: (0, p))],
        out_specs=pl.BlockSpec((None, t, cb), lambda b, p: (b, 0, p)),
        scratch_shapes=[pltpu.VMEM((t, cb), F32),
                        pltpu.VMEM((t, cb), F32),
                        pltpu.VMEM((4, HG_DK, HG_DK), F32),
                        pltpu.VMEM((2, 2, c, cb), F32),
                        pltpu.VMEM((2, 2, c, cb), F32),
                        pltpu.VMEM((2, 2, c, cb), F32),
                        pltpu.VMEM((2, 2, c, 2 * c), BF16),
                        pltpu.VMEM((2, 2, c, cb), BF16),
                        pltpu.VMEM((2, 2, c, cb), BF16),
                        pltpu.VMEM((2, 2, 1, cb), F32)],
        compiler_params=_params(2),
        name="hgrn2",
    )(u_c, u_c, u_c, u_c, u_x, u_x, u_x, u_x, u_x, lb2, hg_norm.reshape(1, w_a).astype(F32))


def _ml_gates(a_row, b_row, amax_row, m_prev, fwd):
    end = a_row.shape[1] - 1 if fwd else 0
    b_end = b_row[:, end:end + 1]
    big_m = jnp.maximum(m_prev, amax_row)
    m_row = b_row + big_m
    m_new = m_row[:, end:end + 1]
    w_s = jnp.exp(b_end + a_row - m_new)
    decay = jnp.exp(b_end + m_prev - m_new)
    return big_m, m_row, m_new, w_s, decay


def _ml_state_lhs(v, w_s):
    vt = v.astype(F32).T
    lhs = jnp.concatenate([vt * w_s, jnp.broadcast_to(w_s, (ML_PAD, vt.shape[1]))], axis=0)
    return vt, lhs.astype(BF16)


def _mlstm_kernel(cq, ck, cv, cgc, cgr, xq, xk, xv, xo, xz, xgc, xgr, nw_ref, y_ref,
                  acc_f, acc_b, st_ref, wt_sc, vt_sc, lhs_sc, row_sc):
    c = ML_CHUNK
    ncc = cq.shape[0] // c
    nc = xq.shape[0] // c
    dh = xq.shape[1]
    assert nc % 2 == 0
    head = pl.program_id(1)
    nh = pl.num_programs(1)
    accs = (acc_f, acc_b)
    st_ref[...] = jnp.zeros_like(st_ref)
    lane = lax.broadcasted_iota(jnp.int32, (c, LANES), 1)
    lower, upper = _tri_masks(c)

    def chunk_rows(d, k, n):
        return pl.ds(pl.multiple_of((k if d == 0 else n - 1 - k) * c, c), c)

    def gate_rows(grr, d, rows):
        slot = d * nh + head
        return (grr[pl.ds(slot, 1), rows], grr[pl.ds(2 * nh + slot, 1), rows],
                grr[pl.ds(4 * nh + slot, 1), rows])

    def ctx_body(k, ms):
        out = []
        for d in range(2):
            rows = chunk_rows(d, k, ncc)
            _, _, m_new, w_s, decay = _ml_gates(*gate_rows(cgr, d, rows), ms[d], d == 0)
            _, lhs = _ml_state_lhs(cv[rows, :], w_s)
            st_ref[d] = decay * st_ref[d] + jnp.dot(lhs, ck[rows, :], preferred_element_type=F32)
            out.append(m_new)
        return tuple(out)

    def score_stage(d, k, slot, m_prev):
        rows = chunk_rows(d, k, nc)
        big_m, m_row, m_new, w_s, decay = _ml_gates(*gate_rows(xgr, d, rows), m_prev, d == 0)
        vt, lhs = _ml_state_lhs(xv[rows, :], w_s)
        lhs_sc[slot, d] = lhs
        vt_sc[slot, d] = vt.astype(BF16)
        a_col = jnp.sum(jnp.where(lane == d * nh + head, xgc[rows, :], 0.0), axis=1, keepdims=True)
        mask = upper if d == 0 else lower
        w_t = (jnp.exp(jnp.where(mask, a_col - big_m, -jnp.inf))
               * lax.dot_general(xk[rows, :], xq[rows, :], _NT, preferred_element_type=F32))
        wt_sc[slot, d] = w_t
        row_sc[slot, d, 0:1, :] = jnp.sum(w_t, axis=0, keepdims=True)
        row_sc[slot, d, 1:2, :] = jnp.exp(m_prev - big_m)
        row_sc[slot, d, 2:3, :] = jnp.exp(-m_row)
        row_sc[slot, d, 3:4, :] = jnp.broadcast_to(decay, (1, c))
        return m_new

    def output_stage(d, k, slot):
        rows = chunk_rows(d, k, nc)
        st = st_ref[d]
        r = lax.dot_general(st.astype(BF16), xq[rows, :], _NT, preferred_element_type=F32)
        w_inter = row_sc[slot, d, 1:2, :]
        den = w_inter * r[dh:dh + 1, :] + row_sc[slot, d, 0:1, :]
        inv = 1.0 / jnp.maximum(jnp.abs(den), row_sc[slot, d, 2:3, :])
        accs[d][:, rows] = (r[:dh, :] * (w_inter * inv)
                            + jnp.dot(vt_sc[slot, d], (wt_sc[slot, d] * inv).astype(BF16),
                                      preferred_element_type=F32))
        st_ref[d] = (row_sc[slot, d, 3:4, 0:1] * st
                     + jnp.dot(lhs_sc[slot, d], xk[rows, :], preferred_element_type=F32))

    def pair_body(kk, ms):
        for slot in range(2):
            k = 2 * kk + slot
            nxt = jnp.minimum(k + 1, nc - 1)
            for d in range(2):
                output_stage(d, k, slot)
            ms = tuple(score_stage(d, nxt, 1 - slot, ms[d]) for d in range(2))
        return ms

    def final_body(i, carry):
        rows = pl.ds(pl.multiple_of(i * c, c), c)
        tot = acc_f[:, rows] + acc_b[:, rows]
        mu = jnp.mean(tot, axis=0, keepdims=True)
        tc_ = tot - mu
        var = jnp.mean(tc_ * tc_, axis=0, keepdims=True)
        normed = (tc_ * lax.rsqrt(var + NORM_EPS)).T * nw_ref[...]
        gate = jax.nn.sigmoid(xo[rows, :].astype(F32)) * _silu(xz[rows, :].astype(F32))
        y_ref[rows, :] = (normed * gate).astype(y_ref.dtype)
        return carry

    ms = (jnp.zeros((1, 1), F32), jnp.zeros((1, 1), F32))
    ms = lax.fori_loop(0, ncc, ctx_body, ms)
    ms = tuple(score_stage(d, 0, 0, ms[d]) for d in range(2))
    lax.fori_loop(0, nc // 2, pair_body, ms)
    lax.fori_loop(0, nc, final_body, 0)


def _mlstm(qk_x, qk_c, u_x, u_c, v_col, o_col, z_col, gates_x, gates_c, ml_norm, w_b):
    bsz, t, _ = u_x.shape
    tc = u_c.shape[1]
    nh = w_b // ML_DH
    cb = ML_DH
    assert cb == COL_BLOCK and t % (2 * ML_CHUNK) == 0 and tc % ML_CHUNK == 0

    def col(k):
        return lambda b, h: (b, 0, k + h)

    def seq_specs(n, with_gates):
        specs = [pl.BlockSpec((None, n, cb), col(0)),
                 pl.BlockSpec((None, n, cb), col(nh)),
                 pl.BlockSpec((None, n, cb), col(v_col // cb))]
        if with_gates:
            specs += [pl.BlockSpec((None, n, cb), col(o_col // cb)),
                      pl.BlockSpec((None, n, cb), col(z_col // cb))]
        specs += [pl.BlockSpec((n, LANES), lambda b, h: (b, 0)),
                  pl.BlockSpec((GATE_ROWS, n), lambda b, h: (0, b))]
        return specs

    return pl.pallas_call(
        _mlstm_kernel,
        out_shape=jax.ShapeDtypeStruct((bsz, t, w_b), BF16),
        grid=(bsz, nh),
        in_specs=seq_specs(tc, False) + seq_specs(t, True) + [pl.BlockSpec((1, cb), lambda b, h: (0, h))],
        out_specs=pl.BlockSpec((None, t, cb), lambda b, h: (b, 0, h)),
        scratch_shapes=[pltpu.VMEM((cb, t), F32),
                        pltpu.VMEM((cb, t), F32),
                        pltpu.VMEM((2, ML_DH + ML_PAD, ML_DH), F32),
                        pltpu.VMEM((2, 2, ML_CHUNK, ML_CHUNK), F32),
                        pltpu.VMEM((2, 2, ML_DH, ML_CHUNK), BF16),
                        pltpu.VMEM((2, 2, ML_DH + ML_PAD, ML_CHUNK), BF16),
                        pltpu.VMEM((2, 2, 8, ML_CHUNK), F32)],
        compiler_params=_params(2),
        name="mlstm",
    )(qk_c, qk_c, u_c, gates_c[0], gates_c[1],
      qk_x, qk_x, u_x, u_x, u_x, gates_x[0], gates_x[1],
      ml_norm.reshape(1, w_b).astype(F32))


def _outproj_kernel(ya_ref, yb_ref, x_ref, gate_ref, wa_ref, wb_ref, g_ref, b_ref, o_ref, *, sub):
    for r0 in range(0, x_ref.shape[0], sub):
        rows = slice(r0, r0 + sub)
        y = (jnp.dot(ya_ref[rows, :], wa_ref[...], preferred_element_type=F32)
             + jnp.dot(yb_ref[rows, :], wb_ref[...], preferred_element_type=F32))
        r = ALPHA * x_ref[rows, :] + gate_ref[...] * y
        mu = jnp.mean(r, axis=-1, keepdims=True)
        rc = r - mu
        var = jnp.mean(rc * rc, axis=-1, keepdims=True)
        o_ref[rows, :] = rc * lax.rsqrt(var + LN_EPS) * g_ref[...] + b_ref[...]


def _outproj(y_a, y_b, x2, mod3, t, w_oa, w_ob, ln_g, ln_b):
    rows, d = x2.shape
    tm = 512
    wa, wb = y_a.shape[1], y_b.shape[1]
    return pl.pallas_call(
        functools.partial(_outproj_kernel, sub=256),
        out_shape=jax.ShapeDtypeStruct((rows, d), F32),
        grid=(rows // tm,),
        in_specs=[pl.BlockSpec((tm, wa), lambda i: (i, 0)),
                  pl.BlockSpec((tm, wb), lambda i: (i, 0)),
                  pl.BlockSpec((tm, d), lambda i: (i, 0)),
                  pl.BlockSpec((None, 1, d), lambda i: ((i * tm) // t, 0, 2)),
                  pl.BlockSpec((wa, d), lambda i: (0, 0)),
                  pl.BlockSpec((wb, d), lambda i: (0, 0)),
                  pl.BlockSpec((1, d), lambda i: (0, 0)),
                  pl.BlockSpec((1, d), lambda i: (0, 0))],
        out_specs=pl.BlockSpec((tm, d), lambda i: (i, 0)),
        compiler_params=_params(1),
        name="outproj",
    )(y_a, y_b, x2, mod3, w_oa, w_ob, ln_g.reshape(1, d).astype(F32), ln_b.reshape(1, d).astype(F32))


def kernel(x, c, ctx, c_ctx, w_mod, b_mod, w_in, conv_w, conv_b, hg_lb, ml_gate_b, hg_norm_w,
           ml_norm_w, w_out, ln_g, ln_b):
    bsz, t, d = x.shape
    tc = ctx.shape[1]
    assert w_in.shape[0] == DEPTH and hg_lb.shape[1] == DEPTH + 1
    d_inner = w_out.shape[1]
    w_a = hg_lb.shape[2]
    w_b = d_inner - w_a
    nh = ml_gate_b.shape[2]
    nu = 5 * w_a + 5 * w_b
    assert nh * ML_DH == w_b and w_in.shape[2] == nu + 4 * nh and t % GRID_W == 0

    mod_rows = 8 * ((bsz + 1 + 7) // 8)
    cc = jnp.zeros((mod_rows, d), F32).at[:bsz].set(c).at[bsz].set(c_ctx)
    mod3 = _modulation(cc, w_mod[0], b_mod[0]).reshape(mod_rows, 1, 3 * d)

    w_in_t = w_in[0].T
    w_main = _wcast(w_in_t, 0, nu, 512)
    w_g = _wcast(w_in_t, nu, LANES, LANES)
    tm = min(1024, t)
    u_x, g_x = _inproj(x.reshape(bsz * t, d), mod3, lambda i: (i * tm) // t, w_main, w_g, tm)
    u_c, g_c = _inproj(ctx.reshape(bsz * tc, d), mod3, lambda i: bsz, w_main, w_g, min(1024, bsz * tc))
    u_x = u_x.reshape(bsz, t, nu)
    u_c = u_c.reshape(bsz, tc, nu)

    gates_x = _gateprep(g_x, ml_gate_b[0], nh, ML_CHUNK)
    gates_c = _gateprep(g_c, ml_gate_b[0], nh, ML_CHUNK)

    qk_col = 5 * w_a
    qk_x = _short_conv(u_x, qk_col, 2 * w_b, conv_w[0], conv_b[0], grid=True)
    qk_c = _short_conv(u_c, qk_col, 2 * w_b, conv_w[0], conv_b[0], grid=False)

    y_a = _hgrn2(u_x, u_c, w_a, hg_lb, hg_norm_w[0])
    y_b = _mlstm(qk_x, qk_c, u_x, u_c, qk_col + 2 * w_b, qk_col + 3 * w_b, qk_col + 4 * w_b,
                 gates_x, gates_c, ml_norm_w[0], w_b)

    w_o = w_out[0].astype(BF16)
    out = _outproj(y_a.reshape(bsz * t, w_a), y_b.reshape(bsz * t, w_b), x.reshape(bsz * t, d), mod3, t,
                   w_o[:w_a], w_o[w_a:], ln_g[0], ln_b[0])
    return out.reshape(bsz, t, d)
```

```python
import functools

import jax
import jax.numpy as jnp
from jax import lax
from jax.experimental import pallas as pl
from jax.experimental.pallas import tpu as pltpu

F32 = jnp.float32
BF16 = jnp.bfloat16

LN_EPS = 1e-5
NORM_EPS = 1e-6
DEPTH = 1
ALPHA = (2 * DEPTH) ** 0.25

HG_DK = 128
ML_DH = 256
GRID_W = 64
CONV_K = 3

LANES = 128
HG_CHUNK = 64
HG_SUB = 8
HG_SAFE_LOG2 = 64.0
LOG2E = 1.4426950408889634
ML_CHUNK = 256
ML_PAD = 16
GATE_ROWS = 24
COL_BLOCK = 256
VMEM_LIMIT = 56 * 1024 * 1024

_NT = (((1,), (1,)), ((), ()))
_TN = (((0,), (0,)), ((), ()))


def _params(n_grid):
    return pltpu.CompilerParams(dimension_semantics=("arbitrary",) * n_grid,
                                vmem_limit_bytes=VMEM_LIMIT)


def _silu(a):
    return a * jax.nn.sigmoid(a)


def _log_sigmoid(a):
    return jnp.minimum(a, 0.0) - jnp.log1p(jnp.exp(-jnp.abs(a)))


def _split3(a):
    hi = a.astype(BF16)
    r1 = a - hi.astype(F32)
    mid = r1.astype(BF16)
    lo = (r1 - mid.astype(F32)).astype(BF16)
    return hi, mid, lo


def _tri_left(tri, a):
    return sum(jnp.dot(tri, t, preferred_element_type=F32) for t in _split3(a))


def _tri_right(a, tri):
    return sum(jnp.dot(t, tri, preferred_element_type=F32) for t in _split3(a))


def _tri_masks(n):
    r = lax.broadcasted_iota(jnp.int32, (n, n), 0)
    c = lax.broadcasted_iota(jnp.int32, (n, n), 1)
    return r >= c, r <= c


def _mod_kernel(c_ref, w_ref, b_ref, o_ref):
    s = _silu(c_ref[...])
    o_ref[...] = jnp.dot(s, w_ref[...], precision=lax.Precision.HIGHEST,
                         preferred_element_type=F32) + b_ref[...]


def _modulation(cc, w_mod, b_mod):
    rows, d = cc.shape
    n = w_mod.shape[1]
    tn = 512
    return pl.pallas_call(
        _mod_kernel,
        out_shape=jax.ShapeDtypeStruct((rows, n), F32),
        grid=(n // tn,),
        in_specs=[pl.BlockSpec((rows, d), lambda j: (0, 0)),
                  pl.BlockSpec((d, tn), lambda j: (0, j)),
                  pl.BlockSpec((1, tn), lambda j: (0, j))],
        out_specs=pl.BlockSpec((rows, tn), lambda j: (0, j)),
        compiler_params=_params(1),
        name="mod",
    )(cc, w_mod, b_mod.reshape(1, n))


def _wcast_kernel(wt_ref, o_ref, *, valid):
    w = wt_ref[...]
    if valid < w.shape[0]:
        w = jnp.where(lax.broadcasted_iota(jnp.int32, w.shape, 0) < valid, w, 0.0)
    o_ref[...] = w.T.astype(o_ref.dtype)


def _wcast(wt, col0, n_out, tn):
    n, d = wt.shape
    assert col0 % tn == 0 and n_out % tn == 0
    return pl.pallas_call(
        functools.partial(_wcast_kernel, valid=min(tn, n - col0)),
        out_shape=jax.ShapeDtypeStruct((d, n_out), BF16),
        grid=(n_out // tn,),
        in_specs=[pl.BlockSpec((tn, d), lambda j: (col0 // tn + j, 0))],
        out_specs=pl.BlockSpec((d, tn), lambda j: (0, j)),
        compiler_params=_params(1),
        name="wcast",
    )(wt)


def _ln_modulate(xv, shift, scale):
    mu = jnp.mean(xv, axis=-1, keepdims=True)
    xc = xv - mu
    var = jnp.mean(xc * xc, axis=-1, keepdims=True)
    return (xc * lax.rsqrt(var + LN_EPS) * (1.0 + scale) + shift).astype(BF16)


def _ln_kernel(x_ref, sh_ref, sc_ref, h_ref):
    h_ref[...] = _ln_modulate(x_ref[...], sh_ref[...], sc_ref[...])


def _inproj_kernel(xn_ref, h0_ref, sh_ref, sc_ref, w_ref, wg_ref, u_ref, g_ref, h_a, h_b, *, nsub):
    i = pl.program_id(0)
    j = pl.program_id(1)
    rs = xn_ref.shape[0]

    @pl.when((i == 0) & (j == 0))
    def _():
        h_a[...] = h0_ref[...]

    def body(h_cur, h_nxt):
        @pl.when(j == 0)
        def _():
            g_ref[...] = jnp.dot(h_cur[...], wg_ref[...], preferred_element_type=F32)

        rows = pl.ds(pl.multiple_of(jnp.minimum(j, nsub - 1) * rs, rs), rs)
        h_nxt[rows, :] = _ln_modulate(xn_ref[...], sh_ref[...], sc_ref[...])
        u_ref[...] = jnp.dot(h_cur[...], w_ref[...], preferred_element_type=F32).astype(BF16)

    pl.when(i % 2 == 0)(lambda: body(h_a, h_b))
    pl.when(i % 2 == 1)(lambda: body(h_b, h_a))


def _inproj(x2, mod3, mod_row, w_main, w_g, tm):
    rows, d = x2.shape
    nu = w_main.shape[1]
    tn = 1024 if nu % 1024 == 0 else 512
    nt, nj = rows // tm, nu // tn
    nsub = min(8, 1 << (nj.bit_length() - 1))
    rs = tm // nsub
    assert nu % tn == 0 and rows % tm == 0 and tm % nsub == 0

    def mod_spec(part, row_of):
        return pl.BlockSpec((None, 1, d), lambda *ij: (row_of(ij[0]), 0, part))

    h0 = pl.pallas_call(
        _ln_kernel,
        out_shape=jax.ShapeDtypeStruct((tm, d), BF16),
        grid=(nsub,),
        in_specs=[pl.BlockSpec((rs, d), lambda r: (r, 0)),
                  mod_spec(0, lambda r: mod_row(0)), mod_spec(1, lambda r: mod_row(0))],
        out_specs=pl.BlockSpec((rs, d), lambda r: (r, 0)),
        compiler_params=_params(1),
        name="ln0",
    )(x2, mod3, mod3)

    def nxt(i):
        return jnp.minimum(i + 1, nt - 1)

    return pl.pallas_call(
        functools.partial(_inproj_kernel, nsub=nsub),
        out_shape=(jax.ShapeDtypeStruct((rows, nu), BF16),
                   jax.ShapeDtypeStruct((rows, LANES), F32)),
        grid=(nt, nj),
        in_specs=[pl.BlockSpec((rs, d), lambda i, j: (nxt(i) * nsub + jnp.minimum(j, nsub - 1), 0)),
                  pl.BlockSpec((tm, d), lambda i, j: (0, 0)),
                  mod_spec(0, lambda i: mod_row(nxt(i))), mod_spec(1, lambda i: mod_row(nxt(i))),
                  pl.BlockSpec((d, tn), lambda i, j: (0, j)),
                  pl.BlockSpec((d, LANES), lambda i, j: (0, 0))],
        out_specs=(pl.BlockSpec((tm, tn), lambda i, j: (i, j)),
                   pl.BlockSpec((tm, LANES), lambda i, j: (i, 0))),
        scratch_shapes=[pltpu.VMEM((tm, d), BF16), pltpu.VMEM((tm, d), BF16)],
        compiler_params=_params(2),
        name="inproj",
    )(x2, h0, mod3, mod3, w_main, w_g)


def _gateprep_kernel(g_ref, bias_ref, gc_ref, gr_ref, *, nh, c):
    lower, upper = _tri_masks(c)
    tril = jnp.where(lower, 1.0, 0.0).astype(BF16)
    triu = jnp.where(upper, 1.0, 0.0).astype(BF16)
    lane = lax.broadcasted_iota(jnp.int32, (c, LANES), 1)
    fwd_f = (lane >= 2 * nh) & (lane < 3 * nh)
    bwd_f = (lane >= 3 * nh) & (lane < 4 * nh)
    row = lax.broadcasted_iota(jnp.int32, (GATE_ROWS, c), 0)
    pos = lax.broadcasted_iota(jnp.int32, (GATE_ROWS, c), 1)
    fwd_r = (row >= 2 * nh) & (row < 3 * nh)
    bwd_r = (row >= 3 * nh) & (row < 4 * nh)

    for r0 in range(0, g_ref.shape[0], c):
        pre = g_ref[r0:r0 + c, :] + bias_ref[...]
        lf = _log_sigmoid(pre)
        b = jnp.where(fwd_f, _tri_left(tril, lf), jnp.where(bwd_f, _tri_left(triu, lf), 0.0))
        gc_ref[r0:r0 + c, :] = pre - pltpu.roll(b, LANES - 2 * nh, axis=1)

        pre_t = pre.T[:GATE_ROWS]
        lf_t = _log_sigmoid(pre_t)
        b_t = jnp.where(fwd_r, _tri_right(lf_t, triu), jnp.where(bwd_r, _tri_right(lf_t, tril), 0.0))
        a_t = pre_t - pltpu.roll(b_t, GATE_ROWS - 2 * nh, axis=0)
        pmax, smax = a_t, a_t
        shift = 1
        while shift < c:
            pmax = jnp.maximum(pmax, jnp.where(pos >= shift, pltpu.roll(pmax, shift, axis=1), -jnp.inf))
            smax = jnp.maximum(smax, jnp.where(pos < c - shift, pltpu.roll(smax, c - shift, axis=1), -jnp.inf))
            shift *= 2
        amax_t = pltpu.roll(jnp.where(row < nh, pmax, smax), 4 * nh, axis=0)
        gr_ref[:, r0:r0 + c] = jnp.where(row < 2 * nh, a_t, jnp.where(row < 4 * nh, b_t, amax_t))


def _gateprep(g, bias, nh, chunk):
    rows = g.shape[0]
    assert 6 * nh <= GATE_ROWS
    blk = 4 * chunk if rows % (4 * chunk) == 0 else chunk
    bias_row = jnp.zeros((1, LANES), F32).at[0, :4 * nh].set(bias.reshape(-1))
    return pl.pallas_call(
        functools.partial(_gateprep_kernel, nh=nh, c=chunk),
        out_shape=(jax.ShapeDtypeStruct((rows, LANES), F32),
                   jax.ShapeDtypeStruct((GATE_ROWS, rows), F32)),
        grid=(rows // blk,),
        in_specs=[pl.BlockSpec((blk, LANES), lambda i: (i, 0)),
                  pl.BlockSpec((1, LANES), lambda i: (0, 0))],
        out_specs=(pl.BlockSpec((blk, LANES), lambda i: (i, 0)),
                   pl.BlockSpec((GATE_ROWS, blk), lambda i: (0, i))),
        compiler_params=_params(1),
        name="gateprep",
    )(g, bias_row)


def _conv_kernel(x_ref, w_ref, b_ref, o_ref, *, rows, width, taps):
    cb = x_ref.shape[-1]
    j = pl.program_id(1)
    scale = jnp.where(j >= pl.num_programs(1) // 2, ML_DH ** -0.5, 1.0).astype(F32)
    w = w_ref[...]
    bias = b_ref[...]
    col = lax.broadcasted_iota(jnp.int32, (width, cb), 0)
    first = col == 0
    last = col == width - 1

    def one_row(r):
        left = mid = right = None
        for ki in taps:
            rr = r + (ki - 1)
            valid = jnp.where((rr >= 0) & (rr < rows), 1.0, 0.0).astype(F32)
            wk = w[3 * ki:3 * ki + 3, :] * valid
            start = pl.multiple_of(jnp.clip(rr, 0, rows - 1) * width, width)
            xv = x_ref[pl.ds(start, width), :].astype(F32)
            terms = [xv * wk[kj:kj + 1, :] for kj in range(3)]
            left, mid, right = terms if left is None else (left + terms[0], mid + terms[1], right + terms[2])
        out = (mid + jnp.where(first, 0.0, pltpu.roll(left, 1, axis=0))
               + jnp.where(last, 0.0, pltpu.roll(right, width - 1, axis=0)) + bias)
        dst = pl.multiple_of(r * width, width)
        o_ref[pl.ds(dst, width), :] = (_silu(out) * scale).astype(o_ref.dtype)

    def body(r, carry):
        one_row(r)
        return carry

    lax.fori_loop(0, rows, body, 0)


def _short_conv(u, col0, n_ch, conv_w, conv_b, grid):
    bsz, t, _ = u.shape
    cb = COL_BLOCK
    rows, width, taps = (t // GRID_W, GRID_W, (0, 1, 2)) if grid else (1, t, (1,))
    return pl.pallas_call(
        functools.partial(_conv_kernel, rows=rows, width=width, taps=taps),
        out_shape=jax.ShapeDtypeStruct((bsz, t, n_ch), BF16),
        grid=(bsz, n_ch // cb),
        in_specs=[pl.BlockSpec((None, t, cb), lambda b, j: (b, 0, col0 // cb + j)),
                  pl.BlockSpec((CONV_K * CONV_K, cb), lambda b, j: (0, j)),
                  pl.BlockSpec((1, cb), lambda b, j: (0, j))],
        out_specs=pl.BlockSpec((None, t, cb), lambda b, j: (b, 0, j)),
        compiler_params=_params(2),
        name="conv_grid" if grid else "conv_seq",
    )(u, conv_w.reshape(CONV_K * CONV_K, n_ch).astype(F32), conv_b.reshape(1, n_ch).astype(F32))


def _hg_prepare(qraw, z, lb, fwd):
    lower, upper = _tri_masks(z.shape[0])
    tri = jnp.where(lower if fwd else upper, 1.0, 0.0).astype(BF16)
    f = lb + (1.0 - lb) * jax.nn.sigmoid(z)
    b2 = _tri_left(tri, jnp.log(f) * LOG2E)
    c2 = b2 - jnp.log(1.0 - f) * LOG2E
    return _silu(qraw), b2, c2


def _block_diag(a):
    r, w = a.shape
    zero = jnp.zeros((r, w // 2), a.dtype)
    return jnp.concatenate([jnp.concatenate([a[:, :w // 2], zero], axis=1),
                            jnp.concatenate([zero, a[:, w // 2:]], axis=1)], axis=0)


def _hg_scores(q_ref, b_ref, c_ref, fwd):
    c, w = q_ref.shape
    sub = HG_SUB
    nb = c // sub
    assert 2 * c == LANES and c & (c - 1) == 0
    row = lax.broadcasted_iota(jnp.int32, (sub, LANES), 0)
    col = lax.broadcasted_iota(jnp.int32, (sub, LANES), 1) & (c - 1)
    src = col & (sub - 1)
    code = jnp.where((src <= row) if fwd else (src >= row), col, -1)
    kr = lax.broadcasted_iota(jnp.int32, (w, LANES), 0)
    kl = lax.broadcasted_iota(jnp.int32, (w, LANES), 1)
    sel = jnp.where((kr >= w // 2) == (kl >= c), 1.0, 0.0).astype(BF16)

    units = []
    for j in range(nb):
        r0 = j * sub
        qb = q_ref[r0:r0 + sub, :]
        bb = b_ref[r0:r0 + sub, :]
        cb = c_ref[r0:r0 + sub, :]
        for s in range(sub):
            units.append(qb * jnp.exp2(bb - cb[s:s + 1, :]))
    red = jnp.dot(jnp.concatenate(units, axis=0).astype(BF16), sel, preferred_element_type=F32)

    blocks = []
    for j in range(nb):
        r0 = j * sub
        blk = jnp.zeros((sub, LANES), F32)
        for s in range(sub):
            u0 = (j * sub + s) * sub
            blk = jnp.where(code == r0 + s, red[u0:u0 + sub], blk)
        lo, hi = (0, r0) if fwd else (r0 + sub, c)
        if hi > lo:
            beta = b_ref[pl.ds(r0 - 1 if fwd else r0 + sub, 1), :]
            pieces = [jnp.zeros((lo, w), F32)] if lo else []
            pieces.append(jnp.exp2(beta - c_ref[lo:hi, :]))
            if hi < c:
                pieces.append(jnp.zeros((c - hi, w), F32))
            kh = jnp.concatenate(pieces, axis=0).astype(BF16)
            qh = q_ref[r0:r0 + sub, :] * jnp.exp2(b_ref[r0:r0 + sub, :] - beta)
            off = lax.dot_general(_block_diag(qh).astype(BF16), kh, _NT, preferred_element_type=F32)
            blk = blk + jnp.concatenate([off[:sub], off[sub:]], axis=1)
        blocks.append(blk)
    return jnp.concatenate(blocks, axis=0).astype(BF16)


def _hg_scores_factored(q_ref, b_ref, c_ref, fwd):
    c, w = q_ref.shape
    sub = HG_SUB
    nb = c // sub
    row = lax.broadcasted_iota(jnp.int32, (sub, LANES), 0)
    col = lax.broadcasted_iota(jnp.int32, (sub, LANES), 1) & (c - 1)
    blocks = []
    for j in range(nb):
        r0 = j * sub
        lo, hi = (0, r0 + sub) if fwd else (r0, c)
        edge = r0 - 1 if fwd else r0 + sub
        beta = b_ref[pl.ds(edge, 1), :] if 0 <= edge < c else jnp.zeros((1, w), F32)
        pieces = [jnp.zeros((lo, w), F32)] if lo else []
        pieces.append(jnp.exp2(beta - c_ref[lo:hi, :]))
        if hi < c:
            pieces.append(jnp.zeros((c - hi, w), F32))
        kh = jnp.concatenate(pieces, axis=0).astype(BF16)
        qh = q_ref[r0:r0 + sub, :] * jnp.exp2(b_ref[r0:r0 + sub, :] - beta)
        off = lax.dot_general(_block_diag(qh).astype(BF16), kh, _NT, preferred_element_type=F32)
        blk = jnp.concatenate([off[:sub], off[sub:]], axis=1)
        causal = (col <= row + r0) if fwd else (col >= row + r0)
        blocks.append(jnp.where(causal, blk, 0.0))
    return jnp.concatenate(blocks, axis=0).astype(BF16)


def _hg_state_update(st_refs, v, k_dec, dec, qe=None):
    hw = v.shape[1] // 2
    outs = []
    for h, st_ref in enumerate(st_refs):
        ls = slice(h * hw, (h + 1) * hw)
        st = st_ref[...]
        st_ref[...] = st * dec[:, ls] + lax.dot_general(v[:, ls], k_dec[:, ls], _TN, preferred_element_type=F32)
        if qe is not None:
            outs.append(lax.dot_general(qe[:, ls], st.astype(BF16), _NT, preferred_element_type=F32))
    return outs


def _hgrn2_kernel(cq, cff, cfb, ci, aq, aff, afb, ai, az, lb_ref, nw_ref, y_ref,
                  acc_f, acc_b, st_ref, q_sc, b_sc, c_sc, a_sc, qe_sc, kd_sc, dec_sc, *, slots):
    c = HG_CHUNK
    ncc = cq.shape[0] // c
    nc = aq.shape[0] // c
    hw = HG_DK
    assert nc % 4 == 0
    st_ref[...] = jnp.zeros_like(st_ref)
    accs = (acc_f, acc_b)
    lbs = []
    for d in range(2):
        hg = lb_ref[d * slots:(d + 1) * slots, :]
        e = jnp.exp(hg - jnp.max(hg, axis=0, keepdims=True))
        lbs.append(e[0:1, :] / jnp.sum(e, axis=0, keepdims=True))

    def chunk_rows(d, k, n):
        return pl.ds(pl.multiple_of((k if d == 0 else n - 1 - k) * c, c), c)

    def states(d):
        return st_ref.at[2 * d], st_ref.at[2 * d + 1]

    def ctx_body(k, carry):
        for d in range(2):
            rows = chunk_rows(d, k, ncc)
            _, b2, c2 = _hg_prepare(cq[rows, :].astype(F32), (cff, cfb)[d][rows, :].astype(F32), lbs[d], d == 0)
            end = c - 1 if d == 0 else 0
            b_end = b2[end:end + 1, :]
            _hg_state_update(states(d), ci[rows, :], jnp.exp2(b_end - c2).astype(BF16), jnp.exp2(b_end))
        return carry

    def gates(d, k, slot):
        rows = chunk_rows(d, k, nc)
        q, b2, c2 = _hg_prepare(aq[rows, :].astype(F32), (aff, afb)[d][rows, :].astype(F32), lbs[d], d == 0)
        q_sc[slot, d] = q
        b_sc[slot, d] = b2
        c_sc[slot, d] = c2

    def scores(d, slot, exact):
        qr, br, cr = q_sc.at[slot, d], b_sc.at[slot, d], c_sc.at[slot, d]
        end = c - 1 if d == 0 else 0
        b_end = br[end:end + 1, :]
        qe_sc[slot, d] = (qr[...] * jnp.exp2(br[...])).astype(BF16)
        kd_sc[slot, d] = jnp.exp2(b_end - cr[...]).astype(BF16)
        dec_sc[slot, d] = jnp.exp2(b_end)
        a_sc[slot, d] = (_hg_scores if exact else _hg_scores_factored)(qr, br, cr, d == 0)

    def outputs(d, k, slot):
        rows = chunk_rows(d, k, nc)
        v = ai[rows, :]
        outs = _hg_state_update(states(d), v, kd_sc[slot, d], dec_sc[slot, d], qe_sc[slot, d])
        accs[d][rows, :] = (jnp.concatenate(outs, axis=1)
                            + jnp.dot(a_sc[slot, d], _block_diag(v), preferred_element_type=F32))

    def step(k, slot, exact, do_out=True, do_gates=True, do_scores=True):
        for d in range(2):
            if do_out:
                outputs(d, k - 2, slot)
            if do_gates:
                gates(d, k, slot)
        for d in range(2):
            if do_scores:
                scores(d, 1 - slot, exact)

    def scan(exact):
        def quad_body(kq, carry):
            for i in range(4):
                step(4 * kq + i, i % 2, exact)
            return carry

        step(0, 0, exact, do_out=False, do_scores=False)
        step(1, 1, exact, do_out=False)
        step(2, 0, exact)
        step(3, 1, exact)
        lax.fori_loop(1, nc // 4, quad_body, 0)
        step(nc, 0, exact, do_gates=False)
        step(nc + 1, 1, exact, do_gates=False, do_scores=False)

    def final_body(i, carry):
        rows = pl.ds(pl.multiple_of(i * c, c), c)
        tot = acc_f[rows, :] + acc_b[rows, :]
        gate = _silu(az[rows, :].astype(F32)) * nw_ref[...]
        for h in range(2):
            ls = slice(h * hw, (h + 1) * hw)
            th = tot[:, ls]
            ms = jnp.mean(th * th, axis=-1, keepdims=True)
            y_ref[rows, ls] = (th * lax.rsqrt(ms + NORM_EPS) * gate[:, ls]).astype(y_ref.dtype)
        return carry

    lax.fori_loop(0, ncc, ctx_body, 0, unroll=2)
    depth = HG_SUB * jnp.max(-jnp.log(jnp.minimum(lbs[0], lbs[1])) * LOG2E)
    pl.when(depth <= HG_SAFE_LOG2)(lambda: scan(False))
    pl.when(jnp.logical_not(depth <= HG_SAFE_LOG2))(lambda: scan(True))
    lax.fori_loop(0, nc, final_body, 0, unroll=4)


def _hgrn2(u_x, u_c, w_a, hg_lb, hg_norm):
    bsz, t, _ = u_x.shape
    tc = u_c.shape[1]
    cb = 2 * HG_DK
    c = HG_CHUNK
    assert cb == COL_BLOCK
    seg = w_a // cb
    slots = hg_lb.shape[1]
    lb2 = hg_lb.reshape(2 * slots, w_a).astype(F32)

    def col(k):
        return lambda b, p: (b, 0, k * seg + p)

    ctx_specs = [pl.BlockSpec((None, tc, cb), col(k)) for k in (0, 1, 2, 3)]
    lat_specs = [pl.BlockSpec((None, t, cb), col(k)) for k in (0, 1, 2, 3, 4)]
    return pl.pallas_call(
        functools.partial(_hgrn2_kernel, slots=slots),
        out_shape=jax.ShapeDtypeStruct((bsz, t, w_a), BF16),
        grid=(bsz, seg),
        in_specs=ctx_specs + lat_specs + [
            pl.BlockSpec((2 * slots, cb), lambda b, p: (0, p)),
            pl.BlockSpec((1, cb), lambda b, p: (0, p))],
        out_specs=pl.BlockSpec((None, t, cb), lambda b, p: (b, 0, p)),
        scratch_shapes=[pltpu.VMEM((t, cb), F32),
                        pltpu.VMEM((t, cb), F32),
                        pltpu.VMEM((4, HG_DK, HG_DK), F32),
                        pltpu.VMEM((2, 2, c, cb), F32),
                        pltpu.VMEM((2, 2, c, cb), F32),
                        pltpu.VMEM((2, 2, c, cb), F32),
                        pltpu.VMEM((2, 2, c, 2 * c), BF16),
                        pltpu.VMEM((2, 2, c, cb), BF16),
                        pltpu.VMEM((2, 2, c, cb), BF16),
                        pltpu.VMEM((2, 2, 1, cb), F32)],
        compiler_params=_params(2),
        name="hgrn2",
    )(u_c, u_c, u_c, u_c, u_x, u_x, u_x, u_x, u_x, lb2, hg_norm.reshape(1, w_a).astype(F32))


def _ml_gates(a_row, b_row, amax_row, m_prev, fwd):
    end = a_row.shape[1] - 1 if fwd else 0
    b_end = b_row[:, end:end + 1]
    big_m = jnp.maximum(m_prev, amax_row)
    m_row = b_row + big_m
    m_new = m_row[:, end:end + 1]
    w_s = jnp.exp(b_end + a_row - m_new)
    decay = jnp.exp(b_end + m_prev - m_new)
    return big_m, m_row, m_new, w_s, decay


def _ml_state_lhs(v, w_s):
    vt = v.astype(F32).T
    lhs = jnp.concatenate([vt * w_s, jnp.broadcast_to(w_s, (ML_PAD, vt.shape[1]))], axis=0)
    return vt, lhs.astype(BF16)


def _mlstm_kernel(cq, ck, cv, cgc, cgr, xq, xk, xv, xo, xz, xgc, xgr, nw_ref, y_ref,
                  acc_f, acc_b, st_ref, wt_sc, vt_sc, lhs_sc, row_sc):
    c = ML_CHUNK
    ncc = cq.shape[0] // c
    nc = xq.shape[0] // c
    dh = xq.shape[1]
    assert nc % 2 == 0
    head = pl.program_id(1)
    nh = pl.num_programs(1)
    accs = (acc_f, acc_b)
    st_ref[...] = jnp.zeros_like(st_ref)
    lane = lax.broadcasted_iota(jnp.int32, (c, LANES), 1)
    lower, upper = _tri_masks(c)

    def chunk_rows(d, k, n):
        return pl.ds(pl.multiple_of((k if d == 0 else n - 1 - k) * c, c), c)

    def gate_rows(grr, d, rows):
        slot = d * nh + head
        return (grr[pl.ds(slot, 1), rows], grr[pl.ds(2 * nh + slot, 1), rows],
                grr[pl.ds(4 * nh + slot, 1), rows])

    def ctx_body(k, ms):
        out = []
        for d in range(2):
            rows = chunk_rows(d, k, ncc)
            _, _, m_new, w_s, decay = _ml_gates(*gate_rows(cgr, d, rows), ms[d], d == 0)
            _, lhs = _ml_state_lhs(cv[rows, :], w_s)
            st_ref[d] = decay * st_ref[d] + jnp.dot(lhs, ck[rows, :], preferred_element_type=F32)
            out.append(m_new)
        return tuple(out)

    def score_stage(d, k, slot, m_prev):
        rows = chunk_rows(d, k, nc)
        big_m, m_row, m_new, w_s, decay = _ml_gates(*gate_rows(xgr, d, rows), m_prev, d == 0)
        vt, lhs = _ml_state_lhs(xv[rows, :], w_s)
        lhs_sc[slot, d] = lhs
        vt_sc[slot, d] = vt.astype(BF16)
        a_col = jnp.sum(jnp.where(lane == d * nh + head, xgc[rows, :], 0.0), axis=1, keepdims=True)
        mask = upper if d == 0 else lower
        w_t = (jnp.exp(jnp.where(mask, a_col - big_m, -jnp.inf))
               * lax.dot_general(xk[rows, :], xq[rows, :], _NT, preferred_element_type=F32))
        wt_sc[slot, d] = w_t
        row_sc[slot, d, 0:1, :] = jnp.sum(w_t, axis=0, keepdims=True)
        row_sc[slot, d, 1:2, :] = jnp.exp(m_prev - big_m)
        row_sc[slot, d, 2:3, :] = jnp.exp(-m_row)
        row_sc[slot, d, 3:4, :] = jnp.broadcast_to(decay, (1, c))
        return m_new

    def output_stage(d, k, slot):
        rows = chunk_rows(d, k, nc)
        st = st_ref[d]
        r = lax.dot_general(st.astype(BF16), xq[rows, :], _NT, preferred_element_type=F32)
        w_inter = row_sc[slot, d, 1:2, :]
        den = w_inter * r[dh:dh + 1, :] + row_sc[slot, d, 0:1, :]
        inv = 1.0 / jnp.maximum(jnp.abs(den), row_sc[slot, d, 2:3, :])
        accs[d][:, rows] = (r[:dh, :] * (w_inter * inv)
                            + jnp.dot(vt_sc[slot, d], (wt_sc[slot, d] * inv).astype(BF16),
                                      preferred_element_type=F32))
        st_ref[d] = (row_sc[slot, d, 3:4, 0:1] * st
                     + jnp.dot(lhs_sc[slot, d], xk[rows, :], preferred_element_type=F32))

    def pair_body(kk, ms):
        for slot in range(2):
            k = 2 * kk + slot
            nxt = jnp.minimum(k + 1, nc - 1)
            for d in range(2):
                output_stage(d, k, slot)
            ms = tuple(score_stage(d, nxt, 1 - slot, ms[d]) for d in range(2))
        return ms

    def final_body(i, carry):
        rows = pl.ds(pl.multiple_of(i * c, c), c)
        tot = acc_f[:, rows] + acc_b[:, rows]
        mu = jnp.mean(tot, axis=0, keepdims=True)
        tc_ = tot - mu
        var = jnp.mean(tc_ * tc_, axis=0, keepdims=True)
        normed = (tc_ * lax.rsqrt(var + NORM_EPS)).T * nw_ref[...]
        gate = jax.nn.sigmoid(xo[rows, :].astype(F32)) * _silu(xz[rows, :].astype(F32))
        y_ref[rows, :] = (normed * gate).astype(y_ref.dtype)
        return carry

    ms = (jnp.zeros((1, 1), F32), jnp.zeros((1, 1), F32))
    ms = lax.fori_loop(0, ncc, ctx_body, ms)
    ms = tuple(score_stage(d, 0, 0, ms[d]) for d in range(2))
    lax.fori_loop(0, nc // 2, pair_body, ms)
    lax.fori_loop(0, nc, final_body, 0)


def _mlstm(qk_x, qk_c, u_x, u_c, v_col, o_col, z_col, gates_x, gates_c, ml_norm, w_b):
    bsz, t, _ = u_x.shape
    tc = u_c.shape[1]
    nh = w_b // ML_DH
    cb = ML_DH
    assert cb == COL_BLOCK and t % (2 * ML_CHUNK) == 0 and tc % ML_CHUNK == 0

    def col(k):
        return lambda b, h: (b, 0, k + h)

    def seq_specs(n, with_gates):
        specs = [pl.BlockSpec((None, n, cb), col(0)),
                 pl.BlockSpec((None, n, cb), col(nh)),
                 pl.BlockSpec((None, n, cb), col(v_col // cb))]
        if with_gates:
            specs += [pl.BlockSpec((None, n, cb), col(o_col // cb)),
                      pl.BlockSpec((None, n, cb), col(z_col // cb))]
        specs += [pl.BlockSpec((n, LANES), lambda b, h: (b, 0)),
                  pl.BlockSpec((GATE_ROWS, n), lambda b, h: (0, b))]
        return specs

    return pl.pallas_call(
        _mlstm_kernel,
        out_shape=jax.ShapeDtypeStruct((bsz, t, w_b), BF16),
        grid=(bsz, nh),
        in_specs=seq_specs(tc, False) + seq_specs(t, True) + [pl.BlockSpec((1, cb), lambda b, h: (0, h))],
        out_specs=pl.BlockSpec((None, t, cb), lambda b, h: (b, 0, h)),
        scratch_shapes=[pltpu.VMEM((cb, t), F32),
                        pltpu.VMEM((cb, t), F32),
                        pltpu.VMEM((2, ML_DH + ML_PAD, ML_DH), F32),
                        pltpu.VMEM((2, 2, ML_CHUNK, ML_CHUNK), F32),
                        pltpu.VMEM((2, 2, ML_DH, ML_CHUNK), BF16),
                        pltpu.VMEM((2, 2, ML_DH + ML_PAD, ML_CHUNK), BF16),
                        pltpu.VMEM((2, 2, 8, ML_CHUNK), F32)],
        compiler_params=_params(2),
        name="mlstm",
    )(qk_c, qk_c, u_c, gates_c[0], gates_c[1],
      qk_x, qk_x, u_x, u_x, u_x, gates_x[0], gates_x[1],
      ml_norm.reshape(1, w_b).astype(F32))


def _outproj_kernel(ya_ref, yb_ref, x_ref, gate_ref, wa_ref, wb_ref, g_ref, b_ref, o_ref, *, sub):
    for r0 in range(0, x_ref.shape[0], sub):
        rows = slice(r0, r0 + sub)
        y = (jnp.dot(ya_ref[rows, :], wa_ref[...], preferred_element_type=F32)
             + jnp.dot(yb_ref[rows, :], wb_ref[...], preferred_element_type=F32))
        r = ALPHA * x_ref[rows, :] + gate_ref[...] * y
        mu = jnp.mean(r, axis=-1, keepdims=True)
        rc = r - mu
        var = jnp.mean(rc * rc, axis=-1, keepdims=True)
        o_ref[rows, :] = rc * lax.rsqrt(var + LN_EPS) * g_ref[...] + b_ref[...]


def _outproj(y_a, y_b, x2, mod3, t, w_oa, w_ob, ln_g, ln_b):
    rows, d = x2.shape
    tm = 512
    wa, wb = y_a.shape[1], y_b.shape[1]
    return pl.pallas_call(
        functools.partial(_outproj_kernel, sub=256),
        out_shape=jax.ShapeDtypeStruct((rows, d), F32),
        grid=(rows // tm,),
        in_specs=[pl.BlockSpec((tm, wa), lambda i: (i, 0)),
                  pl.BlockSpec((tm, wb), lambda i: (i, 0)),
                  pl.BlockSpec((tm, d), lambda i: (i, 0)),
                  pl.BlockSpec((None, 1, d), lambda i: ((i * tm) // t, 0, 2)),
                  pl.BlockSpec((wa, d), lambda i: (0, 0)),
                  pl.BlockSpec((wb, d), lambda i: (0, 0)),
                  pl.BlockSpec((1, d), lambda i: (0, 0)),
                  pl.BlockSpec((1, d), lambda i: (0, 0))],
        out_specs=pl.BlockSpec((tm, d), lambda i: (i, 0)),
        compiler_params=_params(1),
        name="outproj",
    )(y_a, y_b, x2, mod3, w_oa, w_ob, ln_g.reshape(1, d).astype(F32), ln_b.reshape(1, d).astype(F32))


def kernel(x, c, ctx, c_ctx, w_mod, b_mod, w_in, conv_w, conv_b, hg_lb, ml_gate_b, hg_norm_w,
           ml_norm_w, w_out, ln_g, ln_b):
    bsz, t, d = x.shape
    tc = ctx.shape[1]
    assert w_in.shape[0] == DEPTH and hg_lb.shape[1] == DEPTH + 1
    d_inner = w_out.shape[1]
    w_a = hg_lb.shape[2]
    w_b = d_inner - w_a
    nh = ml_gate_b.shape[2]
    nu = 5 * w_a + 5 * w_b
    assert nh * ML_DH == w_b and w_in.shape[2] == nu + 4 * nh and t % GRID_W == 0

    mod_rows = 8 * ((bsz + 1 + 7) // 8)
    cc = jnp.zeros((mod_rows, d), F32).at[:bsz].set(c).at[bsz].set(c_ctx)
    mod3 = _modulation(cc, w_mod[0], b_mod[0]).reshape(mod_rows, 1, 3 * d)

    w_in_t = w_in[0].T
    w_main = _wcast(w_in_t, 0, nu, 512)
    w_g = _wcast(w_in_t, nu, LANES, LANES)
    tm = min(1024, t)
    u_x, g_x = _inproj(x.reshape(bsz * t, d), mod3, lambda i: (i * tm) // t, w_main, w_g, tm)
    u_c, g_c = _inproj(ctx.reshape(bsz * tc, d), mod3, lambda i: bsz, w_main, w_g, min(1024, bsz * tc))
    u_x = u_x.reshape(bsz, t, nu)
    u_c = u_c.reshape(bsz, tc, nu)

    gates_x = _gateprep(g_x, ml_gate_b[0], nh, ML_CHUNK)
    gates_c = _gateprep(g_c, ml_gate_b[0], nh, ML_CHUNK)

    qk_col = 5 * w_a
    qk_x = _short_conv(u_x, qk_col, 2 * w_b, conv_w[0], conv_b[0], grid=True)
    qk_c = _short_conv(u_c, qk_col, 2 * w_b, conv_w[0], conv_b[0], grid=False)

    y_a = _hgrn2(u_x, u_c, w_a, hg_lb, hg_norm_w[0])
    y_b = _mlstm(qk_x, qk_c, u_x, u_c, qk_col + 2 * w_b, qk_col + 3 * w_b, qk_col + 4 * w_b,
                 gates_x, gates_c, ml_norm_w[0], w_b)

    w_o = w_out[0].astype(BF16)
    out = _outproj(y_a.reshape(bsz * t, w_a), y_b.reshape(bsz * t, w_b), x.reshape(bsz * t, d), mod3, t,
                   w_o[:w_a], w_o[w_a:], ln_g[0], ln_b[0])
    return out.reshape(bsz, t, d)
```

```python
import functools

import jax
import jax.numpy as jnp
from jax import lax
from jax.experimental import pallas as pl
from jax.experimental.pallas import tpu as pltpu

F32 = jnp.float32
BF16 = jnp.bfloat16

LN_EPS = 1e-5
NORM_EPS = 1e-6
DEPTH = 1
ALPHA = (2 * DEPTH) ** 0.25

HG_DK = 128
ML_DH = 256
GRID_W = 64
CONV_K = 3

LANES = 128
HG_CHUNK = 64
HG_SUB = 8
HG_SAFE_LOG2 = 64.0
LOG2E = 1.4426950408889634
ML_CHUNK = 256
ML_PAD = 16
GATE_ROWS = 24
COL_BLOCK = 256
VMEM_LIMIT = 56 * 1024 * 1024

_NT = (((1,), (1,)), ((), ()))
_TN = (((0,), (0,)), ((), ()))


def _params(n_grid):
    return pltpu.CompilerParams(dimension_semantics=("arbitrary",) * n_grid,
                                vmem_limit_bytes=VMEM_LIMIT)


def _silu(a):
    return a * jax.nn.sigmoid(a)


def _log_sigmoid(a):
    return jnp.minimum(a, 0.0) - jnp.log1p(jnp.exp(-jnp.abs(a)))


def _split3(a):
    hi = a.astype(BF16)
    r1 = a - hi.astype(F32)
    mid = r1.astype(BF16)
    lo = (r1 - mid.astype(F32)).astype(BF16)
    return hi, mid, lo


def _tri_left(tri, a):
    return sum(jnp.dot(tri, t, preferred_element_type=F32) for t in _split3(a))


def _tri_right(a, tri):
    return sum(jnp.dot(t, tri, preferred_element_type=F32) for t in _split3(a))


def _tri_masks(n):
    r = lax.broadcasted_iota(jnp.int32, (n, n), 0)
    c = lax.broadcasted_iota(jnp.int32, (n, n), 1)
    return r >= c, r <= c


def _mod_kernel(c_ref, w_ref, b_ref, o_ref):
    s = _silu(c_ref[...]).astype(BF16)
    o_ref[...] = jnp.dot(s, w_ref[...].astype(BF16), preferred_element_type=F32) + b_ref[...]


def _modulation(cc, w_mod, b_mod):
    rows, d = cc.shape
    n = w_mod.shape[1]
    tn = 512
    return pl.pallas_call(
        _mod_kernel,
        out_shape=jax.ShapeDtypeStruct((rows, n), F32),
        grid=(n // tn,),
        in_specs=[pl.BlockSpec((rows, d), lambda j: (0, 0)),
                  pl.BlockSpec((d, tn), lambda j: (0, j)),
                  pl.BlockSpec((1, tn), lambda j: (0, j))],
        out_specs=pl.BlockSpec((rows, tn), lambda j: (0, j)),
        compiler_params=_params(1),
        name="mod",
    )(cc, w_mod, b_mod.reshape(1, n))


def _wcast_kernel(wt_ref, o_ref, *, valid):
    w = wt_ref[...]
    if valid < w.shape[0]:
        w = jnp.where(lax.broadcasted_iota(jnp.int32, w.shape, 0) < valid, w, 0.0)
    o_ref[...] = w.T.astype(o_ref.dtype)


def _wcast(wt, col0, n_out, tn):
    n, d = wt.shape
    assert col0 % tn == 0 and n_out % tn == 0
    return pl.pallas_call(
        functools.partial(_wcast_kernel, valid=min(tn, n - col0)),
        out_shape=jax.ShapeDtypeStruct((d, n_out), BF16),
        grid=(n_out // tn,),
        in_specs=[pl.BlockSpec((tn, d), lambda j: (col0 // tn + j, 0))],
        out_specs=pl.BlockSpec((d, tn), lambda j: (0, j)),
        compiler_params=_params(1),
        name="wcast",
    )(wt)


def _ln_modulate(xv, shift, scale):
    mu = jnp.mean(xv, axis=-1, keepdims=True)
    xc = xv - mu
    var = jnp.mean(xc * xc, axis=-1, keepdims=True)
    return (xc * lax.rsqrt(var + LN_EPS) * (1.0 + scale) + shift).astype(BF16)


def _ln_kernel(x_ref, sh_ref, sc_ref, h_ref):
    h_ref[...] = _ln_modulate(x_ref[...], sh_ref[...], sc_ref[...])


def _inproj_kernel(xn_ref, h0_ref, sh_ref, sc_ref, w_ref, wg_ref, u_ref, g_ref, h_a, h_b, *, nsub):
    i = pl.program_id(0)
    j = pl.program_id(1)
    rs = xn_ref.shape[0]

    @pl.when((i == 0) & (j == 0))
    def _():
        h_a[...] = h0_ref[...]

    def body(h_cur, h_nxt):
        @pl.when(j == 0)
        def _():
            g_ref[...] = jnp.dot(h_cur[...], wg_ref[...], preferred_element_type=F32)

        rows = pl.ds(pl.multiple_of(jnp.minimum(j, nsub - 1) * rs, rs), rs)
        h_nxt[rows, :] = _ln_modulate(xn_ref[...], sh_ref[...], sc_ref[...])
        u_ref[...] = jnp.dot(h_cur[...], w_ref[...], preferred_element_type=F32).astype(BF16)

    pl.when(i % 2 == 0)(lambda: body(h_a, h_b))
    pl.when(i % 2 == 1)(lambda: body(h_b, h_a))


def _inproj(x2, mod3, mod_row, w_main, w_g, tm):
    rows, d = x2.shape
    nu = w_main.shape[1]
    tn = 1024 if nu % 1024 == 0 else 512
    nt, nj = rows // tm, nu // tn
    nsub = min(8, 1 << (nj.bit_length() - 1))
    rs = tm // nsub
    assert nu % tn == 0 and rows % tm == 0 and tm % nsub == 0

    def mod_spec(part, row_of):
        return pl.BlockSpec((None, 1, d), lambda *ij: (row_of(ij[0]), 0, part))

    h0 = pl.pallas_call(
        _ln_kernel,
        out_shape=jax.ShapeDtypeStruct((tm, d), BF16),
        grid=(2,),
        in_specs=[pl.BlockSpec((tm // 2, d), lambda r: (r, 0)),
                  mod_spec(0, lambda r: mod_row(0)), mod_spec(1, lambda r: mod_row(0))],
        out_specs=pl.BlockSpec((tm // 2, d), lambda r: (r, 0)),
        compiler_params=_params(1),
        name="ln0",
    )(x2, mod3, mod3)

    def nxt(i):
        return jnp.minimum(i + 1, nt - 1)

    return pl.pallas_call(
        functools.partial(_inproj_kernel, nsub=nsub),
        out_shape=(jax.ShapeDtypeStruct((rows, nu), BF16),
                   jax.ShapeDtypeStruct((rows, LANES), F32)),
        grid=(nt, nj),
        in_specs=[pl.BlockSpec((rs, d), lambda i, j: (nxt(i) * nsub + jnp.minimum(j, nsub - 1), 0)),
                  pl.BlockSpec((tm, d), lambda i, j: (0, 0)),
                  mod_spec(0, lambda i: mod_row(nxt(i))), mod_spec(1, lambda i: mod_row(nxt(i))),
                  pl.BlockSpec((d, tn), lambda i, j: (0, j)),
                  pl.BlockSpec((d, LANES), lambda i, j: (0, 0))],
        out_specs=(pl.BlockSpec((tm, tn), lambda i, j: (i, j)),
                   pl.BlockSpec((tm, LANES), lambda i, j: (i, 0))),
        scratch_shapes=[pltpu.VMEM((tm, d), BF16), pltpu.VMEM((tm, d), BF16)],
        compiler_params=_params(2),
        name="inproj",
    )(x2, h0, mod3, mod3, w_main, w_g)


def _gateprep_kernel(g_ref, bias_ref, gc_ref, gr_ref, *, nh, c):
    lower, upper = _tri_masks(c)
    tril = jnp.where(lower, 1.0, 0.0).astype(BF16)
    triu = jnp.where(upper, 1.0, 0.0).astype(BF16)
    lane = lax.broadcasted_iota(jnp.int32, (c, LANES), 1)
    fwd_f = (lane >= 2 * nh) & (lane < 3 * nh)
    bwd_f = (lane >= 3 * nh) & (lane < 4 * nh)
    row = lax.broadcasted_iota(jnp.int32, (GATE_ROWS, c), 0)
    pos = lax.broadcasted_iota(jnp.int32, (GATE_ROWS, c), 1)
    fwd_r = (row >= 2 * nh) & (row < 3 * nh)
    bwd_r = (row >= 3 * nh) & (row < 4 * nh)

    for r0 in range(0, g_ref.shape[0], c):
        pre = g_ref[r0:r0 + c, :] + bias_ref[...]
        lf = _log_sigmoid(pre)
        b = jnp.where(fwd_f, _tri_left(tril, lf), jnp.where(bwd_f, _tri_left(triu, lf), 0.0))
        gc_ref[r0:r0 + c, :] = pre - pltpu.roll(b, LANES - 2 * nh, axis=1)

        pre_t = pre.T[:GATE_ROWS]
        lf_t = _log_sigmoid(pre_t)
        b_t = jnp.where(fwd_r, _tri_right(lf_t, triu), jnp.where(bwd_r, _tri_right(lf_t, tril), 0.0))
        a_t = pre_t - pltpu.roll(b_t, GATE_ROWS - 2 * nh, axis=0)
        pmax, smax = a_t, a_t
        shift = 1
        while shift < c:
            pmax = jnp.maximum(pmax, jnp.where(pos >= shift, pltpu.roll(pmax, shift, axis=1), -jnp.inf))
            smax = jnp.maximum(smax, jnp.where(pos < c - shift, pltpu.roll(smax, c - shift, axis=1), -jnp.inf))
            shift *= 2
        amax_t = pltpu.roll(jnp.where(row < nh, pmax, smax), 4 * nh, axis=0)
        gr_ref[:, r0:r0 + c] = jnp.where(row < 2 * nh, a_t, jnp.where(row < 4 * nh, b_t, amax_t))


def _gateprep(g, bias, nh, chunk):
    rows = g.shape[0]
    assert 6 * nh <= GATE_ROWS
    blk = 4 * chunk if rows % (4 * chunk) == 0 else chunk
    bias_row = jnp.zeros((1, LANES), F32).at[0, :4 * nh].set(bias.reshape(-1))
    return pl.pallas_call(
        functools.partial(_gateprep_kernel, nh=nh, c=chunk),
        out_shape=(jax.ShapeDtypeStruct((rows, LANES), F32),
                   jax.ShapeDtypeStruct((GATE_ROWS, rows), F32)),
        grid=(rows // blk,),
        in_specs=[pl.BlockSpec((blk, LANES), lambda i: (i, 0)),
                  pl.BlockSpec((1, LANES), lambda i: (0, 0))],
        out_specs=(pl.BlockSpec((blk, LANES), lambda i: (i, 0)),
                   pl.BlockSpec((GATE_ROWS, blk), lambda i: (0, i))),
        compiler_params=_params(1),
        name="gateprep",
    )(g, bias_row)


def _conv_kernel(x_ref, w_ref, b_ref, o_ref, *, rows, width, taps):
    cb = x_ref.shape[-1]
    j = pl.program_id(1)
    scale = jnp.where(j >= pl.num_programs(1) // 2, ML_DH ** -0.5, 1.0).astype(F32)
    w = w_ref[...]
    bias = b_ref[...]
    col = lax.broadcasted_iota(jnp.int32, (width, cb), 0)
    first = col == 0
    last = col == width - 1

    def one_row(r):
        left = mid = right = None
        for ki in taps:
            rr = r + (ki - 1)
            valid = jnp.where((rr >= 0) & (rr < rows), 1.0, 0.0).astype(F32)
            wk = w[3 * ki:3 * ki + 3, :] * valid
            start = pl.multiple_of(jnp.clip(rr, 0, rows - 1) * width, width)
            xv = x_ref[pl.ds(start, width), :].astype(F32)
            terms = [xv * wk[kj:kj + 1, :] for kj in range(3)]
            left, mid, right = terms if left is None else (left + terms[0], mid + terms[1], right + terms[2])
        out = (mid + jnp.where(first, 0.0, pltpu.roll(left, 1, axis=0))
               + jnp.where(last, 0.0, pltpu.roll(right, width - 1, axis=0)) + bias)
        dst = pl.multiple_of(r * width, width)
        o_ref[pl.ds(dst, width), :] = (_silu(out) * scale).astype(o_ref.dtype)

    def body(r, carry):
        one_row(r)
        return carry

    lax.fori_loop(0, rows, body, 0)


def _short_conv(u, col0, n_ch, conv_w, conv_b, grid):
    bsz, t, _ = u.shape
    cb = COL_BLOCK
    rows, width, taps = (t // GRID_W, GRID_W, (0, 1, 2)) if grid else (1, t, (1,))
    return pl.pallas_call(
        functools.partial(_conv_kernel, rows=rows, width=width, taps=taps),
        out_shape=jax.ShapeDtypeStruct((bsz, t, n_ch), BF16),
        grid=(bsz, n_ch // cb),
        in_specs=[pl.BlockSpec((None, t, cb), lambda b, j: (b, 0, col0 // cb + j)),
                  pl.BlockSpec((CONV_K * CONV_K, cb), lambda b, j: (0, j)),
                  pl.BlockSpec((1, cb), lambda b, j: (0, j))],
        out_specs=pl.BlockSpec((None, t, cb), lambda b, j: (b, 0, j)),
        compiler_params=_params(2),
        name="conv_grid" if grid else "conv_seq",
    )(u, conv_w.reshape(CONV_K * CONV_K, n_ch).astype(F32), conv_b.reshape(1, n_ch).astype(F32))


def _hg_prepare(qraw, z, lb, fwd):
    lower, upper = _tri_masks(z.shape[0])
    tri = jnp.where(lower if fwd else upper, 1.0, 0.0).astype(BF16)
    f = lb + (1.0 - lb) * jax.nn.sigmoid(z)
    b2 = _tri_left(tri, jnp.log(f) * LOG2E)
    c2 = b2 - jnp.log(1.0 - f) * LOG2E
    return _silu(qraw), b2, c2


def _block_diag(a):
    r, w = a.shape
    zero = jnp.zeros((r, w // 2), a.dtype)
    return jnp.concatenate([jnp.concatenate([a[:, :w // 2], zero], axis=1),
                            jnp.concatenate([zero, a[:, w // 2:]], axis=1)], axis=0)


def _hg_scores(q_ref, b_ref, c_ref, fwd):
    c, w = q_ref.shape
    sub = HG_SUB
    nb = c // sub
    assert 2 * c == LANES and c & (c - 1) == 0
    row = lax.broadcasted_iota(jnp.int32, (sub, LANES), 0)
    col = lax.broadcasted_iota(jnp.int32, (sub, LANES), 1) & (c - 1)
    src = col & (sub - 1)
    code = jnp.where((src <= row) if fwd else (src >= row), col, -1)
    kr = lax.broadcasted_iota(jnp.int32, (w, LANES), 0)
    kl = lax.broadcasted_iota(jnp.int32, (w, LANES), 1)
    sel = jnp.where((kr >= w // 2) == (kl >= c), 1.0, 0.0).astype(BF16)

    units = []
    for j in range(nb):
        r0 = j * sub
        qb = q_ref[r0:r0 + sub, :]
        bb = b_ref[r0:r0 + sub, :]
        cb = c_ref[r0:r0 + sub, :]
        for s in range(sub):
            units.append(qb * jnp.exp2(bb - cb[s:s + 1, :]))
    red = jnp.dot(jnp.concatenate(units, axis=0).astype(BF16), sel, preferred_element_type=F32)

    blocks = []
    for j in range(nb):
        r0 = j * sub
        blk = jnp.zeros((sub, LANES), F32)
        for s in range(sub):
            u0 = (j * sub + s) * sub
            blk = jnp.where(code == r0 + s, red[u0:u0 + sub], blk)
        lo, hi = (0, r0) if fwd else (r0 + sub, c)
        if hi > lo:
            beta = b_ref[pl.ds(r0 - 1 if fwd else r0 + sub, 1), :]
            pieces = [jnp.zeros((lo, w), F32)] if lo else []
            pieces.append(jnp.exp2(beta - c_ref[lo:hi, :]))
            if hi < c:
                pieces.append(jnp.zeros((c - hi, w), F32))
            kh = jnp.concatenate(pieces, axis=0).astype(BF16)
            qh = q_ref[r0:r0 + sub, :] * jnp.exp2(b_ref[r0:r0 + sub, :] - beta)
            off = lax.dot_general(_block_diag(qh).astype(BF16), kh, _NT, preferred_element_type=F32)
            blk = blk + jnp.concatenate([off[:sub], off[sub:]], axis=1)
        blocks.append(blk)
    return jnp.concatenate(blocks, axis=0).astype(BF16)


def _hg_scores_factored(q_ref, b_ref, c_ref, fwd):
    c, w = q_ref.shape
    sub = HG_SUB
    nb = c // sub
    row = lax.broadcasted_iota(jnp.int32, (sub, LANES), 0)
    col = lax.broadcasted_iota(jnp.int32, (sub, LANES), 1) & (c - 1)
    blocks = []
    for j in range(nb):
        r0 = j * sub
        lo, hi = (0, r0 + sub) if fwd else (r0, c)
        edge = r0 - 1 if fwd else r0 + sub
        beta = b_ref[pl.ds(edge, 1), :] if 0 <= edge < c else jnp.zeros((1, w), F32)
        pieces = [jnp.zeros((lo, w), F32)] if lo else []
        pieces.append(jnp.exp2(beta - c_ref[lo:hi, :]))
        if hi < c:
            pieces.append(jnp.zeros((c - hi, w), F32))
        kh = jnp.concatenate(pieces, axis=0).astype(BF16)
        qh = q_ref[r0:r0 + sub, :] * jnp.exp2(b_ref[r0:r0 + sub, :] - beta)
        off = lax.dot_general(_block_diag(qh).astype(BF16), kh, _NT, preferred_element_type=F32)
        blk = jnp.concatenate([off[:sub], off[sub:]], axis=1)
        causal = (col <= row + r0) if fwd else (col >= row + r0)
        blocks.append(jnp.where(causal, blk, 0.0))
    return jnp.concatenate(blocks, axis=0).astype(BF16)


def _hg_state_update(st_refs, v, k_dec, dec, qe=None):
    hw = v.shape[1] // 2
    outs = []
    for h, st_ref in enumerate(st_refs):
        ls = slice(h * hw, (h + 1) * hw)
        st = st_ref[...]
        st_ref[...] = st * dec[:, ls] + lax.dot_general(v[:, ls], k_dec[:, ls], _TN, preferred_element_type=F32)
        if qe is not None:
            outs.append(lax.dot_general(qe[:, ls], st.astype(BF16), _NT, preferred_element_type=F32))
    return outs


def _hgrn2_kernel(cq, cff, cfb, ci, aq, aff, afb, ai, az, lb_ref, nw_ref, y_ref,
                  acc_f, acc_b, st_ref, q_sc, b_sc, c_sc, a_sc, qe_sc, kd_sc, dec_sc, *, slots):
    c = HG_CHUNK
    ncc = cq.shape[0] // c
    nc = aq.shape[0] // c
    hw = HG_DK
    assert nc % 4 == 0
    st_ref[...] = jnp.zeros_like(st_ref)
    accs = (acc_f, acc_b)
    lbs = []
    for d in range(2):
        hg = lb_ref[d * slots:(d + 1) * slots, :]
        e = jnp.exp(hg - jnp.max(hg, axis=0, keepdims=True))
        lbs.append(e[0:1, :] / jnp.sum(e, axis=0, keepdims=True))

    def chunk_rows(d, k, n):
        return pl.ds(pl.multiple_of((k if d == 0 else n - 1 - k) * c, c), c)

    def states(d):
        return st_ref.at[2 * d], st_ref.at[2 * d + 1]

    def ctx_body(k, carry):
        for d in range(2):
            rows = chunk_rows(d, k, ncc)
            _, b2, c2 = _hg_prepare(cq[rows, :].astype(F32), (cff, cfb)[d][rows, :].astype(F32), lbs[d], d == 0)
            end = c - 1 if d == 0 else 0
            b_end = b2[end:end + 1, :]
            _hg_state_update(states(d), ci[rows, :], jnp.exp2(b_end - c2).astype(BF16), jnp.exp2(b_end))
        return carry

    def gates(d, k, slot):
        rows = chunk_rows(d, k, nc)
        q, b2, c2 = _hg_prepare(aq[rows, :].astype(F32), (aff, afb)[d][rows, :].astype(F32), lbs[d], d == 0)
        q_sc[slot, d] = q
        b_sc[slot, d] = b2
        c_sc[slot, d] = c2

    def scores(d, slot, exact):
        qr, br, cr = q_sc.at[slot, d], b_sc.at[slot, d], c_sc.at[slot, d]
        end = c - 1 if d == 0 else 0
        b_end = br[end:end + 1, :]
        qe_sc[slot, d] = (qr[...] * jnp.exp2(br[...])).astype(BF16)
        kd_sc[slot, d] = jnp.exp2(b_end - cr[...]).astype(BF16)
        dec_sc[slot, d] = jnp.exp2(b_end)
        a_sc[slot, d] = (_hg_scores if exact else _hg_scores_factored)(qr, br, cr, d == 0)

    def outputs(d, k, slot):
        rows = chunk_rows(d, k, nc)
        v = ai[rows, :]
        outs = _hg_state_update(states(d), v, kd_sc[slot, d], dec_sc[slot, d], qe_sc[slot, d])
        accs[d][rows, :] = (jnp.concatenate(outs, axis=1)
                            + jnp.dot(a_sc[slot, d], _block_diag(v), preferred_element_type=F32))

    def step(k, slot, exact, do_out=True, do_gates=True, do_scores=True):
        for d in range(2):
            if do_out:
                outputs(d, k - 2, slot)
            if do_gates:
                gates(d, k, slot)
        for d in range(2):
            if do_scores:
                scores(d, 1 - slot, exact)

    def scan(exact):
        def quad_body(kq, carry):
            for i in range(4):
                step(4 * kq + i, i % 2, exact)
            return carry

        step(0, 0, exact, do_out=False, do_scores=False)
        step(1, 1, exact, do_out=False)
        step(2, 0, exact)
        step(3, 1, exact)
        lax.fori_loop(1, nc // 4, quad_body, 0)
        step(nc, 0, exact, do_gates=False)
        step(nc + 1, 1, exact, do_gates=False, do_scores=False)

    def final_body(i, carry):
        rows = pl.ds(pl.multiple_of(i * c, c), c)
        tot = acc_f[rows, :] + acc_b[rows, :]
        gate = _silu(az[rows, :].astype(F32)) * nw_ref[...]
        for h in range(2):
            ls = slice(h * hw, (h + 1) * hw)
            th = tot[:, ls]
            ms = jnp.mean(th * th, axis=-1, keepdims=True)
            y_ref[rows, ls] = (th * lax.rsqrt(ms + NORM_EPS) * gate[:, ls]).astype(y_ref.dtype)
        return carry

    lax.fori_loop(0, ncc, ctx_body, 0, unroll=True)
    depth = HG_SUB * jnp.max(-jnp.log(jnp.minimum(lbs[0], lbs[1])) * LOG2E)
    pl.when(depth <= HG_SAFE_LOG2)(lambda: scan(False))
    pl.when(jnp.logical_not(depth <= HG_SAFE_LOG2))(lambda: scan(True))
    lax.fori_loop(0, nc, final_body, 0, unroll=4)


def _hgrn2(u_x, u_c, w_a, hg_lb, hg_norm):
    bsz, t, _ = u_x.shape
    tc = u_c.shape[1]
    cb = 2 * HG_DK
    c = HG_CHUNK
    assert cb == COL_BLOCK
    seg = w_a // cb
    slots = hg_lb.shape[1]
    lb2 = hg_lb.reshape(2 * slots, w_a).astype(F32)

    def col(k):
        return lambda b, p: (b, 0, k * seg + p)

    ctx_specs = [pl.BlockSpec((None, tc, cb), col(k)) for k in (0, 1, 2, 3)]
    lat_specs = [pl.BlockSpec((None, t, cb), col(k)) for k in (0, 1, 2, 3, 4)]
    return pl.pallas_call(
        functools.partial(_hgrn2_kernel, slots=slots),
        out_shape=jax.ShapeDtypeStruct((bsz, t, w_a), BF16),
        grid=(bsz, seg),
        in_specs=ctx_specs + lat_specs + [
            pl.BlockSpec((2 * slots, cb), lambda b, p: (0, p)),
            pl.BlockSpec((1, cb), lambda b, p: (0, p))],
        out_specs=pl.BlockSpec((None, t, cb), lambda b, p: (b, 0, p)),
        scratch_shapes=[pltpu.VMEM((t, cb), F32),
                        pltpu.VMEM((t, cb), F32),
                        pltpu.VMEM((4, HG_DK, HG_DK), F32),
                        pltpu.VMEM((2, 2, c, cb), F32),
                        pltpu.VMEM((2, 2, c, cb), F32),
                        pltpu.VMEM((2, 2, c, cb), F32),
                        pltpu.VMEM((2, 2, c, 2 * c), BF16),
                        pltpu.VMEM((2, 2, c, cb), BF16),
                        pltpu.VMEM((2, 2, c, cb), BF16),
                        pltpu.VMEM((2, 2, 1, cb), F32)],
        compiler_params=_params(2),
        name="hgrn2",
    )(u_c, u_c, u_c, u_c, u_x, u_x, u_x, u_x, u_x, lb2, hg_norm.reshape(1, w_a).astype(F32))


def _ml_gates(a_row, b_row, amax_row, m_prev, fwd):
    end = a_row.shape[1] - 1 if fwd else 0
    b_end = b_row[:, end:end + 1]
    big_m = jnp.maximum(m_prev, amax_row)
    m_row = b_row + big_m
    m_new = m_row[:, end:end + 1]
    w_s = jnp.exp(b_end + a_row - m_new)
    decay = jnp.exp(b_end + m_prev - m_new)
    return big_m, m_row, m_new, w_s, decay


def _ml_state_lhs(v, w_s):
    vt = v.astype(F32).T
    lhs = jnp.concatenate([vt * w_s, jnp.broadcast_to(w_s, (ML_PAD, vt.shape[1]))], axis=0)
    return vt, lhs.astype(BF16)


def _mlstm_kernel(cq, ck, cv, cgc, cgr, xq, xk, xv, xo, xz, xgc, xgr, nw_ref, y_ref,
                  acc_f, acc_b, st_ref, wt_sc, vt_sc, lhs_sc, row_sc):
    c = ML_CHUNK
    ncc = cq.shape[0] // c
    nc = xq.shape[0] // c
    dh = xq.shape[1]
    assert nc % 2 == 0
    head = pl.program_id(1)
    nh = pl.num_programs(1)
    accs = (acc_f, acc_b)
    st_ref[...] = jnp.zeros_like(st_ref)
    lane = lax.broadcasted_iota(jnp.int32, (c, LANES), 1)
    lower, upper = _tri_masks(c)

    def chunk_rows(d, k, n):
        return pl.ds(pl.multiple_of((k if d == 0 else n - 1 - k) * c, c), c)

    def gate_rows(grr, d, rows):
        slot = d * nh + head
        return (grr[pl.ds(slot, 1), rows], grr[pl.ds(2 * nh + slot, 1), rows],
                grr[pl.ds(4 * nh + slot, 1), rows])

    def ctx_body(k, ms):
        out = []
        for d in range(2):
            rows = chunk_rows(d, k, ncc)
            _, _, m_new, w_s, decay = _ml_gates(*gate_rows(cgr, d, rows), ms[d], d == 0)
            _, lhs = _ml_state_lhs(cv[rows, :], w_s)
            st_ref[d] = decay * st_ref[d] + jnp.dot(lhs, ck[rows, :], preferred_element_type=F32)
            out.append(m_new)
        return tuple(out)

    def score_stage(d, k, slot, m_prev):
        rows = chunk_rows(d, k, nc)
        big_m, m_row, m_new, w_s, decay = _ml_gates(*gate_rows(xgr, d, rows), m_prev, d == 0)
        vt, lhs = _ml_state_lhs(xv[rows, :], w_s)
        lhs_sc[slot, d] = lhs
        vt_sc[slot, d] = vt.astype(BF16)
        a_col = jnp.sum(jnp.where(lane == d * nh + head, xgc[rows, :], 0.0), axis=1, keepdims=True)
        mask = upper if d == 0 else lower
        w_t = (jnp.exp(jnp.where(mask, a_col - big_m, -jnp.inf))
               * lax.dot_general(xk[rows, :], xq[rows, :], _NT, preferred_element_type=F32))
        wt_sc[slot, d] = w_t
        row_sc[slot, d, 0:1, :] = jnp.sum(w_t, axis=0, keepdims=True)
        row_sc[slot, d, 1:2, :] = jnp.exp(m_prev - big_m)
        row_sc[slot, d, 2:3, :] = jnp.exp(-m_row)
        row_sc[slot, d, 3:4, :] = jnp.broadcast_to(decay, (1, c))
        return m_new

    def output_stage(d, k, slot):
        rows = chunk_rows(d, k, nc)
        st = st_ref[d]
        r = lax.dot_general(st.astype(BF16), xq[rows, :], _NT, preferred_element_type=F32)
        w_inter = row_sc[slot, d, 1:2, :]
        den = w_inter * r[dh:dh + 1, :] + row_sc[slot, d, 0:1, :]
        inv = 1.0 / jnp.maximum(jnp.abs(den), row_sc[slot, d, 2:3, :])
        accs[d][:, rows] = (r[:dh, :] * (w_inter * inv)
                            + jnp.dot(vt_sc[slot, d], (wt_sc[slot, d] * inv).astype(BF16),
                                      preferred_element_type=F32))
        st_ref[d] = (row_sc[slot, d, 3:4, 0:1] * st
                     + jnp.dot(lhs_sc[slot, d], xk[rows, :], preferred_element_type=F32))

    def pair_body(kk, ms):
        for slot in range(2):
            k = 2 * kk + slot
            nxt = jnp.minimum(k + 1, nc - 1)
            for d in range(2):
                output_stage(d, k, slot)
            ms = tuple(score_stage(d, nxt, 1 - slot, ms[d]) for d in range(2))
        return ms

    def final_body(i, carry):
        rows = pl.ds(pl.multiple_of(i * c, c), c)
        tot = acc_f[:, rows] + acc_b[:, rows]
        mu = jnp.mean(tot, axis=0, keepdims=True)
        tc_ = tot - mu
        var = jnp.mean(tc_ * tc_, axis=0, keepdims=True)
        normed = (tc_ * lax.rsqrt(var + NORM_EPS)).T * nw_ref[...]
        gate = jax.nn.sigmoid(xo[rows, :].astype(F32)) * _silu(xz[rows, :].astype(F32))
        y_ref[rows, :] = (normed * gate).astype(y_ref.dtype)
        return carry

    ms = (jnp.zeros((1, 1), F32), jnp.zeros((1, 1), F32))
    ms = lax.fori_loop(0, ncc, ctx_body, ms)
    ms = tuple(score_stage(d, 0, 0, ms[d]) for d in range(2))
    lax.fori_loop(0, nc // 2, pair_body, ms)
    lax.fori_loop(0, nc, final_body, 0, unroll=2)


def _mlstm(qk_x, qk_c, u_x, u_c, v_col, o_col, z_col, gates_x, gates_c, ml_norm, w_b):
    bsz, t, _ = u_x.shape
    tc = u_c.shape[1]
    nh = w_b // ML_DH
    cb = ML_DH
    assert cb == COL_BLOCK and t % (2 * ML_CHUNK) == 0 and tc % ML_CHUNK == 0

    def col(k):
        return lambda b, h: (b, 0, k + h)

    def seq_specs(n, with_gates):
        specs = [pl.BlockSpec((None, n, cb), col(0)),
                 pl.BlockSpec((None, n, cb), col(nh)),
                 pl.BlockSpec((None, n, cb), col(v_col // cb))]
        if with_gates:
            specs += [pl.BlockSpec((None, n, cb), col(o_col // cb)),
                      pl.BlockSpec((None, n, cb), col(z_col // cb))]
        specs += [pl.BlockSpec((n, LANES), lambda b, h: (b, 0)),
                  pl.BlockSpec((GATE_ROWS, n), lambda b, h: (0, b))]
        return specs

    return pl.pallas_call(
        _mlstm_kernel,
        out_shape=jax.ShapeDtypeStruct((bsz, t, w_b), BF16),
        grid=(bsz, nh),
        in_specs=seq_specs(tc, False) + seq_specs(t, True) + [pl.BlockSpec((1, cb), lambda b, h: (0, h))],
        out_specs=pl.BlockSpec((None, t, cb), lambda b, h: (b, 0, h)),
        scratch_shapes=[pltpu.VMEM((cb, t), F32),
                        pltpu.VMEM((cb, t), F32),
                        pltpu.VMEM((2, ML_DH + ML_PAD, ML_DH), F32),
                        pltpu.VMEM((2, 2, ML_CHUNK, ML_CHUNK), F32),
                        pltpu.VMEM((2, 2, ML_DH, ML_CHUNK), BF16),
                        pltpu.VMEM((2, 2, ML_DH + ML_PAD, ML_CHUNK), BF16),
                        pltpu.VMEM((2, 2, 8, ML_CHUNK), F32)],
        compiler_params=_params(2),
        name="mlstm",
    )(qk_c, qk_c, u_c, gates_c[0], gates_c[1],
      qk_x, qk_x, u_x, u_x, u_x, gates_x[0], gates_x[1],
      ml_norm.reshape(1, w_b).astype(F32))


def _outproj_kernel(ya_ref, yb_ref, x_ref, gate_ref, wa_ref, wb_ref, g_ref, b_ref, o_ref, *, sub):
    for r0 in range(0, x_ref.shape[0], sub):
        rows = slice(r0, r0 + sub)
        y = (jnp.dot(ya_ref[rows, :], wa_ref[...], preferred_element_type=F32)
             + jnp.dot(yb_ref[rows, :], wb_ref[...], preferred_element_type=F32))
        r = ALPHA * x_ref[rows, :] + gate_ref[...] * y
        mu = jnp.mean(r, axis=-1, keepdims=True)
        rc = r - mu
        var = jnp.mean(rc * rc, axis=-1, keepdims=True)
        o_ref[rows, :] = rc * lax.rsqrt(var + LN_EPS) * g_ref[...] + b_ref[...]


def _outproj(y_a, y_b, x2, mod3, t, w_oa, w_ob, ln_g, ln_b):
    rows, d = x2.shape
    tm = 512
    wa, wb = y_a.shape[1], y_b.shape[1]
    return pl.pallas_call(
        functools.partial(_outproj_kernel, sub=256),
        out_shape=jax.ShapeDtypeStruct((rows, d), F32),
        grid=(rows // tm,),
        in_specs=[pl.BlockSpec((tm, wa), lambda i: (i, 0)),
                  pl.BlockSpec((tm, wb), lambda i: (i, 0)),
                  pl.BlockSpec((tm, d), lambda i: (i, 0)),
                  pl.BlockSpec((None, 1, d), lambda i: ((i * tm) // t, 0, 2)),
                  pl.BlockSpec((wa, d), lambda i: (0, 0)),
                  pl.BlockSpec((wb, d), lambda i: (0, 0)),
                  pl.BlockSpec((1, d), lambda i: (0, 0)),
                  pl.BlockSpec((1, d), lambda i: (0, 0))],
        out_specs=pl.BlockSpec((tm, d), lambda i: (i, 0)),
        compiler_params=_params(1),
        name="outproj",
    )(y_a, y_b, x2, mod3, w_oa, w_ob, ln_g.reshape(1, d).astype(F32), ln_b.reshape(1, d).astype(F32))


def kernel(x, c, ctx, c_ctx, w_mod, b_mod, w_in, conv_w, conv_b, hg_lb, ml_gate_b, hg_norm_w,
           ml_norm_w, w_out, ln_g, ln_b):
    bsz, t, d = x.shape
    tc = ctx.shape[1]
    assert w_in.shape[0] == DEPTH and hg_lb.shape[1] == DEPTH + 1
    d_inner = w_out.shape[1]
    w_a = hg_lb.shape[2]
    w_b = d_inner - w_a
    nh = ml_gate_b.shape[2]
    nu = 5 * w_a + 5 * w_b
    assert nh * ML_DH == w_b and w_in.shape[2] == nu + 4 * nh and t % GRID_W == 0

    mod_rows = 8 * ((bsz + 1 + 7) // 8)
    cc = jnp.zeros((mod_rows, d), F32).at[:bsz].set(c).at[bsz].set(c_ctx)
    mod3 = _modulation(cc, w_mod[0], b_mod[0]).reshape(mod_rows, 1, 3 * d)

    w_in_t = w_in[0].T
    w_main = _wcast(w_in_t, 0, nu, 512)
    w_g = _wcast(w_in_t, nu, LANES, LANES)
    tm = min(1024, t)
    u_x, g_x = _inproj(x.reshape(bsz * t, d), mod3, lambda i: (i * tm) // t, w_main, w_g, tm)
    u_c, g_c = _inproj(ctx.reshape(bsz * tc, d), mod3, lambda i: bsz, w_main, w_g, min(1024, bsz * tc))
    u_x = u_x.reshape(bsz, t, nu)
    u_c = u_c.reshape(bsz, tc, nu)

    gates_x = _gateprep(g_x, ml_gate_b[0], nh, ML_CHUNK)
    gates_c = _gateprep(g_c, ml_gate_b[0], nh, ML_CHUNK)

    qk_col = 5 * w_a
    qk_x = _short_conv(u_x, qk_col, 2 * w_b, conv_w[0], conv_b[0], grid=True)
    qk_c = _short_conv(u_c, qk_col, 2 * w_b, conv_w[0], conv_b[0], grid=False)

    y_a = _hgrn2(u_x, u_c, w_a, hg_lb, hg_norm_w[0])
    y_b = _mlstm(qk_x, qk_c, u_x, u_c, qk_col + 2 * w_b, qk_col + 3 * w_b, qk_col + 4 * w_b,
                 gates_x, gates_c, ml_norm_w[0], w_b)

    w_o = w_out[0].astype(BF16)
    out = _outproj(y_a.reshape(bsz * t, w_a), y_b.reshape(bsz * t, w_b), x.reshape(bsz * t, d), mod3, t,
                   w_o[:w_a], w_o[w_a:], ln_g[0], ln_b[0])
    return out.reshape(bsz, t, d)
```

```python
import functools

import jax
import jax.numpy as jnp
from jax import lax
from jax.experimental import pallas as pl
from jax.experimental.pallas import tpu as pltpu

F32 = jnp.float32
BF16 = jnp.bfloat16

LN_EPS = 1e-5
NORM_EPS = 1e-6
DEPTH = 1
ALPHA = (2 * DEPTH) ** 0.25

HG_DK = 128
ML_DH = 256
GRID_W = 64
CONV_K = 3

LANES = 128
HG_CHUNK = 64
HG_SUB = 8
HG_SAFE_LOG2 = 96.0
LOG2E = 1.4426950408889634
ML_CHUNK = 256
ML_PAD = 16
GATE_ROWS = 24
COL_BLOCK = 256
VMEM_LIMIT = 56 * 1024 * 1024

_NT = (((1,), (1,)), ((), ()))
_TN = (((0,), (0,)), ((), ()))


def _params(n_grid):
    return pltpu.CompilerParams(dimension_semantics=("arbitrary",) * n_grid,
                                vmem_limit_bytes=VMEM_LIMIT)


def _silu(a):
    return a * jax.nn.sigmoid(a)


def _log_sigmoid(a):
    return jnp.minimum(a, 0.0) - jnp.log1p(jnp.exp(-jnp.abs(a)))


def _split3(a):
    hi = a.astype(BF16)
    r1 = a - hi.astype(F32)
    mid = r1.astype(BF16)
    lo = (r1 - mid.astype(F32)).astype(BF16)
    return hi, mid, lo


def _tri_left(tri, a):
    return sum(jnp.dot(tri, t, preferred_element_type=F32) for t in _split3(a))


def _tri_right(a, tri):
    return sum(jnp.dot(t, tri, preferred_element_type=F32) for t in _split3(a))


def _tri_masks(n):
    r = lax.broadcasted_iota(jnp.int32, (n, n), 0)
    c = lax.broadcasted_iota(jnp.int32, (n, n), 1)
    return r >= c, r <= c


def _mod_kernel(c_ref, w_ref, b_ref, o_ref):
    s = _silu(c_ref[...]).astype(BF16)
    o_ref[...] = jnp.dot(s, w_ref[...].astype(BF16), preferred_element_type=F32) + b_ref[...]


def _modulation(cc, w_mod, b_mod):
    rows, d = cc.shape
    n = w_mod.shape[1]
    tn = 512
    return pl.pallas_call(
        _mod_kernel,
        out_shape=jax.ShapeDtypeStruct((rows, n), F32),
        grid=(n // tn,),
        in_specs=[pl.BlockSpec((rows, d), lambda j: (0, 0)),
                  pl.BlockSpec((d, tn), lambda j: (0, j)),
                  pl.BlockSpec((1, tn), lambda j: (0, j))],
        out_specs=pl.BlockSpec((rows, tn), lambda j: (0, j)),
        compiler_params=_params(1),
        name="mod",
    )(cc, w_mod, b_mod.reshape(1, n))


def _wcast_kernel(wt_ref, o_ref, *, valid):
    w = wt_ref[...]
    if valid < w.shape[0]:
        w = jnp.where(lax.broadcasted_iota(jnp.int32, w.shape, 0) < valid, w, 0.0)
    o_ref[...] = w.T.astype(o_ref.dtype)


def _wcast(wt, col0, n_out, tn):
    n, d = wt.shape
    assert col0 % tn == 0 and n_out % tn == 0
    return pl.pallas_call(
        functools.partial(_wcast_kernel, valid=min(tn, n - col0)),
        out_shape=jax.ShapeDtypeStruct((d, n_out), BF16),
        grid=(n_out // tn,),
        in_specs=[pl.BlockSpec((tn, d), lambda j: (col0 // tn + j, 0))],
        out_specs=pl.BlockSpec((d, tn), lambda j: (0, j)),
        compiler_params=_params(1),
        name="wcast",
    )(wt)


def _ln_modulate(xv, shift, scale):
    mu = jnp.mean(xv, axis=-1, keepdims=True)
    xc = xv - mu
    var = jnp.mean(xc * xc, axis=-1, keepdims=True)
    return (xc * lax.rsqrt(var + LN_EPS) * (1.0 + scale) + shift).astype(BF16)


def _ln_kernel(x_ref, sh_ref, sc_ref, h_ref):
    h_ref[...] = _ln_modulate(x_ref[...], sh_ref[...], sc_ref[...])


def _inproj_kernel(xn_ref, h0_ref, sh_ref, sc_ref, w_ref, wg_ref, u_ref, g_ref, h_a, h_b, *, nsub):
    i = pl.program_id(0)
    j = pl.program_id(1)
    rs = xn_ref.shape[0]

    @pl.when((i == 0) & (j == 0))
    def _():
        h_a[...] = h0_ref[...]

    def body(h_cur, h_nxt):
        @pl.when(j == 0)
        def _():
            g_ref[...] = jnp.dot(h_cur[...], wg_ref[...], preferred_element_type=F32)

        rows = pl.ds(pl.multiple_of(jnp.minimum(j, nsub - 1) * rs, rs), rs)
        h_nxt[rows, :] = _ln_modulate(xn_ref[...], sh_ref[...], sc_ref[...])
        u_ref[...] = jnp.dot(h_cur[...], w_ref[...], preferred_element_type=F32).astype(BF16)

    pl.when(i % 2 == 0)(lambda: body(h_a, h_b))
    pl.when(i % 2 == 1)(lambda: body(h_b, h_a))


def _inproj(x2, mod3, mod_row, w_main, w_g, tm):
    rows, d = x2.shape
    nu = w_main.shape[1]
    tn = 1024 if nu % 1024 == 0 else 512
    nt, nj = rows // tm, nu // tn
    nsub = min(8, 1 << (nj.bit_length() - 1))
    rs = tm // nsub
    assert nu % tn == 0 and rows % tm == 0 and tm % nsub == 0

    def mod_spec(part, row_of):
        return pl.BlockSpec((None, 1, d), lambda *ij: (row_of(ij[0]), 0, part))

    h0 = pl.pallas_call(
        _ln_kernel,
        out_shape=jax.ShapeDtypeStruct((tm, d), BF16),
        grid=(2,),
        in_specs=[pl.BlockSpec((tm // 2, d), lambda r: (r, 0)),
                  mod_spec(0, lambda r: mod_row(0)), mod_spec(1, lambda r: mod_row(0))],
        out_specs=pl.BlockSpec((tm // 2, d), lambda r: (r, 0)),
        compiler_params=_params(1),
        name="ln0",
    )(x2, mod3, mod3)

    def nxt(i):
        return jnp.minimum(i + 1, nt - 1)

    return pl.pallas_call(
        functools.partial(_inproj_kernel, nsub=nsub),
        out_shape=(jax.ShapeDtypeStruct((rows, nu), BF16),
                   jax.ShapeDtypeStruct((rows, LANES), F32)),
        grid=(nt, nj),
        in_specs=[pl.BlockSpec((rs, d), lambda i, j: (nxt(i) * nsub + jnp.minimum(j, nsub - 1), 0)),
                  pl.BlockSpec((tm, d), lambda i, j: (0, 0)),
                  mod_spec(0, lambda i: mod_row(nxt(i))), mod_spec(1, lambda i: mod_row(nxt(i))),
                  pl.BlockSpec((d, tn), lambda i, j: (0, j)),
                  pl.BlockSpec((d, LANES), lambda i, j: (0, 0))],
        out_specs=(pl.BlockSpec((tm, tn), lambda i, j: (i, j)),
                   pl.BlockSpec((tm, LANES), lambda i, j: (i, 0))),
        scratch_shapes=[pltpu.VMEM((tm, d), BF16), pltpu.VMEM((tm, d), BF16)],
        compiler_params=_params(2),
        name="inproj",
    )(x2, h0, mod3, mod3, w_main, w_g)


def _gateprep_kernel(g_ref, bias_ref, gc_ref, gr_ref, *, nh, c):
    lower, upper = _tri_masks(c)
    tril = jnp.where(lower, 1.0, 0.0).astype(BF16)
    triu = jnp.where(upper, 1.0, 0.0).astype(BF16)
    lane = lax.broadcasted_iota(jnp.int32, (c, LANES), 1)
    fwd_f = (lane >= 2 * nh) & (lane < 3 * nh)
    bwd_f = (lane >= 3 * nh) & (lane < 4 * nh)
    row = lax.broadcasted_iota(jnp.int32, (GATE_ROWS, c), 0)
    pos = lax.broadcasted_iota(jnp.int32, (GATE_ROWS, c), 1)
    fwd_r = (row >= 2 * nh) & (row < 3 * nh)
    bwd_r = (row >= 3 * nh) & (row < 4 * nh)

    for r0 in range(0, g_ref.shape[0], c):
        pre = g_ref[r0:r0 + c, :] + bias_ref[...]
        lf = _log_sigmoid(pre)
        b = jnp.where(fwd_f, _tri_left(tril, lf), jnp.where(bwd_f, _tri_left(triu, lf), 0.0))
        gc_ref[r0:r0 + c, :] = pre - pltpu.roll(b, LANES - 2 * nh, axis=1)

        pre_t = pre.T[:GATE_ROWS]
        lf_t = _log_sigmoid(pre_t)
        b_t = jnp.where(fwd_r, _tri_right(lf_t, triu), jnp.where(bwd_r, _tri_right(lf_t, tril), 0.0))
        a_t = pre_t - pltpu.roll(b_t, GATE_ROWS - 2 * nh, axis=0)
        pmax, smax = a_t, a_t
        shift = 1
        while shift < c:
            pmax = jnp.maximum(pmax, jnp.where(pos >= shift, pltpu.roll(pmax, shift, axis=1), -jnp.inf))
            smax = jnp.maximum(smax, jnp.where(pos < c - shift, pltpu.roll(smax, c - shift, axis=1), -jnp.inf))
            shift *= 2
        amax_t = pltpu.roll(jnp.where(row < nh, pmax, smax), 4 * nh, axis=0)
        gr_ref[:, r0:r0 + c] = jnp.where(row < 2 * nh, a_t, jnp.where(row < 4 * nh, b_t, amax_t))


def _gateprep(g, bias, nh, chunk):
    rows = g.shape[0]
    assert 6 * nh <= GATE_ROWS
    blk = 4 * chunk if rows % (4 * chunk) == 0 else chunk
    bias_row = jnp.zeros((1, LANES), F32).at[0, :4 * nh].set(bias.reshape(-1))
    return pl.pallas_call(
        functools.partial(_gateprep_kernel, nh=nh, c=chunk),
        out_shape=(jax.ShapeDtypeStruct((rows, LANES), F32),
                   jax.ShapeDtypeStruct((GATE_ROWS, rows), F32)),
        grid=(rows // blk,),
        in_specs=[pl.BlockSpec((blk, LANES), lambda i: (i, 0)),
                  pl.BlockSpec((1, LANES), lambda i: (0, 0))],
        out_specs=(pl.BlockSpec((blk, LANES), lambda i: (i, 0)),
                   pl.BlockSpec((GATE_ROWS, blk), lambda i: (0, i))),
        compiler_params=_params(1),
        name="gateprep",
    )(g, bias_row)


def _conv_kernel(x_ref, w_ref, b_ref, o_ref, *, rows, width, taps):
    cb = x_ref.shape[-1]
    j = pl.program_id(1)
    scale = jnp.where(j >= pl.num_programs(1) // 2, ML_DH ** -0.5, 1.0).astype(F32)
    w = w_ref[...]
    bias = b_ref[...]
    col = lax.broadcasted_iota(jnp.int32, (width, cb), 0)
    first = col == 0
    last = col == width - 1

    def one_row(r):
        left = mid = right = None
        for ki in taps:
            rr = r + (ki - 1)
            valid = jnp.where((rr >= 0) & (rr < rows), 1.0, 0.0).astype(F32)
            wk = w[3 * ki:3 * ki + 3, :] * valid
            start = pl.multiple_of(jnp.clip(rr, 0, rows - 1) * width, width)
            xv = x_ref[pl.ds(start, width), :].astype(F32)
            terms = [xv * wk[kj:kj + 1, :] for kj in range(3)]
            left, mid, right = terms if left is None else (left + terms[0], mid + terms[1], right + terms[2])
        out = (mid + jnp.where(first, 0.0, pltpu.roll(left, 1, axis=0))
               + jnp.where(last, 0.0, pltpu.roll(right, width - 1, axis=0)) + bias)
        dst = pl.multiple_of(r * width, width)
        o_ref[pl.ds(dst, width), :] = (_silu(out) * scale).astype(o_ref.dtype)

    def body(r, carry):
        one_row(r)
        return carry

    lax.fori_loop(0, rows, body, 0)


def _short_conv(u, col0, n_ch, conv_w, conv_b, grid):
    bsz, t, _ = u.shape
    cb = COL_BLOCK
    rows, width, taps = (t // GRID_W, GRID_W, (0, 1, 2)) if grid else (1, t, (1,))
    return pl.pallas_call(
        functools.partial(_conv_kernel, rows=rows, width=width, taps=taps),
        out_shape=jax.ShapeDtypeStruct((bsz, t, n_ch), BF16),
        grid=(bsz, n_ch // cb),
        in_specs=[pl.BlockSpec((None, t, cb), lambda b, j: (b, 0, col0 // cb + j)),
                  pl.BlockSpec((CONV_K * CONV_K, cb), lambda b, j: (0, j)),
                  pl.BlockSpec((1, cb), lambda b, j: (0, j))],
        out_specs=pl.BlockSpec((None, t, cb), lambda b, j: (b, 0, j)),
        compiler_params=_params(2),
        name="conv_grid" if grid else "conv_seq",
    )(u, conv_w.reshape(CONV_K * CONV_K, n_ch).astype(F32), conv_b.reshape(1, n_ch).astype(F32))


def _hg_prepare(qraw, z, lb, fwd):
    lower, upper = _tri_masks(z.shape[0])
    tri = jnp.where(lower if fwd else upper, 1.0, 0.0).astype(BF16)
    f = lb + (1.0 - lb) * jax.nn.sigmoid(z)
    b2 = _tri_left(tri, jnp.log(f) * LOG2E)
    c2 = b2 - jnp.log(1.0 - f) * LOG2E
    return _silu(qraw), b2, c2


def _block_diag(a):
    r, w = a.shape
    zero = jnp.zeros((r, w // 2), a.dtype)
    return jnp.concatenate([jnp.concatenate([a[:, :w // 2], zero], axis=1),
                            jnp.concatenate([zero, a[:, w // 2:]], axis=1)], axis=0)


def _hg_scores(q_ref, b_ref, c_ref, fwd):
    c, w = q_ref.shape
    sub = HG_SUB
    nb = c // sub
    assert 2 * c == LANES and c & (c - 1) == 0
    row = lax.broadcasted_iota(jnp.int32, (sub, LANES), 0)
    col = lax.broadcasted_iota(jnp.int32, (sub, LANES), 1) & (c - 1)
    src = col & (sub - 1)
    code = jnp.where((src <= row) if fwd else (src >= row), col, -1)
    kr = lax.broadcasted_iota(jnp.int32, (w, LANES), 0)
    kl = lax.broadcasted_iota(jnp.int32, (w, LANES), 1)
    sel = jnp.where((kr >= w // 2) == (kl >= c), 1.0, 0.0).astype(BF16)

    units = []
    for j in range(nb):
        r0 = j * sub
        qb = q_ref[r0:r0 + sub, :]
        bb = b_ref[r0:r0 + sub, :]
        cb = c_ref[r0:r0 + sub, :]
        for s in range(sub):
            units.append(qb * jnp.exp2(bb - cb[s:s + 1, :]))
    red = jnp.dot(jnp.concatenate(units, axis=0).astype(BF16), sel, preferred_element_type=F32)

    blocks = []
    for j in range(nb):
        r0 = j * sub
        blk = jnp.zeros((sub, LANES), F32)
        for s in range(sub):
            u0 = (j * sub + s) * sub
            blk = jnp.where(code == r0 + s, red[u0:u0 + sub], blk)
        lo, hi = (0, r0) if fwd else (r0 + sub, c)
        if hi > lo:
            beta = b_ref[pl.ds(r0 - 1 if fwd else r0 + sub, 1), :]
            pieces = [jnp.zeros((lo, w), F32)] if lo else []
            pieces.append(jnp.exp2(beta - c_ref[lo:hi, :]))
            if hi < c:
                pieces.append(jnp.zeros((c - hi, w), F32))
            kh = jnp.concatenate(pieces, axis=0).astype(BF16)
            qh = q_ref[r0:r0 + sub, :] * jnp.exp2(b_ref[r0:r0 + sub, :] - beta)
            off = lax.dot_general(_block_diag(qh).astype(BF16), kh, _NT, preferred_element_type=F32)
            blk = blk + jnp.concatenate([off[:sub], off[sub:]], axis=1)
        blocks.append(blk)
    return jnp.concatenate(blocks, axis=0).astype(BF16)


def _hg_scores_factored(q_ref, b_ref, c_ref, fwd, sub):
    c, w = q_ref.shape
    nb = c // sub
    row = lax.broadcasted_iota(jnp.int32, (sub, LANES), 0)
    col = lax.broadcasted_iota(jnp.int32, (sub, LANES), 1) & (c - 1)
    blocks = []
    for j in range(nb):
        r0 = j * sub
        lo, hi = (0, r0 + sub) if fwd else (r0, c)
        edge = r0 - 1 if fwd else r0 + sub
        beta = b_ref[pl.ds(edge, 1), :] if 0 <= edge < c else jnp.zeros((1, w), F32)
        pieces = [jnp.zeros((lo, w), F32)] if lo else []
        pieces.append(jnp.exp2(beta - c_ref[lo:hi, :]))
        if hi < c:
            pieces.append(jnp.zeros((c - hi, w), F32))
        kh = jnp.concatenate(pieces, axis=0).astype(BF16)
        qh = q_ref[r0:r0 + sub, :] * jnp.exp2(b_ref[r0:r0 + sub, :] - beta)
        off = lax.dot_general(_block_diag(qh).astype(BF16), kh, _NT, preferred_element_type=F32)
        blk = jnp.concatenate([off[:sub], off[sub:]], axis=1)
        causal = (col <= row + r0) if fwd else (col >= row + r0)
        blocks.append(jnp.where(causal, blk, 0.0))
    return jnp.concatenate(blocks, axis=0).astype(BF16)


def _hg_state_update(st_refs, v, k_dec, dec, qe=None):
    hw = v.shape[1] // 2
    outs = []
    for h, st_ref in enumerate(st_refs):
        ls = slice(h * hw, (h + 1) * hw)
        st = st_ref[...]
        st_ref[...] = st * dec[:, ls] + lax.dot_general(v[:, ls], k_dec[:, ls], _TN, preferred_element_type=F32)
        if qe is not None:
            outs.append(lax.dot_general(qe[:, ls], st.astype(BF16), _NT, preferred_element_type=F32))
    return outs


def _hgrn2_kernel(cq, cff, cfb, ci, aq, aff, afb, ai, az, lb_ref, nw_ref, y_ref,
                  acc_f, acc_b, st_ref, q_sc, b_sc, c_sc, a_sc, qe_sc, kd_sc, dec_sc, *, slots):
    c = HG_CHUNK
    ncc = cq.shape[0] // c
    nc = aq.shape[0] // c
    hw = HG_DK
    assert nc % 4 == 0
    st_ref[...] = jnp.zeros_like(st_ref)
    accs = (acc_f, acc_b)
    lbs = []
    for d in range(2):
        hg = lb_ref[d * slots:(d + 1) * slots, :]
        e = jnp.exp(hg - jnp.max(hg, axis=0, keepdims=True))
        lbs.append(e[0:1, :] / jnp.sum(e, axis=0, keepdims=True))

    def chunk_rows(d, k, n):
        return pl.ds(pl.multiple_of((k if d == 0 else n - 1 - k) * c, c), c)

    def states(d):
        return st_ref.at[2 * d], st_ref.at[2 * d + 1]

    def ctx_body(k, carry):
        for d in range(2):
            rows = chunk_rows(d, k, ncc)
            _, b2, c2 = _hg_prepare(cq[rows, :].astype(F32), (cff, cfb)[d][rows, :].astype(F32), lbs[d], d == 0)
            end = c - 1 if d == 0 else 0
            b_end = b2[end:end + 1, :]
            _hg_state_update(states(d), ci[rows, :], jnp.exp2(b_end - c2).astype(BF16), jnp.exp2(b_end))
        return carry

    def gates(d, k, slot):
        rows = chunk_rows(d, k, nc)
        q, b2, c2 = _hg_prepare(aq[rows, :].astype(F32), (aff, afb)[d][rows, :].astype(F32), lbs[d], d == 0)
        q_sc[slot, d] = q
        b_sc[slot, d] = b2
        c_sc[slot, d] = c2

    def scores(d, slot, block):
        qr, br, cr = q_sc.at[slot, d], b_sc.at[slot, d], c_sc.at[slot, d]
        end = c - 1 if d == 0 else 0
        b_end = br[end:end + 1, :]
        qe_sc[slot, d] = (qr[...] * jnp.exp2(br[...])).astype(BF16)
        kd_sc[slot, d] = jnp.exp2(b_end - cr[...]).astype(BF16)
        dec_sc[slot, d] = jnp.exp2(b_end)
        if block is None:
            a_sc[slot, d] = _hg_scores(qr, br, cr, d == 0)
        else:
            a_sc[slot, d] = _hg_scores_factored(qr, br, cr, d == 0, block)

    def outputs(d, k, slot):
        rows = chunk_rows(d, k, nc)
        v = ai[rows, :]
        outs = _hg_state_update(states(d), v, kd_sc[slot, d], dec_sc[slot, d], qe_sc[slot, d])
        accs[d][rows, :] = (jnp.concatenate(outs, axis=1)
                            + jnp.dot(a_sc[slot, d], _block_diag(v), preferred_element_type=F32))

    def step(k, slot, block, do_out=True, do_gates=True, do_scores=True):
        for d in range(2):
            if do_out:
                outputs(d, k - 2, slot)
            if do_gates:
                gates(d, k, slot)
        for d in range(2):
            if do_scores:
                scores(d, 1 - slot, block)

    def scan(block):
        def quad_body(kq, carry):
            for i in range(4):
                step(4 * kq + i, i % 2, block)
            return carry

        step(0, 0, block, do_out=False, do_scores=False)
        step(1, 1, block, do_out=False)
        step(2, 0, block)
        step(3, 1, block)
        lax.fori_loop(1, nc // 4, quad_body, 0)
        step(nc, 0, block, do_gates=False)
        step(nc + 1, 1, block, do_gates=False, do_scores=False)

    def final_body(i, carry):
        rows = pl.ds(pl.multiple_of(i * c, c), c)
        tot = acc_f[rows, :] + acc_b[rows, :]
        gate = _silu(az[rows, :].astype(F32)) * nw_ref[...]
        for h in range(2):
            ls = slice(h * hw, (h + 1) * hw)
            th = tot[:, ls]
            ms = jnp.mean(th * th, axis=-1, keepdims=True)
            y_ref[rows, ls] = (th * lax.rsqrt(ms + NORM_EPS) * gate[:, ls]).astype(y_ref.dtype)
        return carry

    lax.fori_loop(0, ncc, ctx_body, 0, unroll=True)
    unit = jnp.max(-jnp.log(jnp.minimum(lbs[0], lbs[1])) * LOG2E)
    wide = c * unit <= HG_SAFE_LOG2
    narrow = jnp.logical_and(jnp.logical_not(wide), HG_SUB * unit <= HG_SAFE_LOG2)
    pl.when(wide)(lambda: scan(c))
    pl.when(narrow)(lambda: scan(HG_SUB))
    pl.when(jnp.logical_not(jnp.logical_or(wide, narrow)))(lambda: scan(None))
    lax.fori_loop(0, nc, final_body, 0, unroll=4)


def _hgrn2(u_x, u_c, w_a, hg_lb, hg_norm):
    bsz, t, _ = u_x.shape
    tc = u_c.shape[1]
    cb = 2 * HG_DK
    c = HG_CHUNK
    assert cb == COL_BLOCK
    seg = w_a // cb
    slots = hg_lb.shape[1]
    lb2 = hg_lb.reshape(2 * slots, w_a).astype(F32)

    def col(k):
        return lambda b, p: (b, 0, k * seg + p)

    ctx_specs = [pl.BlockSpec((None, tc, cb), col(k)) for k in (0, 1, 2, 3)]
    lat_specs = [pl.BlockSpec((None, t, cb), col(k)) for k in (0, 1, 2, 3, 4)]
    return pl.pallas_call(
        functools.partial(_hgrn2_kernel, slots=slots),
        out_shape=jax.ShapeDtypeStruct((bsz, t, w_a), BF16),
        grid=(bsz, seg),
        in_specs=ctx_specs + lat_specs + [
            pl.BlockSpec((2 * slots, cb), lambda b, p: (0, p)),
            pl.BlockSpec((1, cb), lambda b, p: (0, p))],
        out_specs=pl.BlockSpec((None, t, cb), lambda b, p: (b, 0, p)),
        scratch_shapes=[pltpu.VMEM((t, cb), F32),
                        pltpu.VMEM((t, cb), F32),
                        pltpu.VMEM((4, HG_DK, HG_DK), F32),
                        pltpu.VMEM((2, 2, c, cb), F32),
                        pltpu.VMEM((2, 2, c, cb), F32),
                        pltpu.VMEM((2, 2, c, cb), F32),
                        pltpu.VMEM((2, 2, c, 2 * c), BF16),
                        pltpu.VMEM((2, 2, c, cb), BF16),
                        pltpu.VMEM((2, 2, c, cb), BF16),
                        pltpu.VMEM((2, 2, 1, cb), F32)],
        compiler_params=_params(2),
        name="hgrn2",
    )(u_c, u_c, u_c, u_c, u_x, u_x, u_x, u_x, u_x, lb2, hg_norm.reshape(1, w_a).astype(F32))


def _ml_gates(a_row, b_row, amax_row, m_prev, fwd):
    end = a_row.shape[1] - 1 if fwd else 0
    b_end = b_row[:, end:end + 1]
    big_m = jnp.maximum(m_prev, amax_row)
    m_row = b_row + big_m
    m_new = m_row[:, end:end + 1]
    w_s = jnp.exp(b_end + a_row - m_new)
    decay = jnp.exp(b_end + m_prev - m_new)
    return big_m, m_row, m_new, w_s, decay


def _ml_state_lhs(v, w_s):
    vt = v.astype(F32).T
    lhs = jnp.concatenate([vt * w_s, jnp.broadcast_to(w_s, (ML_PAD, vt.shape[1]))], axis=0)
    return vt, lhs.astype(BF16)


def _mlstm_kernel(cq, ck, cv, cgc, cgr, xq, xk, xv, xo, xz, xgc, xgr, nw_ref, y_ref,
                  acc_f, acc_b, st_ref, wt_sc, vt_sc, lhs_sc, row_sc):
    c = ML_CHUNK
    ncc = cq.shape[0] // c
    nc = xq.shape[0] // c
    dh = xq.shape[1]
    assert nc % 2 == 0
    head = pl.program_id(1)
    nh = pl.num_programs(1)
    accs = (acc_f, acc_b)
    st_ref[...] = jnp.zeros_like(st_ref)
    lane = lax.broadcasted_iota(jnp.int32, (c, LANES), 1)
    lower, upper = _tri_masks(c)

    def chunk_rows(d, k, n):
        return pl.ds(pl.multiple_of((k if d == 0 else n - 1 - k) * c, c), c)

    def gate_rows(grr, d, rows):
        slot = d * nh + head
        return (grr[pl.ds(slot, 1), rows], grr[pl.ds(2 * nh + slot, 1), rows],
                grr[pl.ds(4 * nh + slot, 1), rows])

    def ctx_body(k, ms):
        out = []
        for d in range(2):
            rows = chunk_rows(d, k, ncc)
            _, _, m_new, w_s, decay = _ml_gates(*gate_rows(cgr, d, rows), ms[d], d == 0)
            _, lhs = _ml_state_lhs(cv[rows, :], w_s)
            st_ref[d] = decay * st_ref[d] + jnp.dot(lhs, ck[rows, :], preferred_element_type=F32)
            out.append(m_new)
        return tuple(out)

    def score_stage(d, k, slot, m_prev):
        rows = chunk_rows(d, k, nc)
        big_m, m_row, m_new, w_s, decay = _ml_gates(*gate_rows(xgr, d, rows), m_prev, d == 0)
        vt, lhs = _ml_state_lhs(xv[rows, :], w_s)
        lhs_sc[slot, d] = lhs
        vt_sc[slot, d] = vt.astype(BF16)
        a_col = jnp.sum(jnp.where(lane == d * nh + head, xgc[rows, :], 0.0), axis=1, keepdims=True)
        mask = upper if d == 0 else lower
        w_t = (jnp.exp(jnp.where(mask, a_col - big_m, -jnp.inf))
               * lax.dot_general(xk[rows, :], xq[rows, :], _NT, preferred_element_type=F32))
        wt_sc[slot, d] = w_t
        row_sc[slot, d, 0:1, :] = jnp.sum(w_t, axis=0, keepdims=True)
        row_sc[slot, d, 1:2, :] = jnp.exp(m_prev - big_m)
        row_sc[slot, d, 2:3, :] = jnp.exp(-m_row)
        row_sc[slot, d, 3:4, :] = jnp.broadcast_to(decay, (1, c))
        return m_new

    def output_stage(d, k, slot):
        rows = chunk_rows(d, k, nc)
        st = st_ref[d]
        r = lax.dot_general(st.astype(BF16), xq[rows, :], _NT, preferred_element_type=F32)
        w_inter = row_sc[slot, d, 1:2, :]
        den = w_inter * r[dh:dh + 1, :] + row_sc[slot, d, 0:1, :]
        inv = 1.0 / jnp.maximum(jnp.abs(den), row_sc[slot, d, 2:3, :])
        accs[d][:, rows] = (r[:dh, :] * (w_inter * inv)
                            + jnp.dot(vt_sc[slot, d], (wt_sc[slot, d] * inv).astype(BF16),
                                      preferred_element_type=F32))
        st_ref[d] = (row_sc[slot, d, 3:4, 0:1] * st
                     + jnp.dot(lhs_sc[slot, d], xk[rows, :], preferred_element_type=F32))

    def pair_body(kk, ms):
        for slot in range(2):
            k = 2 * kk + slot
            nxt = jnp.minimum(k + 1, nc - 1)
            for d in range(2):
                output_stage(d, k, slot)
            ms = tuple(score_stage(d, nxt, 1 - slot, ms[d]) for d in range(2))
        return ms

    def final_body(i, carry):
        rows = pl.ds(pl.multiple_of(i * c, c), c)
        tot = acc_f[:, rows] + acc_b[:, rows]
        mu = jnp.mean(tot, axis=0, keepdims=True)
        tc_ = tot - mu
        var = jnp.mean(tc_ * tc_, axis=0, keepdims=True)
        normed = (tc_ * lax.rsqrt(var + NORM_EPS)).T * nw_ref[...]
        gate = jax.nn.sigmoid(xo[rows, :].astype(F32)) * _silu(xz[rows, :].astype(F32))
        y_ref[rows, :] = (normed * gate).astype(y_ref.dtype)
        return carry

    ms = (jnp.zeros((1, 1), F32), jnp.zeros((1, 1), F32))
    ms = lax.fori_loop(0, ncc, ctx_body, ms)
    ms = tuple(score_stage(d, 0, 0, ms[d]) for d in range(2))
    lax.fori_loop(0, nc // 2, pair_body, ms)
    lax.fori_loop(0, nc, final_body, 0, unroll=2)


def _mlstm(qk_x, qk_c, u_x, u_c, v_col, o_col, z_col, gates_x, gates_c, ml_norm, w_b):
    bsz, t, _ = u_x.shape
    tc = u_c.shape[1]
    nh = w_b // ML_DH
    cb = ML_DH
    assert cb == COL_BLOCK and t % (2 * ML_CHUNK) == 0 and tc % ML_CHUNK == 0

    def col(k):
        return lambda b, h: (b, 0, k + h)

    def seq_specs(n, with_gates):
        specs = [pl.BlockSpec((None, n, cb), col(0)),
                 pl.BlockSpec((None, n, cb), col(nh)),
                 pl.BlockSpec((None, n, cb), col(v_col // cb))]
        if with_gates:
            specs += [pl.BlockSpec((None, n, cb), col(o_col // cb)),
                      pl.BlockSpec((None, n, cb), col(z_col // cb))]
        specs += [pl.BlockSpec((n, LANES), lambda b, h: (b, 0)),
                  pl.BlockSpec((GATE_ROWS, n), lambda b, h: (0, b))]
        return specs

    return pl.pallas_call(
        _mlstm_kernel,
        out_shape=jax.ShapeDtypeStruct((bsz, t, w_b), BF16),
        grid=(bsz, nh),
        in_specs=seq_specs(tc, False) + seq_specs(t, True) + [pl.BlockSpec((1, cb), lambda b, h: (0, h))],
        out_specs=pl.BlockSpec((None, t, cb), lambda b, h: (b, 0, h)),
        scratch_shapes=[pltpu.VMEM((cb, t), F32),
                        pltpu.VMEM((cb, t), F32),
                        pltpu.VMEM((2, ML_DH + ML_PAD, ML_DH), F32),
                        pltpu.VMEM((2, 2, ML_CHUNK, ML_CHUNK), F32),
                        pltpu.VMEM((2, 2, ML_DH, ML_CHUNK), BF16),
                        pltpu.VMEM((2, 2, ML_DH + ML_PAD, ML_CHUNK), BF16),
                        pltpu.VMEM((2, 2, 8, ML_CHUNK), F32)],
        compiler_params=_params(2),
        name="mlstm",
    )(qk_c, qk_c, u_c, gates_c[0], gates_c[1],
      qk_x, qk_x, u_x, u_x, u_x, gates_x[0], gates_x[1],
      ml_norm.reshape(1, w_b).astype(F32))


def _outproj_kernel(ya_ref, yb_ref, x_ref, gate_ref, wa_ref, wb_ref, g_ref, b_ref, o_ref, *, sub):
    for r0 in range(0, x_ref.shape[0], sub):
        rows = slice(r0, r0 + sub)
        y = (jnp.dot(ya_ref[rows, :], wa_ref[...], preferred_element_type=F32)
             + jnp.dot(yb_ref[rows, :], wb_ref[...], preferred_element_type=F32))
        r = ALPHA * x_ref[rows, :] + gate_ref[...] * y
        mu = jnp.mean(r, axis=-1, keepdims=True)
        rc = r - mu
        var = jnp.mean(rc * rc, axis=-1, keepdims=True)
        o_ref[rows, :] = rc * lax.rsqrt(var + LN_EPS) * g_ref[...] + b_ref[...]


def _outproj(y_a, y_b, x2, mod3, t, w_oa, w_ob, ln_g, ln_b):
    rows, d = x2.shape
    tm = 512
    wa, wb = y_a.shape[1], y_b.shape[1]
    return pl.pallas_call(
        functools.partial(_outproj_kernel, sub=256),
        out_shape=jax.ShapeDtypeStruct((rows, d), F32),
        grid=(rows // tm,),
        in_specs=[pl.BlockSpec((tm, wa), lambda i: (i, 0)),
                  pl.BlockSpec((tm, wb), lambda i: (i, 0)),
                  pl.BlockSpec((tm, d), lambda i: (i, 0)),
                  pl.BlockSpec((None, 1, d), lambda i: ((i * tm) // t, 0, 2)),
                  pl.BlockSpec((wa, d), lambda i: (0, 0)),
                  pl.BlockSpec((wb, d), lambda i: (0, 0)),
                  pl.BlockSpec((1, d), lambda i: (0, 0)),
                  pl.BlockSpec((1, d), lambda i: (0, 0))],
        out_specs=pl.BlockSpec((tm, d), lambda i: (i, 0)),
        compiler_params=_params(1),
        name="outproj",
    )(y_a, y_b, x2, mod3, w_oa, w_ob, ln_g.reshape(1, d).astype(F32), ln_b.reshape(1, d).astype(F32))


def kernel(x, c, ctx, c_ctx, w_mod, b_mod, w_in, conv_w, conv_b, hg_lb, ml_gate_b, hg_norm_w,
           ml_norm_w, w_out, ln_g, ln_b):
    bsz, t, d = x.shape
    tc = ctx.shape[1]
    assert w_in.shape[0] == DEPTH and hg_lb.shape[1] == DEPTH + 1
    d_inner = w_out.shape[1]
    w_a = hg_lb.shape[2]
    w_b = d_inner - w_a
    nh = ml_gate_b.shape[2]
    nu = 5 * w_a + 5 * w_b
    assert nh * ML_DH == w_b and w_in.shape[2] == nu + 4 * nh and t % GRID_W == 0

    mod_rows = 8 * ((bsz + 1 + 7) // 8)
    cc = jnp.zeros((mod_rows, d), F32).at[:bsz].set(c).at[bsz].set(c_ctx)
    mod3 = _modulation(cc, w_mod[0], b_mod[0]).reshape(mod_rows, 1, 3 * d)

    w_in_t = w_in[0].T
    w_main = _wcast(w_in_t, 0, nu, 512)
    w_g = _wcast(w_in_t, nu, LANES, LANES)
    tm = min(1024, t)
    u_x, g_x = _inproj(x.reshape(bsz * t, d), mod3, lambda i: (i * tm) // t, w_main, w_g, tm)
    u_c, g_c = _inproj(ctx.reshape(bsz * tc, d), mod3, lambda i: bsz, w_main, w_g, min(1024, bsz * tc))
    u_x = u_x.reshape(bsz, t, nu)
    u_c = u_c.reshape(bsz, tc, nu)

    gates_x = _gateprep(g_x, ml_gate_b[0], nh, ML_CHUNK)
    gates_c = _gateprep(g_c, ml_gate_b[0], nh, ML_CHUNK)

    qk_col = 5 * w_a
    qk_x = _short_conv(u_x, qk_col, 2 * w_b, conv_w[0], conv_b[0], grid=True)
    qk_c = _short_conv(u_c, qk_col, 2 * w_b, conv_w[0], conv_b[0], grid=False)

    y_a = _hgrn2(u_x, u_c, w_a, hg_lb, hg_norm_w[0])
    y_b = _mlstm(qk_x, qk_c, u_x, u_c, qk_col + 2 * w_b, qk_col + 3 * w_b, qk_col + 4 * w_b,
                 gates_x, gates_c, ml_norm_w[0], w_b)

    w_o = w_out[0].astype(BF16)
    out = _outproj(y_a.reshape(bsz * t, w_a), y_b.reshape(bsz * t, w_b), x.reshape(bsz * t, d), mod3, t,
                   w_o[:w_a], w_o[w_a:], ln_g[0], ln_b[0])
    return out.reshape(bsz, t, d)
```

```python
import functools

import jax
import jax.numpy as jnp
from jax import lax
from jax.experimental import pallas as pl
from jax.experimental.pallas import tpu as pltpu

F32 = jnp.float32
BF16 = jnp.bfloat16

LN_EPS = 1e-5
NORM_EPS = 1e-6
DEPTH = 1
ALPHA = (2 * DEPTH) ** 0.25

HG_DK = 128
ML_DH = 256
GRID_W = 64
CONV_K = 3

LANES = 128
HG_CHUNK = 64
HG_SUB = 8
HG_SAFE_LOG2 = 96.0
LOG2E = 1.4426950408889634
ML_CHUNK = 256
ML_PAD = 16
GATE_ROWS = 24
COL_BLOCK = 256
VMEM_LIMIT = 56 * 1024 * 1024

_NT = (((1,), (1,)), ((), ()))
_TN = (((0,), (0,)), ((), ()))


def _params(n_grid):
    return pltpu.CompilerParams(dimension_semantics=("arbitrary",) * n_grid,
                                vmem_limit_bytes=VMEM_LIMIT)


def _silu(a):
    return a * jax.nn.sigmoid(a)


def _log_sigmoid(a):
    return jnp.minimum(a, 0.0) - jnp.log1p(jnp.exp(-jnp.abs(a)))


def _split3(a):
    hi = a.astype(BF16)
    r1 = a - hi.astype(F32)
    mid = r1.astype(BF16)
    lo = (r1 - mid.astype(F32)).astype(BF16)
    return hi, mid, lo


def _tri_left(tri, a):
    return sum(jnp.dot(tri, t, preferred_element_type=F32) for t in _split3(a))


def _tri_right(a, tri):
    return sum(jnp.dot(t, tri, preferred_element_type=F32) for t in _split3(a))


def _tri_masks(n):
    r = lax.broadcasted_iota(jnp.int32, (n, n), 0)
    c = lax.broadcasted_iota(jnp.int32, (n, n), 1)
    return r >= c, r <= c


def _mod_kernel(c_ref, w_ref, b_ref, o_ref):
    s = _silu(c_ref[...]).astype(BF16)
    o_ref[...] = jnp.dot(s, w_ref[...].astype(BF16), preferred_element_type=F32) + b_ref[...]


def _modulation(cc, w_mod, b_mod):
    rows, d = cc.shape
    n = w_mod.shape[1]
    tn = 512
    return pl.pallas_call(
        _mod_kernel,
        out_shape=jax.ShapeDtypeStruct((rows, n), F32),
        grid=(n // tn,),
        in_specs=[pl.BlockSpec((rows, d), lambda j: (0, 0)),
                  pl.BlockSpec((d, tn), lambda j: (0, j)),
                  pl.BlockSpec((1, tn), lambda j: (0, j))],
        out_specs=pl.BlockSpec((rows, tn), lambda j: (0, j)),
        compiler_params=_params(1),
        name="mod",
    )(cc, w_mod, b_mod.reshape(1, n))


def _wcast_kernel(wt_ref, o_ref, *, valid):
    w = wt_ref[...]
    if valid < w.shape[0]:
        w = jnp.where(lax.broadcasted_iota(jnp.int32, w.shape, 0) < valid, w, 0.0)
    o_ref[...] = w.T.astype(o_ref.dtype)


def _wcast(wt, col0, n_out, tn):
    n, d = wt.shape
    assert col0 % tn == 0 and n_out % tn == 0
    return pl.pallas_call(
        functools.partial(_wcast_kernel, valid=min(tn, n - col0)),
        out_shape=jax.ShapeDtypeStruct((d, n_out), BF16),
        grid=(n_out // tn,),
        in_specs=[pl.BlockSpec((tn, d), lambda j: (col0 // tn + j, 0))],
        out_specs=pl.BlockSpec((d, tn), lambda j: (0, j)),
        compiler_params=_params(1),
        name="wcast",
    )(wt)


def _ln_modulate(xv, shift, scale):
    mu = jnp.mean(xv, axis=-1, keepdims=True)
    xc = xv - mu
    var = jnp.mean(xc * xc, axis=-1, keepdims=True)
    return (xc * lax.rsqrt(var + LN_EPS) * (1.0 + scale) + shift).astype(BF16)


def _ln_kernel(x_ref, sh_ref, sc_ref, h_ref):
    h_ref[...] = _ln_modulate(x_ref[...], sh_ref[...], sc_ref[...])


def _inproj_kernel(xn_ref, h0_ref, sh_ref, sc_ref, w_ref, wg_ref, u_ref, g_ref, h_a, h_b, *, nsub):
    i = pl.program_id(0)
    j = pl.program_id(1)
    rs = xn_ref.shape[0]

    @pl.when((i == 0) & (j == 0))
    def _():
        h_a[...] = h0_ref[...]

    def body(h_cur, h_nxt):
        @pl.when(j == 0)
        def _():
            g_ref[...] = jnp.dot(h_cur[...], wg_ref[...], preferred_element_type=F32)

        rows = pl.ds(pl.multiple_of(jnp.minimum(j, nsub - 1) * rs, rs), rs)
        h_nxt[rows, :] = _ln_modulate(xn_ref[...], sh_ref[...], sc_ref[...])
        u_ref[...] = jnp.dot(h_cur[...], w_ref[...], preferred_element_type=F32).astype(BF16)

    pl.when(i % 2 == 0)(lambda: body(h_a, h_b))
    pl.when(i % 2 == 1)(lambda: body(h_b, h_a))


def _inproj(x2, mod3, mod_row, w_main, w_g, tm):
    rows, d = x2.shape
    nu = w_main.shape[1]
    tn = 1024 if nu % 1024 == 0 else 512
    nt, nj = rows // tm, nu // tn
    nsub = min(8, 1 << (nj.bit_length() - 1))
    rs = tm // nsub
    assert nu % tn == 0 and rows % tm == 0 and tm % nsub == 0

    def mod_spec(part, row_of):
        return pl.BlockSpec((None, 1, d), lambda *ij: (row_of(ij[0]), 0, part))

    h0 = pl.pallas_call(
        _ln_kernel,
        out_shape=jax.ShapeDtypeStruct((tm, d), BF16),
        grid=(2,),
        in_specs=[pl.BlockSpec((tm // 2, d), lambda r: (r, 0)),
                  mod_spec(0, lambda r: mod_row(0)), mod_spec(1, lambda r: mod_row(0))],
        out_specs=pl.BlockSpec((tm // 2, d), lambda r: (r, 0)),
        compiler_params=_params(1),
        name="ln0",
    )(x2, mod3, mod3)

    def nxt(i):
        return jnp.minimum(i + 1, nt - 1)

    return pl.pallas_call(
        functools.partial(_inproj_kernel, nsub=nsub),
        out_shape=(jax.ShapeDtypeStruct((rows, nu), BF16),
                   jax.ShapeDtypeStruct((rows, LANES), F32)),
        grid=(nt, nj),
        in_specs=[pl.BlockSpec((rs, d), lambda i, j: (nxt(i) * nsub + jnp.minimum(j, nsub - 1), 0)),
                  pl.BlockSpec((tm, d), lambda i, j: (0, 0)),
                  mod_spec(0, lambda i: mod_row(nxt(i))), mod_spec(1, lambda i: mod_row(nxt(i))),
                  pl.BlockSpec((d, tn), lambda i, j: (0, j)),
                  pl.BlockSpec((d, LANES), lambda i, j: (0, 0))],
        out_specs=(pl.BlockSpec((tm, tn), lambda i, j: (i, j)),
                   pl.BlockSpec((tm, LANES), lambda i, j: (i, 0))),
        scratch_shapes=[pltpu.VMEM((tm, d), BF16), pltpu.VMEM((tm, d), BF16)],
        compiler_params=_params(2),
        name="inproj",
    )(x2, h0, mod3, mod3, w_main, w_g)


def _gateprep_kernel(g_ref, bias_ref, gc_ref, gr_ref, *, nh, c):
    lower, upper = _tri_masks(c)
    tril = jnp.where(lower, 1.0, 0.0).astype(BF16)
    triu = jnp.where(upper, 1.0, 0.0).astype(BF16)
    lane = lax.broadcasted_iota(jnp.int32, (c, LANES), 1)
    fwd_f = (lane >= 2 * nh) & (lane < 3 * nh)
    bwd_f = (lane >= 3 * nh) & (lane < 4 * nh)
    row = lax.broadcasted_iota(jnp.int32, (GATE_ROWS, c), 0)
    pos = lax.broadcasted_iota(jnp.int32, (GATE_ROWS, c), 1)
    fwd_r = (row >= 2 * nh) & (row < 3 * nh)
    bwd_r = (row >= 3 * nh) & (row < 4 * nh)

    for r0 in range(0, g_ref.shape[0], c):
        pre = g_ref[r0:r0 + c, :] + bias_ref[...]
        lf = _log_sigmoid(pre)
        b = jnp.where(fwd_f, _tri_left(tril, lf), jnp.where(bwd_f, _tri_left(triu, lf), 0.0))
        gc_ref[r0:r0 + c, :] = pre - pltpu.roll(b, LANES - 2 * nh, axis=1)

        pre_t = pre.T[:GATE_ROWS]
        lf_t = _log_sigmoid(pre_t)
        b_t = jnp.where(fwd_r, _tri_right(lf_t, triu), jnp.where(bwd_r, _tri_right(lf_t, tril), 0.0))
        a_t = pre_t - pltpu.roll(b_t, GATE_ROWS - 2 * nh, axis=0)
        pmax, smax = a_t, a_t
        shift = 1
        while shift < c:
            pmax = jnp.maximum(pmax, jnp.where(pos >= shift, pltpu.roll(pmax, shift, axis=1), -jnp.inf))
            smax = jnp.maximum(smax, jnp.where(pos < c - shift, pltpu.roll(smax, c - shift, axis=1), -jnp.inf))
            shift *= 2
        amax_t = pltpu.roll(jnp.where(row < nh, pmax, smax), 4 * nh, axis=0)
        gr_ref[:, r0:r0 + c] = jnp.where(row < 2 * nh, a_t, jnp.where(row < 4 * nh, b_t, amax_t))


def _gateprep(g, bias, nh, chunk):
    rows = g.shape[0]
    assert 6 * nh <= GATE_ROWS
    blk = 4 * chunk if rows % (4 * chunk) == 0 else chunk
    bias_row = jnp.zeros((1, LANES), F32).at[0, :4 * nh].set(bias.reshape(-1))
    return pl.pallas_call(
        functools.partial(_gateprep_kernel, nh=nh, c=chunk),
        out_shape=(jax.ShapeDtypeStruct((rows, LANES), F32),
                   jax.ShapeDtypeStruct((GATE_ROWS, rows), F32)),
        grid=(rows // blk,),
        in_specs=[pl.BlockSpec((blk, LANES), lambda i: (i, 0)),
                  pl.BlockSpec((1, LANES), lambda i: (0, 0))],
        out_specs=(pl.BlockSpec((blk, LANES), lambda i: (i, 0)),
                   pl.BlockSpec((GATE_ROWS, blk), lambda i: (0, i))),
        compiler_params=_params(1),
        name="gateprep",
    )(g, bias_row)


def _conv_kernel(x_ref, w_ref, b_ref, o_ref, *, rows, width, taps):
    cb = x_ref.shape[-1]
    j = pl.program_id(1)
    scale = jnp.where(j >= pl.num_programs(1) // 2, ML_DH ** -0.5, 1.0).astype(F32)
    w = w_ref[...]
    bias = b_ref[...]
    col = lax.broadcasted_iota(jnp.int32, (width, cb), 0)
    first = col == 0
    last = col == width - 1

    def one_row(r):
        left = mid = right = None
        for ki in taps:
            rr = r + (ki - 1)
            valid = jnp.where((rr >= 0) & (rr < rows), 1.0, 0.0).astype(F32)
            wk = w[3 * ki:3 * ki + 3, :] * valid
            start = pl.multiple_of(jnp.clip(rr, 0, rows - 1) * width, width)
            xv = x_ref[pl.ds(start, width), :].astype(F32)
            terms = [xv * wk[kj:kj + 1, :] for kj in range(3)]
            left, mid, right = terms if left is None else (left + terms[0], mid + terms[1], right + terms[2])
        out = (mid + jnp.where(first, 0.0, pltpu.roll(left, 1, axis=0))
               + jnp.where(last, 0.0, pltpu.roll(right, width - 1, axis=0)) + bias)
        dst = pl.multiple_of(r * width, width)
        o_ref[pl.ds(dst, width), :] = (_silu(out) * scale).astype(o_ref.dtype)

    def body(r, carry):
        one_row(r)
        return carry

    lax.fori_loop(0, rows, body, 0)


def _short_conv(u, col0, n_ch, conv_w, conv_b, grid):
    bsz, t, _ = u.shape
    cb = COL_BLOCK
    rows, width, taps = (t // GRID_W, GRID_W, (0, 1, 2)) if grid else (1, t, (1,))
    return pl.pallas_call(
        functools.partial(_conv_kernel, rows=rows, width=width, taps=taps),
        out_shape=jax.ShapeDtypeStruct((bsz, t, n_ch), BF16),
        grid=(bsz, n_ch // cb),
        in_specs=[pl.BlockSpec((None, t, cb), lambda b, j: (b, 0, col0 // cb + j)),
                  pl.BlockSpec((CONV_K * CONV_K, cb), lambda b, j: (0, j)),
                  pl.BlockSpec((1, cb), lambda b, j: (0, j))],
        out_specs=pl.BlockSpec((None, t, cb), lambda b, j: (b, 0, j)),
        compiler_params=_params(2),
        name="conv_grid" if grid else "conv_seq",
    )(u, conv_w.reshape(CONV_K * CONV_K, n_ch).astype(F32), conv_b.reshape(1, n_ch).astype(F32))


def _hg_prepare(qraw, z, lb, fwd):
    lower, upper = _tri_masks(z.shape[0])
    tri = jnp.where(lower if fwd else upper, 1.0, 0.0).astype(BF16)
    f = lb + (1.0 - lb) * jax.nn.sigmoid(z)
    b2 = _tri_left(tri, jnp.log(f) * LOG2E)
    c2 = b2 - jnp.log(1.0 - f) * LOG2E
    return _silu(qraw), b2, c2


def _block_diag(a):
    r, w = a.shape
    zero = jnp.zeros((r, w // 2), a.dtype)
    return jnp.concatenate([jnp.concatenate([a[:, :w // 2], zero], axis=1),
                            jnp.concatenate([zero, a[:, w // 2:]], axis=1)], axis=0)


def _hg_scores(q_ref, b_ref, c_ref, fwd):
    c, w = q_ref.shape
    sub = HG_SUB
    nb = c // sub
    assert 2 * c == LANES and c & (c - 1) == 0
    row = lax.broadcasted_iota(jnp.int32, (sub, LANES), 0)
    col = lax.broadcasted_iota(jnp.int32, (sub, LANES), 1) & (c - 1)
    src = col & (sub - 1)
    code = jnp.where((src <= row) if fwd else (src >= row), col, -1)
    kr = lax.broadcasted_iota(jnp.int32, (w, LANES), 0)
    kl = lax.broadcasted_iota(jnp.int32, (w, LANES), 1)
    sel = jnp.where((kr >= w // 2) == (kl >= c), 1.0, 0.0).astype(BF16)

    units = []
    for j in range(nb):
        r0 = j * sub
        qb = q_ref[r0:r0 + sub, :]
        bb = b_ref[r0:r0 + sub, :]
        cb = c_ref[r0:r0 + sub, :]
        for s in range(sub):
            units.append(qb * jnp.exp2(bb - cb[s:s + 1, :]))
    red = jnp.dot(jnp.concatenate(units, axis=0).astype(BF16), sel, preferred_element_type=F32)

    blocks = []
    for j in range(nb):
        r0 = j * sub
        blk = jnp.zeros((sub, LANES), F32)
        for s in range(sub):
            u0 = (j * sub + s) * sub
            blk = jnp.where(code == r0 + s, red[u0:u0 + sub], blk)
        lo, hi = (0, r0) if fwd else (r0 + sub, c)
        if hi > lo:
            beta = b_ref[pl.ds(r0 - 1 if fwd else r0 + sub, 1), :]
            pieces = [jnp.zeros((lo, w), F32)] if lo else []
            pieces.append(jnp.exp2(beta - c_ref[lo:hi, :]))
            if hi < c:
                pieces.append(jnp.zeros((c - hi, w), F32))
            kh = jnp.concatenate(pieces, axis=0).astype(BF16)
            qh = q_ref[r0:r0 + sub, :] * jnp.exp2(b_ref[r0:r0 + sub, :] - beta)
            off = lax.dot_general(_block_diag(qh).astype(BF16), kh, _NT, preferred_element_type=F32)
            blk = blk + jnp.concatenate([off[:sub], off[sub:]], axis=1)
        blocks.append(blk)
    return jnp.concatenate(blocks, axis=0).astype(BF16)


def _hg_scores_factored(q_ref, b_ref, c_ref, fwd, sub):
    c, w = q_ref.shape
    nb = c // sub
    row = lax.broadcasted_iota(jnp.int32, (sub, LANES), 0)
    col = lax.broadcasted_iota(jnp.int32, (sub, LANES), 1) & (c - 1)
    blocks = []
    for j in range(nb):
        r0 = j * sub
        lo, hi = (0, r0 + sub) if fwd else (r0, c)
        edge = r0 - 1 if fwd else r0 + sub
        beta = b_ref[pl.ds(edge, 1), :] if 0 <= edge < c else jnp.zeros((1, w), F32)
        pieces = [jnp.zeros((lo, w), F32)] if lo else []
        pieces.append(jnp.exp2(beta - c_ref[lo:hi, :]))
        if hi < c:
            pieces.append(jnp.zeros((c - hi, w), F32))
        kh = jnp.concatenate(pieces, axis=0).astype(BF16)
        qh = q_ref[r0:r0 + sub, :] * jnp.exp2(b_ref[r0:r0 + sub, :] - beta)
        off = lax.dot_general(_block_diag(qh).astype(BF16), kh, _NT, preferred_element_type=F32)
        blk = jnp.concatenate([off[:sub], off[sub:]], axis=1)
        causal = (col <= row + r0) if fwd else (col >= row + r0)
        blocks.append(jnp.where(causal, blk, 0.0))
    return jnp.concatenate(blocks, axis=0).astype(BF16)


def _hg_state_update(st_refs, v, k_dec, dec, qe=None):
    hw = v.shape[1] // 2
    outs = []
    for h, st_ref in enumerate(st_refs):
        ls = slice(h * hw, (h + 1) * hw)
        st = st_ref[...]
        st_ref[...] = st * dec[:, ls] + lax.dot_general(v[:, ls], k_dec[:, ls], _TN, preferred_element_type=F32)
        if qe is not None:
            outs.append(lax.dot_general(qe[:, ls], st.astype(BF16), _NT, preferred_element_type=F32))
    return outs


def _hgrn2_kernel(cq, cff, cfb, ci, aq, aff, afb, ai, az, lb_ref, nw_ref, y_ref,
                  acc_f, acc_b, st_ref, q_sc, b_sc, c_sc, a_sc, qe_sc, kd_sc, dec_sc, *, slots):
    c = HG_CHUNK
    ncc = cq.shape[0] // c
    nc = aq.shape[0] // c
    hw = HG_DK
    assert nc % 4 == 0
    st_ref[...] = jnp.zeros_like(st_ref)
    accs = (acc_f, acc_b)
    lbs = []
    for d in range(2):
        hg = lb_ref[d * slots:(d + 1) * slots, :]
        e = jnp.exp(hg - jnp.max(hg, axis=0, keepdims=True))
        lbs.append(e[0:1, :] / jnp.sum(e, axis=0, keepdims=True))

    def chunk_rows(d, k, n):
        return pl.ds(pl.multiple_of((k if d == 0 else n - 1 - k) * c, c), c)

    def states(d):
        return st_ref.at[2 * d], st_ref.at[2 * d + 1]

    def ctx_body(k, carry):
        for d in range(2):
            rows = chunk_rows(d, k, ncc)
            _, b2, c2 = _hg_prepare(cq[rows, :].astype(F32), (cff, cfb)[d][rows, :].astype(F32), lbs[d], d == 0)
            end = c - 1 if d == 0 else 0
            b_end = b2[end:end + 1, :]
            _hg_state_update(states(d), ci[rows, :], jnp.exp2(b_end - c2).astype(BF16), jnp.exp2(b_end))
        return carry

    def gates(d, k, slot):
        rows = chunk_rows(d, k, nc)
        q, b2, c2 = _hg_prepare(aq[rows, :].astype(F32), (aff, afb)[d][rows, :].astype(F32), lbs[d], d == 0)
        q_sc[slot, d] = q
        b_sc[slot, d] = b2
        c_sc[slot, d] = c2

    def scores(d, slot, block):
        qr, br, cr = q_sc.at[slot, d], b_sc.at[slot, d], c_sc.at[slot, d]
        end = c - 1 if d == 0 else 0
        b_end = br[end:end + 1, :]
        qe_sc[slot, d] = (qr[...] * jnp.exp2(br[...])).astype(BF16)
        kd_sc[slot, d] = jnp.exp2(b_end - cr[...]).astype(BF16)
        dec_sc[slot, d] = jnp.exp2(b_end)
        if block is None:
            a_sc[slot, d] = _hg_scores(qr, br, cr, d == 0)
        else:
            a_sc[slot, d] = _hg_scores_factored(qr, br, cr, d == 0, block)

    def outputs(d, k, slot):
        rows = chunk_rows(d, k, nc)
        v = ai[rows, :]
        outs = _hg_state_update(states(d), v, kd_sc[slot, d], dec_sc[slot, d], qe_sc[slot, d])
        accs[d][rows, :] = (jnp.concatenate(outs, axis=1)
                            + jnp.dot(a_sc[slot, d], _block_diag(v), preferred_element_type=F32))

    def step(k, slot, block, do_out=True, do_gates=True, do_scores=True):
        for d in range(2):
            if do_gates:
                gates(d, k, slot)
            if do_scores:
                scores(d, 1 - slot, block)
            if do_out:
                outputs(d, k - 2, slot)

    def scan(block):
        def quad_body(kq, carry):
            for i in range(4):
                step(4 * kq + i, i % 2, block)
            return carry

        step(0, 0, block, do_out=False, do_scores=False)
        step(1, 1, block, do_out=False)
        step(2, 0, block)
        step(3, 1, block)
        lax.fori_loop(1, nc // 4, quad_body, 0)
        step(nc, 0, block, do_gates=False)
        step(nc + 1, 1, block, do_gates=False, do_scores=False)

    def final_body(i, carry):
        rows = pl.ds(pl.multiple_of(i * c, c), c)
        tot = acc_f[rows, :] + acc_b[rows, :]
        gate = _silu(az[rows, :].astype(F32)) * nw_ref[...]
        for h in range(2):
            ls = slice(h * hw, (h + 1) * hw)
            th = tot[:, ls]
            ms = jnp.mean(th * th, axis=-1, keepdims=True)
            y_ref[rows, ls] = (th * lax.rsqrt(ms + NORM_EPS) * gate[:, ls]).astype(y_ref.dtype)
        return carry

    lax.fori_loop(0, ncc, ctx_body, 0, unroll=True)
    unit = jnp.max(-jnp.log(jnp.minimum(lbs[0], lbs[1])) * LOG2E)
    wide = c * unit <= HG_SAFE_LOG2
    narrow = jnp.logical_and(jnp.logical_not(wide), HG_SUB * unit <= HG_SAFE_LOG2)
    pl.when(wide)(lambda: scan(c))
    pl.when(narrow)(lambda: scan(HG_SUB))
    pl.when(jnp.logical_not(jnp.logical_or(wide, narrow)))(lambda: scan(None))
    lax.fori_loop(0, nc, final_body, 0, unroll=4)


def _hgrn2(u_x, u_c, w_a, hg_lb, hg_norm):
    bsz, t, _ = u_x.shape
    tc = u_c.shape[1]
    cb = 2 * HG_DK
    c = HG_CHUNK
    assert cb == COL_BLOCK
    seg = w_a // cb
    slots = hg_lb.shape[1]
    lb2 = hg_lb.reshape(2 * slots, w_a).astype(F32)

    def col(k):
        return lambda b, p: (b, 0, k * seg + p)

    ctx_specs = [pl.BlockSpec((None, tc, cb), col(k)) for k in (0, 1, 2, 3)]
    lat_specs = [pl.BlockSpec((None, t, cb), col(k)) for k in (0, 1, 2, 3, 4)]
    return pl.pallas_call(
        functools.partial(_hgrn2_kernel, slots=slots),
        out_shape=jax.ShapeDtypeStruct((bsz, t, w_a), BF16),
        grid=(bsz, seg),
        in_specs=ctx_specs + lat_specs + [
            pl.BlockSpec((2 * slots, cb), lambda b, p: (0, p)),
            pl.BlockSpec((1, cb), lambda b, p: (0, p))],
        out_specs=pl.BlockSpec((None, t, cb), lambda b, p: (b, 0, p)),
        scratch_shapes=[pltpu.VMEM((t, cb), F32),
                        pltpu.VMEM((t, cb), F32),
                        pltpu.VMEM((4, HG_DK, HG_DK), F32),
                        pltpu.VMEM((2, 2, c, cb), F32),
                        pltpu.VMEM((2, 2, c, cb), F32),
                        pltpu.VMEM((2, 2, c, cb), F32),
                        pltpu.VMEM((2, 2, c, 2 * c), BF16),
                        pltpu.VMEM((2, 2, c, cb), BF16),
                        pltpu.VMEM((2, 2, c, cb), BF16),
                        pltpu.VMEM((2, 2, 1, cb), F32)],
        compiler_params=_params(2),
        name="hgrn2",
    )(u_c, u_c, u_c, u_c, u_x, u_x, u_x, u_x, u_x, lb2, hg_norm.reshape(1, w_a).astype(F32))


def _ml_gates(a_row, b_row, amax_row, m_prev, fwd):
    end = a_row.shape[1] - 1 if fwd else 0
    b_end = b_row[:, end:end + 1]
    big_m = jnp.maximum(m_prev, amax_row)
    m_row = b_row + big_m
    m_new = m_row[:, end:end + 1]
    w_s = jnp.exp(b_end + a_row - m_new)
    decay = jnp.exp(b_end + m_prev - m_new)
    return big_m, m_row, m_new, w_s, decay


def _ml_state_lhs(v, w_s):
    vt = v.astype(F32).T
    lhs = jnp.concatenate([vt * w_s, jnp.broadcast_to(w_s, (ML_PAD, vt.shape[1]))], axis=0)
    return vt, lhs.astype(BF16)


def _mlstm_kernel(cq, ck, cv, cgc, cgr, xq, xk, xv, xo, xz, xgc, xgr, nw_ref, y_ref,
                  acc_f, acc_b, st_ref, wt_sc, vt_sc, lhs_sc, row_sc):
    c = ML_CHUNK
    ncc = cq.shape[0] // c
    nc = xq.shape[0] // c
    dh = xq.shape[1]
    assert nc % 2 == 0
    head = pl.program_id(1)
    nh = pl.num_programs(1)
    accs = (acc_f, acc_b)
    st_ref[...] = jnp.zeros_like(st_ref)
    lane = lax.broadcasted_iota(jnp.int32, (c, LANES), 1)
    lower, upper = _tri_masks(c)

    def chunk_rows(d, k, n):
        return pl.ds(pl.multiple_of((k if d == 0 else n - 1 - k) * c, c), c)

    def gate_rows(grr, d, rows):
        slot = d * nh + head
        return (grr[pl.ds(slot, 1), rows], grr[pl.ds(2 * nh + slot, 1), rows],
                grr[pl.ds(4 * nh + slot, 1), rows])

    def ctx_body(k, ms):
        out = []
        for d in range(2):
            rows = chunk_rows(d, k, ncc)
            _, _, m_new, w_s, decay = _ml_gates(*gate_rows(cgr, d, rows), ms[d], d == 0)
            _, lhs = _ml_state_lhs(cv[rows, :], w_s)
            st_ref[d] = decay * st_ref[d] + jnp.dot(lhs, ck[rows, :], preferred_element_type=F32)
            out.append(m_new)
        return tuple(out)

    def score_stage(d, k, slot, m_prev):
        rows = chunk_rows(d, k, nc)
        big_m, m_row, m_new, w_s, decay = _ml_gates(*gate_rows(xgr, d, rows), m_prev, d == 0)
        vt, lhs = _ml_state_lhs(xv[rows, :], w_s)
        lhs_sc[slot, d] = lhs
        vt_sc[slot, d] = vt.astype(BF16)
        a_col = jnp.sum(jnp.where(lane == d * nh + head, xgc[rows, :], 0.0), axis=1, keepdims=True)
        mask = upper if d == 0 else lower
        w_t = (jnp.exp(jnp.where(mask, a_col - big_m, -jnp.inf))
               * lax.dot_general(xk[rows, :], xq[rows, :], _NT, preferred_element_type=F32))
        wt_sc[slot, d] = w_t
        row_sc[slot, d, 0:1, :] = jnp.sum(w_t, axis=0, keepdims=True)
        row_sc[slot, d, 1:2, :] = jnp.exp(m_prev - big_m)
        row_sc[slot, d, 2:3, :] = jnp.exp(-m_row)
        row_sc[slot, d, 3:4, :] = jnp.broadcast_to(decay, (1, c))
        return m_new

    def output_stage(d, k, slot):
        rows = chunk_rows(d, k, nc)
        st = st_ref[d]
        r = lax.dot_general(st.astype(BF16), xq[rows, :], _NT, preferred_element_type=F32)
        w_inter = row_sc[slot, d, 1:2, :]
        den = w_inter * r[dh:dh + 1, :] + row_sc[slot, d, 0:1, :]
        inv = 1.0 / jnp.maximum(jnp.abs(den), row_sc[slot, d, 2:3, :])
        accs[d][:, rows] = (r[:dh, :] * (w_inter * inv)
                            + jnp.dot(vt_sc[slot, d], (wt_sc[slot, d] * inv).astype(BF16),
                                      preferred_element_type=F32))
        st_ref[d] = (row_sc[slot, d, 3:4, 0:1] * st
                     + jnp.dot(lhs_sc[slot, d], xk[rows, :], preferred_element_type=F32))

    def pair_body(kk, ms):
        for slot in range(2):
            k = 2 * kk + slot
            nxt = jnp.minimum(k + 1, nc - 1)
            new_ms = []
            for d in range(2):
                new_ms.append(score_stage(d, nxt, 1 - slot, ms[d]))
                output_stage(d, k, slot)
            ms = tuple(new_ms)
        return ms

    def final_body(i, carry):
        rows = pl.ds(pl.multiple_of(i * c, c), c)
        tot = acc_f[:, rows] + acc_b[:, rows]
        mu = jnp.mean(tot, axis=0, keepdims=True)
        tc_ = tot - mu
        var = jnp.mean(tc_ * tc_, axis=0, keepdims=True)
        normed = (tc_ * lax.rsqrt(var + NORM_EPS)).T * nw_ref[...]
        gate = jax.nn.sigmoid(xo[rows, :].astype(F32)) * _silu(xz[rows, :].astype(F32))
        y_ref[rows, :] = (normed * gate).astype(y_ref.dtype)
        return carry

    ms = (jnp.zeros((1, 1), F32), jnp.zeros((1, 1), F32))
    ms = lax.fori_loop(0, ncc, ctx_body, ms)
    ms = tuple(score_stage(d, 0, 0, ms[d]) for d in range(2))
    lax.fori_loop(0, nc // 2, pair_body, ms)
    lax.fori_loop(0, nc, final_body, 0, unroll=2)


def _mlstm(qk_x, qk_c, u_x, u_c, v_col, o_col, z_col, gates_x, gates_c, ml_norm, w_b):
    bsz, t, _ = u_x.shape
    tc = u_c.shape[1]
    nh = w_b // ML_DH
    cb = ML_DH
    assert cb == COL_BLOCK and t % (2 * ML_CHUNK) == 0 and tc % ML_CHUNK == 0

    def col(k):
        return lambda b, h: (b, 0, k + h)

    def seq_specs(n, with_gates):
        specs = [pl.BlockSpec((None, n, cb), col(0)),
                 pl.BlockSpec((None, n, cb), col(nh)),
                 pl.BlockSpec((None, n, cb), col(v_col // cb))]
        if with_gates:
            specs += [pl.BlockSpec((None, n, cb), col(o_col // cb)),
                      pl.BlockSpec((None, n, cb), col(z_col // cb))]
        specs += [pl.BlockSpec((n, LANES), lambda b, h: (b, 0)),
                  pl.BlockSpec((GATE_ROWS, n), lambda b, h: (0, b))]
        return specs

    return pl.pallas_call(
        _mlstm_kernel,
        out_shape=jax.ShapeDtypeStruct((bsz, t, w_b), BF16),
        grid=(bsz, nh),
        in_specs=seq_specs(tc, False) + seq_specs(t, True) + [pl.BlockSpec((1, cb), lambda b, h: (0, h))],
        out_specs=pl.BlockSpec((None, t, cb), lambda b, h: (b, 0, h)),
        scratch_shapes=[pltpu.VMEM((cb, t), F32),
                        pltpu.VMEM((cb, t), F32),
                        pltpu.VMEM((2, ML_DH + ML_PAD, ML_DH), F32),
                        pltpu.VMEM((2, 2, ML_CHUNK, ML_CHUNK), F32),
                        pltpu.VMEM((2, 2, ML_DH, ML_CHUNK), BF16),
                        pltpu.VMEM((2, 2, ML_DH + ML_PAD, ML_CHUNK), BF16),
                        pltpu.VMEM((2, 2, 8, ML_CHUNK), F32)],
        compiler_params=_params(2),
        name="mlstm",
    )(qk_c, qk_c, u_c, gates_c[0], gates_c[1],
      qk_x, qk_x, u_x, u_x, u_x, gates_x[0], gates_x[1],
      ml_norm.reshape(1, w_b).astype(F32))


def _outproj_kernel(ya_ref, yb_ref, x_ref, gate_ref, wa_ref, wb_ref, g_ref, b_ref, o_ref, *, sub):
    for r0 in range(0, x_ref.shape[0], sub):
        rows = slice(r0, r0 + sub)
        y = (jnp.dot(ya_ref[rows, :], wa_ref[...], preferred_element_type=F32)
             + jnp.dot(yb_ref[rows, :], wb_ref[...], preferred_element_type=F32))
        r = ALPHA * x_ref[rows, :] + gate_ref[...] * y
        mu = jnp.mean(r, axis=-1, keepdims=True)
        rc = r - mu
        var = jnp.mean(rc * rc, axis=-1, keepdims=True)
        o_ref[rows, :] = rc * lax.rsqrt(var + LN_EPS) * g_ref[...] + b_ref[...]


def _outproj(y_a, y_b, x2, mod3, t, w_oa, w_ob, ln_g, ln_b):
    rows, d = x2.shape
    tm = 512
    wa, wb = y_a.shape[1], y_b.shape[1]
    return pl.pallas_call(
        functools.partial(_outproj_kernel, sub=256),
        out_shape=jax.ShapeDtypeStruct((rows, d), F32),
        grid=(rows // tm,),
        in_specs=[pl.BlockSpec((tm, wa), lambda i: (i, 0)),
                  pl.BlockSpec((tm, wb), lambda i: (i, 0)),
                  pl.BlockSpec((tm, d), lambda i: (i, 0)),
                  pl.BlockSpec((None, 1, d), lambda i: ((i * tm) // t, 0, 2)),
                  pl.BlockSpec((wa, d), lambda i: (0, 0)),
                  pl.BlockSpec((wb, d), lambda i: (0, 0)),
                  pl.BlockSpec((1, d), lambda i: (0, 0)),
                  pl.BlockSpec((1, d), lambda i: (0, 0))],
        out_specs=pl.BlockSpec((tm, d), lambda i: (i, 0)),
        compiler_params=_params(1),
        name="outproj",
    )(y_a, y_b, x2, mod3, w_oa, w_ob, ln_g.reshape(1, d).astype(F32), ln_b.reshape(1, d).astype(F32))


def kernel(x, c, ctx, c_ctx, w_mod, b_mod, w_in, conv_w, conv_b, hg_lb, ml_gate_b, hg_norm_w,
           ml_norm_w, w_out, ln_g, ln_b):
    bsz, t, d = x.shape
    tc = ctx.shape[1]
    assert w_in.shape[0] == DEPTH and hg_lb.shape[1] == DEPTH + 1
    d_inner = w_out.shape[1]
    w_a = hg_lb.shape[2]
    w_b = d_inner - w_a
    nh = ml_gate_b.shape[2]
    nu = 5 * w_a + 5 * w_b
    assert nh * ML_DH == w_b and w_in.shape[2] == nu + 4 * nh and t % GRID_W == 0

    mod_rows = 8 * ((bsz + 1 + 7) // 8)
    cc = jnp.zeros((mod_rows, d), F32).at[:bsz].set(c).at[bsz].set(c_ctx)
    mod3 = _modulation(cc, w_mod[0], b_mod[0]).reshape(mod_rows, 1, 3 * d)

    w_in_t = w_in[0].T
    w_main = _wcast(w_in_t, 0, nu, 512)
    w_g = _wcast(w_in_t, nu, LANES, LANES)
    tm = min(1024, t)
    u_x, g_x = _inproj(x.reshape(bsz * t, d), mod3, lambda i: (i * tm) // t, w_main, w_g, tm)
    u_c, g_c = _inproj(ctx.reshape(bsz * tc, d), mod3, lambda i: bsz, w_main, w_g, min(1024, bsz * tc))
    u_x = u_x.reshape(bsz, t, nu)
    u_c = u_c.reshape(bsz, tc, nu)

    gates_x = _gateprep(g_x, ml_gate_b[0], nh, ML_CHUNK)
    gates_c = _gateprep(g_c, ml_gate_b[0], nh, ML_CHUNK)

    qk_col = 5 * w_a
    qk_x = _short_conv(u_x, qk_col, 2 * w_b, conv_w[0], conv_b[0], grid=True)
    qk_c = _short_conv(u_c, qk_col, 2 * w_b, conv_w[0], conv_b[0], grid=False)

    y_a = _hgrn2(u_x, u_c, w_a, hg_lb, hg_norm_w[0])
    y_b = _mlstm(qk_x, qk_c, u_x, u_c, qk_col + 2 * w_b, qk_col + 3 * w_b, qk_col + 4 * w_b,
                 gates_x, gates_c, ml_norm_w[0], w_b)

    w_o = w_out[0].astype(BF16)
    out = _outproj(y_a.reshape(bsz * t, w_a), y_b.reshape(bsz * t, w_b), x.reshape(bsz * t, d), mod3, t,
                   w_o[:w_a], w_o[w_a:], ln_g[0], ln_b[0])
    return out.reshape(bsz, t, d)
```

```python
import functools

import jax
import jax.numpy as jnp
from jax import lax
from jax.experimental import pallas as pl
from jax.experimental.pallas import tpu as pltpu

F32 = jnp.float32
BF16 = jnp.bfloat16

LN_EPS = 1e-5
NORM_EPS = 1e-6
DEPTH = 1
ALPHA = (2 * DEPTH) ** 0.25

HG_DK = 128
ML_DH = 256
GRID_W = 64
CONV_K = 3

LANES = 128
HG_CHUNK = 64
HG_SUB = 8
HG_SAFE_LOG2 = 96.0
LOG2E = 1.4426950408889634
ML_CHUNK = 256
ML_PAD = 16
GATE_ROWS = 24
COL_BLOCK = 256
VMEM_LIMIT = 56 * 1024 * 1024

_NT = (((1,), (1,)), ((), ()))
_TN = (((0,), (0,)), ((), ()))


def _params(n_grid):
    return pltpu.CompilerParams(dimension_semantics=("arbitrary",) * n_grid,
                                vmem_limit_bytes=VMEM_LIMIT)


def _silu(a):
    return a * jax.nn.sigmoid(a)


def _log_sigmoid(a):
    return jnp.minimum(a, 0.0) - jnp.log1p(jnp.exp(-jnp.abs(a)))


def _split3(a):
    hi = a.astype(BF16)
    r1 = a - hi.astype(F32)
    mid = r1.astype(BF16)
    lo = (r1 - mid.astype(F32)).astype(BF16)
    return hi, mid, lo


def _tri_left(tri, a):
    return sum(jnp.dot(tri, t, preferred_element_type=F32) for t in _split3(a))


def _tri_right(a, tri):
    return sum(jnp.dot(t, tri, preferred_element_type=F32) for t in _split3(a))


def _tri_masks(n):
    r = lax.broadcasted_iota(jnp.int32, (n, n), 0)
    c = lax.broadcasted_iota(jnp.int32, (n, n), 1)
    return r >= c, r <= c


def _mod_kernel(c_ref, w_ref, b_ref, o_ref):
    s = _silu(c_ref[...]).astype(BF16)
    o_ref[...] = jnp.dot(s, w_ref[...].astype(BF16), preferred_element_type=F32) + b_ref[...]


def _modulation(cc, w_mod, b_mod):
    rows, d = cc.shape
    n = w_mod.shape[1]
    tn = 512
    return pl.pallas_call(
        _mod_kernel,
        out_shape=jax.ShapeDtypeStruct((rows, n), F32),
        grid=(n // tn,),
        in_specs=[pl.BlockSpec((rows, d), lambda j: (0, 0)),
                  pl.BlockSpec((d, tn), lambda j: (0, j)),
                  pl.BlockSpec((1, tn), lambda j: (0, j))],
        out_specs=pl.BlockSpec((rows, tn), lambda j: (0, j)),
        compiler_params=_params(1),
        name="mod",
    )(cc, w_mod, b_mod.reshape(1, n))


def _wcast_kernel(wt_ref, o_ref, *, valid):
    w = wt_ref[...]
    if valid < w.shape[0]:
        w = jnp.where(lax.broadcasted_iota(jnp.int32, w.shape, 0) < valid, w, 0.0)
    o_ref[...] = w.T.astype(o_ref.dtype)


def _wcast(wt, col0, n_out, tn):
    n, d = wt.shape
    assert col0 % tn == 0 and n_out % tn == 0
    return pl.pallas_call(
        functools.partial(_wcast_kernel, valid=min(tn, n - col0)),
        out_shape=jax.ShapeDtypeStruct((d, n_out), BF16),
        grid=(n_out // tn,),
        in_specs=[pl.BlockSpec((tn, d), lambda j: (col0 // tn + j, 0))],
        out_specs=pl.BlockSpec((d, tn), lambda j: (0, j)),
        compiler_params=_params(1),
        name="wcast",
    )(wt)


def _ln_modulate(xv, shift, scale):
    mu = jnp.mean(xv, axis=-1, keepdims=True)
    xc = xv - mu
    var = jnp.mean(xc * xc, axis=-1, keepdims=True)
    return (xc * lax.rsqrt(var + LN_EPS) * (1.0 + scale) + shift).astype(BF16)


def _ln_kernel(x_ref, sh_ref, sc_ref, h_ref):
    h_ref[...] = _ln_modulate(x_ref[...], sh_ref[...], sc_ref[...])


def _inproj_kernel(xn_ref, h0_ref, sh_ref, sc_ref, w_ref, wg_ref, u_ref, g_ref, h_a, h_b, *, nsub):
    i = pl.program_id(0)
    j = pl.program_id(1)
    rs = xn_ref.shape[0]

    @pl.when((i == 0) & (j == 0))
    def _():
        h_a[...] = h0_ref[...]

    def body(h_cur, h_nxt):
        @pl.when(j == 0)
        def _():
            g_ref[...] = jnp.dot(h_cur[...], wg_ref[...], preferred_element_type=F32)

        rows = pl.ds(pl.multiple_of(jnp.minimum(j, nsub - 1) * rs, rs), rs)
        h_nxt[rows, :] = _ln_modulate(xn_ref[...], sh_ref[...], sc_ref[...])
        u_ref[...] = jnp.dot(h_cur[...], w_ref[...], preferred_element_type=F32).astype(BF16)

    pl.when(i % 2 == 0)(lambda: body(h_a, h_b))
    pl.when(i % 2 == 1)(lambda: body(h_b, h_a))


def _inproj(x2, mod3, mod_row, w_main, w_g, tm):
    rows, d = x2.shape
    nu = w_main.shape[1]
    tn = 1024 if nu % 1024 == 0 else 512
    nt, nj = rows // tm, nu // tn
    nsub = min(8, 1 << (nj.bit_length() - 1))
    rs = tm // nsub
    assert nu % tn == 0 and rows % tm == 0 and tm % nsub == 0

    def mod_spec(part, row_of):
        return pl.BlockSpec((None, 1, d), lambda *ij: (row_of(ij[0]), 0, part))

    h0 = pl.pallas_call(
        _ln_kernel,
        out_shape=jax.ShapeDtypeStruct((tm, d), BF16),
        grid=(2,),
        in_specs=[pl.BlockSpec((tm // 2, d), lambda r: (r, 0)),
                  mod_spec(0, lambda r: mod_row(0)), mod_spec(1, lambda r: mod_row(0))],
        out_specs=pl.BlockSpec((tm // 2, d), lambda r: (r, 0)),
        compiler_params=_params(1),
        name="ln0",
    )(x2, mod3, mod3)

    def nxt(i):
        return jnp.minimum(i + 1, nt - 1)

    return pl.pallas_call(
        functools.partial(_inproj_kernel, nsub=nsub),
        out_shape=(jax.ShapeDtypeStruct((rows, nu), BF16),
                   jax.ShapeDtypeStruct((rows, LANES), F32)),
        grid=(nt, nj),
        in_specs=[pl.BlockSpec((rs, d), lambda i, j: (nxt(i) * nsub + jnp.minimum(j, nsub - 1), 0)),
                  pl.BlockSpec((tm, d), lambda i, j: (0, 0)),
                  mod_spec(0, lambda i: mod_row(nxt(i))), mod_spec(1, lambda i: mod_row(nxt(i))),
                  pl.BlockSpec((d, tn), lambda i, j: (0, j)),
                  pl.BlockSpec((d, LANES), lambda i, j: (0, 0))],
        out_specs=(pl.BlockSpec((tm, tn), lambda i, j: (i, j)),
                   pl.BlockSpec((tm, LANES), lambda i, j: (i, 0))),
        scratch_shapes=[pltpu.VMEM((tm, d), BF16), pltpu.VMEM((tm, d), BF16)],
        compiler_params=_params(2),
        name="inproj",
    )(x2, h0, mod3, mod3, w_main, w_g)


def _gateprep_kernel(g_ref, bias_ref, gc_ref, gr_ref, *, nh, c):
    lower, upper = _tri_masks(c)
    tril = jnp.where(lower, 1.0, 0.0).astype(BF16)
    triu = jnp.where(upper, 1.0, 0.0).astype(BF16)
    lane = lax.broadcasted_iota(jnp.int32, (c, LANES), 1)
    fwd_f = (lane >= 2 * nh) & (lane < 3 * nh)
    bwd_f = (lane >= 3 * nh) & (lane < 4 * nh)
    row = lax.broadcasted_iota(jnp.int32, (GATE_ROWS, c), 0)
    pos = lax.broadcasted_iota(jnp.int32, (GATE_ROWS, c), 1)
    fwd_r = (row >= 2 * nh) & (row < 3 * nh)
    bwd_r = (row >= 3 * nh) & (row < 4 * nh)

    for r0 in range(0, g_ref.shape[0], c):
        pre = g_ref[r0:r0 + c, :] + bias_ref[...]
        lf = _log_sigmoid(pre)
        b = jnp.where(fwd_f, _tri_left(tril, lf), jnp.where(bwd_f, _tri_left(triu, lf), 0.0))
        gc_ref[r0:r0 + c, :] = pre - pltpu.roll(b, LANES - 2 * nh, axis=1)

        pre_t = pre.T[:GATE_ROWS]
        lf_t = _log_sigmoid(pre_t)
        b_t = jnp.where(fwd_r, _tri_right(lf_t, triu), jnp.where(bwd_r, _tri_right(lf_t, tril), 0.0))
        a_t = pre_t - pltpu.roll(b_t, GATE_ROWS - 2 * nh, axis=0)
        pmax, smax = a_t, a_t
        shift = 1
        while shift < c:
            pmax = jnp.maximum(pmax, jnp.where(pos >= shift, pltpu.roll(pmax, shift, axis=1), -jnp.inf))
            smax = jnp.maximum(smax, jnp.where(pos < c - shift, pltpu.roll(smax, c - shift, axis=1), -jnp.inf))
            shift *= 2
        amax_t = pltpu.roll(jnp.where(row < nh, pmax, smax), 4 * nh, axis=0)
        gr_ref[:, r0:r0 + c] = jnp.where(row < 2 * nh, a_t, jnp.where(row < 4 * nh, b_t, amax_t))


def _gateprep(g, bias, nh, chunk):
    rows = g.shape[0]
    assert 6 * nh <= GATE_ROWS
    blk = 4 * chunk if rows % (4 * chunk) == 0 else chunk
    bias_row = jnp.zeros((1, LANES), F32).at[0, :4 * nh].set(bias.reshape(-1))
    return pl.pallas_call(
        functools.partial(_gateprep_kernel, nh=nh, c=chunk),
        out_shape=(jax.ShapeDtypeStruct((rows, LANES), F32),
                   jax.ShapeDtypeStruct((GATE_ROWS, rows), F32)),
        grid=(rows // blk,),
        in_specs=[pl.BlockSpec((blk, LANES), lambda i: (i, 0)),
                  pl.BlockSpec((1, LANES), lambda i: (0, 0))],
        out_specs=(pl.BlockSpec((blk, LANES), lambda i: (i, 0)),
                   pl.BlockSpec((GATE_ROWS, blk), lambda i: (0, i))),
        compiler_params=_params(1),
        name="gateprep",
    )(g, bias_row)


def _conv_kernel(x_ref, w_ref, b_ref, o_ref, *, rows, width, taps):
    cb = x_ref.shape[-1]
    j = pl.program_id(1)
    scale = jnp.where(j >= pl.num_programs(1) // 2, ML_DH ** -0.5, 1.0).astype(F32)
    w = w_ref[...]
    bias = b_ref[...]
    col = lax.broadcasted_iota(jnp.int32, (width, cb), 0)
    first = col == 0
    last = col == width - 1

    def one_row(r):
        left = mid = right = None
        for ki in taps:
            rr = r + (ki - 1)
            valid = jnp.where((rr >= 0) & (rr < rows), 1.0, 0.0).astype(F32)
            wk = w[3 * ki:3 * ki + 3, :] * valid
            start = pl.multiple_of(jnp.clip(rr, 0, rows - 1) * width, width)
            xv = x_ref[pl.ds(start, width), :].astype(F32)
            terms = [xv * wk[kj:kj + 1, :] for kj in range(3)]
            left, mid, right = terms if left is None else (left + terms[0], mid + terms[1], right + terms[2])
        out = (mid + jnp.where(first, 0.0, pltpu.roll(left, 1, axis=0))
               + jnp.where(last, 0.0, pltpu.roll(right, width - 1, axis=0)) + bias)
        dst = pl.multiple_of(r * width, width)
        o_ref[pl.ds(dst, width), :] = (_silu(out) * scale).astype(o_ref.dtype)

    def body(r, carry):
        one_row(r)
        return carry

    lax.fori_loop(0, rows, body, 0)


def _short_conv(u, col0, n_ch, conv_w, conv_b, grid):
    bsz, t, _ = u.shape
    cb = COL_BLOCK
    rows, width, taps = (t // GRID_W, GRID_W, (0, 1, 2)) if grid else (1, t, (1,))
    return pl.pallas_call(
        functools.partial(_conv_kernel, rows=rows, width=width, taps=taps),
        out_shape=jax.ShapeDtypeStruct((bsz, t, n_ch), BF16),
        grid=(bsz, n_ch // cb),
        in_specs=[pl.BlockSpec((None, t, cb), lambda b, j: (b, 0, col0 // cb + j)),
                  pl.BlockSpec((CONV_K * CONV_K, cb), lambda b, j: (0, j)),
                  pl.BlockSpec((1, cb), lambda b, j: (0, j))],
        out_specs=pl.BlockSpec((None, t, cb), lambda b, j: (b, 0, j)),
        compiler_params=_params(2),
        name="conv_grid" if grid else "conv_seq",
    )(u, conv_w.reshape(CONV_K * CONV_K, n_ch).astype(F32), conv_b.reshape(1, n_ch).astype(F32))


def _hg_prepare(qraw, z, lb, fwd):
    lower, upper = _tri_masks(z.shape[0])
    tri = jnp.where(lower if fwd else upper, 1.0, 0.0).astype(BF16)
    f = lb + (1.0 - lb) * jax.nn.sigmoid(z)
    b2 = _tri_left(tri, jnp.log(f) * LOG2E)
    c2 = b2 - jnp.log(1.0 - f) * LOG2E
    return _silu(qraw), b2, c2


def _block_diag(a):
    r, w = a.shape
    zero = jnp.zeros((r, w // 2), a.dtype)
    return jnp.concatenate([jnp.concatenate([a[:, :w // 2], zero], axis=1),
                            jnp.concatenate([zero, a[:, w // 2:]], axis=1)], axis=0)


def _hg_scores(q_ref, b_ref, c_ref, fwd):
    c, w = q_ref.shape
    sub = HG_SUB
    nb = c // sub
    assert 2 * c == LANES and c & (c - 1) == 0
    row = lax.broadcasted_iota(jnp.int32, (sub, LANES), 0)
    col = lax.broadcasted_iota(jnp.int32, (sub, LANES), 1) & (c - 1)
    src = col & (sub - 1)
    code = jnp.where((src <= row) if fwd else (src >= row), col, -1)
    kr = lax.broadcasted_iota(jnp.int32, (w, LANES), 0)
    kl = lax.broadcasted_iota(jnp.int32, (w, LANES), 1)
    sel = jnp.where((kr >= w // 2) == (kl >= c), 1.0, 0.0).astype(BF16)

    units = []
    for j in range(nb):
        r0 = j * sub
        qb = q_ref[r0:r0 + sub, :]
        bb = b_ref[r0:r0 + sub, :]
        cb = c_ref[r0:r0 + sub, :]
        for s in range(sub):
            units.append(qb * jnp.exp2(bb - cb[s:s + 1, :]))
    red = jnp.dot(jnp.concatenate(units, axis=0).astype(BF16), sel, preferred_element_type=F32)

    blocks = []
    for j in range(nb):
        r0 = j * sub
        blk = jnp.zeros((sub, LANES), F32)
        for s in range(sub):
            u0 = (j * sub + s) * sub
            blk = jnp.where(code == r0 + s, red[u0:u0 + sub], blk)
        lo, hi = (0, r0) if fwd else (r0 + sub, c)
        if hi > lo:
            beta = b_ref[pl.ds(r0 - 1 if fwd else r0 + sub, 1), :]
            pieces = [jnp.zeros((lo, w), F32)] if lo else []
            pieces.append(jnp.exp2(beta - c_ref[lo:hi, :]))
            if hi < c:
                pieces.append(jnp.zeros((c - hi, w), F32))
            kh = jnp.concatenate(pieces, axis=0).astype(BF16)
            qh = q_ref[r0:r0 + sub, :] * jnp.exp2(b_ref[r0:r0 + sub, :] - beta)
            off = lax.dot_general(_block_diag(qh).astype(BF16), kh, _NT, preferred_element_type=F32)
            blk = blk + jnp.concatenate([off[:sub], off[sub:]], axis=1)
        blocks.append(blk)
    return jnp.concatenate(blocks, axis=0).astype(BF16)


def _hg_scores_factored(q_ref, b_ref, c_ref, fwd, sub):
    c, w = q_ref.shape
    nb = c // sub
    row = lax.broadcasted_iota(jnp.int32, (sub, LANES), 0)
    col = lax.broadcasted_iota(jnp.int32, (sub, LANES), 1) & (c - 1)
    blocks = []
    for j in range(nb):
        r0 = j * sub
        lo, hi = (0, r0 + sub) if fwd else (r0, c)
        edge = r0 - 1 if fwd else r0 + sub
        beta = b_ref[pl.ds(edge, 1), :] if 0 <= edge < c else jnp.zeros((1, w), F32)
        pieces = [jnp.zeros((lo, w), F32)] if lo else []
        pieces.append(jnp.exp2(beta - c_ref[lo:hi, :]))
        if hi < c:
            pieces.append(jnp.zeros((c - hi, w), F32))
        kh = jnp.concatenate(pieces, axis=0).astype(BF16)
        qh = q_ref[r0:r0 + sub, :] * jnp.exp2(b_ref[r0:r0 + sub, :] - beta)
        off = lax.dot_general(_block_diag(qh).astype(BF16), kh, _NT, preferred_element_type=F32)
        blk = jnp.concatenate([off[:sub], off[sub:]], axis=1)
        causal = (col <= row + r0) if fwd else (col >= row + r0)
        blocks.append(jnp.where(causal, blk, 0.0))
    return jnp.concatenate(blocks, axis=0).astype(BF16)


def _hg_state_update(st_refs, v, k_dec, dec, qe=None):
    hw = v.shape[1] // 2
    outs = []
    for h, st_ref in enumerate(st_refs):
        ls = slice(h * hw, (h + 1) * hw)
        st = st_ref[...]
        if qe is not None:
            outs.append(lax.dot_general(qe[:, ls], st.astype(BF16), _NT, preferred_element_type=F32))
        st_ref[...] = st * dec[:, ls] + lax.dot_general(v[:, ls], k_dec[:, ls], _TN, preferred_element_type=F32)
    return outs


def _hgrn2_kernel(cq, cff, cfb, ci, aq, aff, afb, ai, az, lb_ref, nw_ref, y_ref,
                  acc_f, acc_b, st_ref, q_sc, b_sc, c_sc, a_sc, qe_sc, kd_sc, dec_sc, *, slots):
    c = HG_CHUNK
    ncc = cq.shape[0] // c
    nc = aq.shape[0] // c
    hw = HG_DK
    assert nc % 4 == 0
    st_ref[...] = jnp.zeros_like(st_ref)
    accs = (acc_f, acc_b)
    lbs = []
    for d in range(2):
        hg = lb_ref[d * slots:(d + 1) * slots, :]
        e = jnp.exp(hg - jnp.max(hg, axis=0, keepdims=True))
        lbs.append(e[0:1, :] / jnp.sum(e, axis=0, keepdims=True))

    def chunk_rows(d, k, n):
        return pl.ds(pl.multiple_of((k if d == 0 else n - 1 - k) * c, c), c)

    def states(d):
        return st_ref.at[2 * d], st_ref.at[2 * d + 1]

    def ctx_body(k, carry):
        for d in range(2):
            rows = chunk_rows(d, k, ncc)
            _, b2, c2 = _hg_prepare(cq[rows, :].astype(F32), (cff, cfb)[d][rows, :].astype(F32), lbs[d], d == 0)
            end = c - 1 if d == 0 else 0
            b_end = b2[end:end + 1, :]
            _hg_state_update(states(d), ci[rows, :], jnp.exp2(b_end - c2).astype(BF16), jnp.exp2(b_end))
        return carry

    def gates(d, k, slot):
        rows = chunk_rows(d, k, nc)
        q, b2, c2 = _hg_prepare(aq[rows, :].astype(F32), (aff, afb)[d][rows, :].astype(F32), lbs[d], d == 0)
        q_sc[slot, d] = q
        b_sc[slot, d] = b2
        c_sc[slot, d] = c2

    def scores(d, slot, block):
        qr, br, cr = q_sc.at[slot, d], b_sc.at[slot, d], c_sc.at[slot, d]
        end = c - 1 if d == 0 else 0
        b_end = br[end:end + 1, :]
        qe_sc[slot, d] = (qr[...] * jnp.exp2(br[...])).astype(BF16)
        kd_sc[slot, d] = jnp.exp2(b_end - cr[...]).astype(BF16)
        dec_sc[slot, d] = jnp.exp2(b_end)
        if block is None:
            a_sc[slot, d] = _hg_scores(qr, br, cr, d == 0)
        else:
            a_sc[slot, d] = _hg_scores_factored(qr, br, cr, d == 0, block)

    def outputs(d, k, slot):
        rows = chunk_rows(d, k, nc)
        v = ai[rows, :]
        outs = _hg_state_update(states(d), v, kd_sc[slot, d], dec_sc[slot, d], qe_sc[slot, d])
        accs[d][rows, :] = (jnp.concatenate(outs, axis=1)
                            + jnp.dot(a_sc[slot, d], _block_diag(v), preferred_element_type=F32))

    def step(k, slot, block, do_out=True, do_gates=True, do_scores=True):
        for d in range(2):
            if do_gates:
                gates(d, k, slot)
            if do_scores:
                scores(d, 1 - slot, block)
            if do_out:
                outputs(d, k - 2, slot)

    def scan(block):
        def quad_body(kq, carry):
            for i in range(4):
                step(4 * kq + i, i % 2, block)
            return carry

        step(0, 0, block, do_out=False, do_scores=False)
        step(1, 1, block, do_out=False)
        step(2, 0, block)
        step(3, 1, block)
        lax.fori_loop(1, nc // 4, quad_body, 0)
        step(nc, 0, block, do_gates=False)
        step(nc + 1, 1, block, do_gates=False, do_scores=False)

    def final_body(i, carry):
        rows = pl.ds(pl.multiple_of(i * c, c), c)
        tot = acc_f[rows, :] + acc_b[rows, :]
        gate = _silu(az[rows, :].astype(F32)) * nw_ref[...]
        for h in range(2):
            ls = slice(h * hw, (h + 1) * hw)
            th = tot[:, ls]
            ms = jnp.mean(th * th, axis=-1, keepdims=True)
            y_ref[rows, ls] = (th * lax.rsqrt(ms + NORM_EPS) * gate[:, ls]).astype(y_ref.dtype)
        return carry

    lax.fori_loop(0, ncc, ctx_body, 0, unroll=True)
    unit = jnp.max(-jnp.log(jnp.minimum(lbs[0], lbs[1])) * LOG2E)
    wide = c * unit <= HG_SAFE_LOG2
    narrow = jnp.logical_and(jnp.logical_not(wide), HG_SUB * unit <= HG_SAFE_LOG2)
    pl.when(wide)(lambda: scan(c))
    pl.when(narrow)(lambda: scan(HG_SUB))
    pl.when(jnp.logical_not(jnp.logical_or(wide, narrow)))(lambda: scan(None))
    lax.fori_loop(0, nc, final_body, 0, unroll=4)


def _hgrn2(u_x, u_c, w_a, hg_lb, hg_norm):
    bsz, t, _ = u_x.shape
    tc = u_c.shape[1]
    cb = 2 * HG_DK
    c = HG_CHUNK
    assert cb == COL_BLOCK
    seg = w_a // cb
    slots = hg_lb.shape[1]
    lb2 = hg_lb.reshape(2 * slots, w_a).astype(F32)

    def col(k):
        return lambda b, p: (b, 0, k * seg + p)

    ctx_specs = [pl.BlockSpec((None, tc, cb), col(k)) for k in (0, 1, 2, 3)]
    lat_specs = [pl.BlockSpec((None, t, cb), col(k)) for k in (0, 1, 2, 3, 4)]
    return pl.pallas_call(
        functools.partial(_hgrn2_kernel, slots=slots),
        out_shape=jax.ShapeDtypeStruct((bsz, t, w_a), BF16),
        grid=(bsz, seg),
        in_specs=ctx_specs + lat_specs + [
            pl.BlockSpec((2 * slots, cb), lambda b, p: (0, p)),
            pl.BlockSpec((1, cb), lambda b, p: (0, p))],
        out_specs=pl.BlockSpec((None, t, cb), lambda b, p: (b, 0, p)),
        scratch_shapes=[pltpu.VMEM((t, cb), F32),
                        pltpu.VMEM((t, cb), F32),
                        pltpu.VMEM((4, HG_DK, HG_DK), F32),
                        pltpu.VMEM((2, 2, c, cb), F32),
                        pltpu.VMEM((2, 2, c, cb), F32),
                        pltpu.VMEM((2, 2, c, cb), F32),
                        pltpu.VMEM((2, 2, c, 2 * c), BF16),
                        pltpu.VMEM((2, 2, c, cb), BF16),
                        pltpu.VMEM((2, 2, c, cb), BF16),
                        pltpu.VMEM((2, 2, 1, cb), F32)],
        compiler_params=_params(2),
        name="hgrn2",
    )(u_c, u_c, u_c, u_c, u_x, u_x, u_x, u_x, u_x, lb2, hg_norm.reshape(1, w_a).astype(F32))


def _ml_gates(a_row, b_row, amax_row, m_prev, fwd):
    end = a_row.shape[1] - 1 if fwd else 0
    b_end = b_row[:, end:end + 1]
    big_m = jnp.maximum(m_prev, amax_row)
    m_row = b_row + big_m
    m_new = m_row[:, end:end + 1]
    w_s = jnp.exp(b_end + a_row - m_new)
    decay = jnp.exp(b_end + m_prev - m_new)
    return big_m, m_row, m_new, w_s, decay


def _ml_state_lhs(v, w_s):
    vt = v.astype(F32).T
    lhs = jnp.concatenate([vt * w_s, jnp.broadcast_to(w_s, (ML_PAD, vt.shape[1]))], axis=0)
    return vt, lhs.astype(BF16)


def _mlstm_kernel(cq, ck, cv, cgc, cgr, xq, xk, xv, xo, xz, xgc, xgr, nw_ref, y_ref,
                  acc_f, acc_b, st_ref, wt_sc, vt_sc, lhs_sc, row_sc):
    c = ML_CHUNK
    ncc = cq.shape[0] // c
    nc = xq.shape[0] // c
    dh = xq.shape[1]
    assert nc % 4 == 0
    head = pl.program_id(1)
    nh = pl.num_programs(1)
    accs = (acc_f, acc_b)
    st_ref[...] = jnp.zeros_like(st_ref)
    lane = lax.broadcasted_iota(jnp.int32, (c, LANES), 1)
    lower, upper = _tri_masks(c)

    def chunk_rows(d, k, n):
        return pl.ds(pl.multiple_of((k if d == 0 else n - 1 - k) * c, c), c)

    def gate_rows(grr, d, rows):
        slot = d * nh + head
        return (grr[pl.ds(slot, 1), rows], grr[pl.ds(2 * nh + slot, 1), rows],
                grr[pl.ds(4 * nh + slot, 1), rows])

    def ctx_body(k, ms):
        out = []
        for d in range(2):
            rows = chunk_rows(d, k, ncc)
            _, _, m_new, w_s, decay = _ml_gates(*gate_rows(cgr, d, rows), ms[d], d == 0)
            _, lhs = _ml_state_lhs(cv[rows, :], w_s)
            st_ref[d] = decay * st_ref[d] + jnp.dot(lhs, ck[rows, :], preferred_element_type=F32)
            out.append(m_new)
        return tuple(out)

    def score_stage(d, k, slot, m_prev):
        rows = chunk_rows(d, k, nc)
        kq = lax.dot_general(xk[rows, :], xq[rows, :], _NT, preferred_element_type=F32)
        big_m, m_row, m_new, w_s, decay = _ml_gates(*gate_rows(xgr, d, rows), m_prev, d == 0)
        a_col = jnp.sum(jnp.where(lane == d * nh + head, xgc[rows, :], 0.0), axis=1, keepdims=True)
        mask = upper if d == 0 else lower
        w_t = jnp.exp(jnp.where(mask, a_col - big_m, -jnp.inf)) * kq
        wt_sc[slot, d] = w_t
        vt, lhs = _ml_state_lhs(xv[rows, :], w_s)
        lhs_sc[slot, d] = lhs
        vt_sc[slot, d] = vt.astype(BF16)
        row_sc[slot, d, 0:1, :] = jnp.sum(w_t, axis=0, keepdims=True)
        row_sc[slot, d, 1:2, :] = jnp.exp(m_prev - big_m)
        row_sc[slot, d, 2:3, :] = jnp.exp(-m_row)
        row_sc[slot, d, 3:4, :] = jnp.broadcast_to(decay, (1, c))
        return m_new

    def output_stage(d, k, slot):
        rows = chunk_rows(d, k, nc)
        st = st_ref[d]
        r = lax.dot_general(st.astype(BF16), xq[rows, :], _NT, preferred_element_type=F32)
        w_inter = row_sc[slot, d, 1:2, :]
        den = w_inter * r[dh:dh + 1, :] + row_sc[slot, d, 0:1, :]
        inv = 1.0 / jnp.maximum(jnp.abs(den), row_sc[slot, d, 2:3, :])
        accs[d][:, rows] = (r[:dh, :] * (w_inter * inv)
                            + jnp.dot(vt_sc[slot, d], (wt_sc[slot, d] * inv).astype(BF16),
                                      preferred_element_type=F32))
        st_ref[d] = (row_sc[slot, d, 3:4, 0:1] * st
                     + jnp.dot(lhs_sc[slot, d], xk[rows, :], preferred_element_type=F32))

    def quad_body(kq, ms):
        for i in range(4):
            slot = i % 2
            k = 4 * kq + i
            nxt = jnp.minimum(k + 1, nc - 1)
            new_ms = []
            for d in range(2):
                new_ms.append(score_stage(d, nxt, 1 - slot, ms[d]))
                output_stage(d, k, slot)
            ms = tuple(new_ms)
        return ms

    def final_body(i, carry):
        rows = pl.ds(pl.multiple_of(i * c, c), c)
        tot = acc_f[:, rows] + acc_b[:, rows]
        mu = jnp.mean(tot, axis=0, keepdims=True)
        tc_ = tot - mu
        var = jnp.mean(tc_ * tc_, axis=0, keepdims=True)
        normed = (tc_ * lax.rsqrt(var + NORM_EPS)).T * nw_ref[...]
        gate = jax.nn.sigmoid(xo[rows, :].astype(F32)) * _silu(xz[rows, :].astype(F32))
        y_ref[rows, :] = (normed * gate).astype(y_ref.dtype)
        return carry

    ms = (jnp.zeros((1, 1), F32), jnp.zeros((1, 1), F32))
    ms = lax.fori_loop(0, ncc, ctx_body, ms)
    ms = tuple(score_stage(d, 0, 0, ms[d]) for d in range(2))
    lax.fori_loop(0, nc // 4, quad_body, ms)
    lax.fori_loop(0, nc, final_body, 0, unroll=2)


def _mlstm(qk_x, qk_c, u_x, u_c, v_col, o_col, z_col, gates_x, gates_c, ml_norm, w_b):
    bsz, t, _ = u_x.shape
    tc = u_c.shape[1]
    nh = w_b // ML_DH
    cb = ML_DH
    assert cb == COL_BLOCK and t % (4 * ML_CHUNK) == 0 and tc % ML_CHUNK == 0

    def col(k):
        return lambda b, h: (b, 0, k + h)

    def seq_specs(n, with_gates):
        specs = [pl.BlockSpec((None, n, cb), col(0)),
                 pl.BlockSpec((None, n, cb), col(nh)),
                 pl.BlockSpec((None, n, cb), col(v_col // cb))]
        if with_gates:
            specs += [pl.BlockSpec((None, n, cb), col(o_col // cb)),
                      pl.BlockSpec((None, n, cb), col(z_col // cb))]
        specs += [pl.BlockSpec((n, LANES), lambda b, h: (b, 0)),
                  pl.BlockSpec((GATE_ROWS, n), lambda b, h: (0, b))]
        return specs

    return pl.pallas_call(
        _mlstm_kernel,
        out_shape=jax.ShapeDtypeStruct((bsz, t, w_b), BF16),
        grid=(bsz, nh),
        in_specs=seq_specs(tc, False) + seq_specs(t, True) + [pl.BlockSpec((1, cb), lambda b, h: (0, h))],
        out_specs=pl.BlockSpec((None, t, cb), lambda b, h: (b, 0, h)),
        scratch_shapes=[pltpu.VMEM((cb, t), F32),
                        pltpu.VMEM((cb, t), F32),
                        pltpu.VMEM((2, ML_DH + ML_PAD, ML_DH), F32),
                        pltpu.VMEM((2, 2, ML_CHUNK, ML_CHUNK), F32),
                        pltpu.VMEM((2, 2, ML_DH, ML_CHUNK), BF16),
                        pltpu.VMEM((2, 2, ML_DH + ML_PAD, ML_CHUNK), BF16),
                        pltpu.VMEM((2, 2, 8, ML_CHUNK), F32)],
        compiler_params=_params(2),
        name="mlstm",
    )(qk_c, qk_c, u_c, gates_c[0], gates_c[1],
      qk_x, qk_x, u_x, u_x, u_x, gates_x[0], gates_x[1],
      ml_norm.reshape(1, w_b).astype(F32))


def _outproj_kernel(ya_ref, yb_ref, x_ref, gate_ref, wa_ref, wb_ref, g_ref, b_ref, o_ref, *, sub):
    for r0 in range(0, x_ref.shape[0], sub):
        rows = slice(r0, r0 + sub)
        y = (jnp.dot(ya_ref[rows, :], wa_ref[...], preferred_element_type=F32)
             + jnp.dot(yb_ref[rows, :], wb_ref[...], preferred_element_type=F32))
        r = ALPHA * x_ref[rows, :] + gate_ref[...] * y
        mu = jnp.mean(r, axis=-1, keepdims=True)
        rc = r - mu
        var = jnp.mean(rc * rc, axis=-1, keepdims=True)
        o_ref[rows, :] = rc * lax.rsqrt(var + LN_EPS) * g_ref[...] + b_ref[...]


def _outproj(y_a, y_b, x2, mod3, t, w_oa, w_ob, ln_g, ln_b):
    rows, d = x2.shape
    tm = 512
    wa, wb = y_a.shape[1], y_b.shape[1]
    return pl.pallas_call(
        functools.partial(_outproj_kernel, sub=256),
        out_shape=jax.ShapeDtypeStruct((rows, d), F32),
        grid=(rows // tm,),
        in_specs=[pl.BlockSpec((tm, wa), lambda i: (i, 0)),
                  pl.BlockSpec((tm, wb), lambda i: (i, 0)),
                  pl.BlockSpec((tm, d), lambda i: (i, 0)),
                  pl.BlockSpec((None, 1, d), lambda i: ((i * tm) // t, 0, 2)),
                  pl.BlockSpec((wa, d), lambda i: (0, 0)),
                  pl.BlockSpec((wb, d), lambda i: (0, 0)),
                  pl.BlockSpec((1, d), lambda i: (0, 0)),
                  pl.BlockSpec((1, d), lambda i: (0, 0))],
        out_specs=pl.BlockSpec((tm, d), lambda i: (i, 0)),
        compiler_params=_params(1),
        name="outproj",
    )(y_a, y_b, x2, mod3, w_oa, w_ob, ln_g.reshape(1, d).astype(F32), ln_b.reshape(1, d).astype(F32))


def kernel(x, c, ctx, c_ctx, w_mod, b_mod, w_in, conv_w, conv_b, hg_lb, ml_gate_b, hg_norm_w,
           ml_norm_w, w_out, ln_g, ln_b):
    bsz, t, d = x.shape
    tc = ctx.shape[1]
    assert w_in.shape[0] == DEPTH and hg_lb.shape[1] == DEPTH + 1
    d_inner = w_out.shape[1]
    w_a = hg_lb.shape[2]
    w_b = d_inner - w_a
    nh = ml_gate_b.shape[2]
    nu = 5 * w_a + 5 * w_b
    assert nh * ML_DH == w_b and w_in.shape[2] == nu + 4 * nh and t % GRID_W == 0

    mod_rows = 8 * ((bsz + 1 + 7) // 8)
    cc = jnp.zeros((mod_rows, d), F32).at[:bsz].set(c).at[bsz].set(c_ctx)
    mod3 = _modulation(cc, w_mod[0], b_mod[0]).reshape(mod_rows, 1, 3 * d)

    w_in_t = w_in[0].T
    w_main = _wcast(w_in_t, 0, nu, 512)
    w_g = _wcast(w_in_t, nu, LANES, LANES)
    tm = min(1024, t)
    u_x, g_x = _inproj(x.reshape(bsz * t, d), mod3, lambda i: (i * tm) // t, w_main, w_g, tm)
    u_c, g_c = _inproj(ctx.reshape(bsz * tc, d), mod3, lambda i: bsz, w_main, w_g, min(1024, bsz * tc))
    u_x = u_x.reshape(bsz, t, nu)
    u_c = u_c.reshape(bsz, tc, nu)

    gates_x = _gateprep(g_x, ml_gate_b[0], nh, ML_CHUNK)
    gates_c = _gateprep(g_c, ml_gate_b[0], nh, ML_CHUNK)

    qk_col = 5 * w_a
    qk_x = _short_conv(u_x, qk_col, 2 * w_b, conv_w[0], conv_b[0], grid=True)
    qk_c = _short_conv(u_c, qk_col, 2 * w_b, conv_w[0], conv_b[0], grid=False)

    y_a = _hgrn2(u_x, u_c, w_a, hg_lb, hg_norm_w[0])
    y_b = _mlstm(qk_x, qk_c, u_x, u_c, qk_col + 2 * w_b, qk_col + 3 * w_b, qk_col + 4 * w_b,
                 gates_x, gates_c, ml_norm_w[0], w_b)

    w_o = w_out[0].astype(BF16)
    out = _outproj(y_a.reshape(bsz * t, w_a), y_b.reshape(bsz * t, w_b), x.reshape(bsz * t, d), mod3, t,
                   w_o[:w_a], w_o[w_a:], ln_g[0], ln_b[0])
    return out.reshape(bsz, t, d)
```

```python
import functools

import jax
import jax.numpy as jnp
from jax import lax
from jax.experimental import pallas as pl
from jax.experimental.pallas import tpu as pltpu

F32 = jnp.float32
BF16 = jnp.bfloat16

LN_EPS = 1e-5
NORM_EPS = 1e-6
DEPTH = 1
ALPHA = (2 * DEPTH) ** 0.25

HG_DK = 128
ML_DH = 256
GRID_W = 64
CONV_K = 3

LANES = 128
HG_CHUNK = 64
HG_SUB = 8
HG_SAFE_LOG2 = 96.0
LOG2E = 1.4426950408889634
ML_CHUNK = 256
ML_PAD = 16
GATE_ROWS = 24
COL_BLOCK = 256
VMEM_LIMIT = 56 * 1024 * 1024

PROJ_TM = 1024
PROJ_TN = 1024
PROJ_TN_SMALL = 512
LN_SUBBLOCKS = 8
OUT_TM = 512
OUT_SUB = 256
WEIGHT_TN = 512
GATE_CHUNKS = 4

_NT = (((1,), (1,)), ((), ()))
_TN = (((0,), (0,)), ((), ()))


def _params(n_grid):
    return pltpu.CompilerParams(dimension_semantics=("arbitrary",) * n_grid,
                                vmem_limit_bytes=VMEM_LIMIT)


def _silu(a):
    return a * jax.nn.sigmoid(a)


def _log_sigmoid(a):
    return jnp.minimum(a, 0.0) - jnp.log1p(jnp.exp(-jnp.abs(a)))


def _split3(a):
    hi = a.astype(BF16)
    r1 = a - hi.astype(F32)
    mid = r1.astype(BF16)
    lo = (r1 - mid.astype(F32)).astype(BF16)
    return hi, mid, lo


def _tri_left(tri, a):
    return sum(jnp.dot(tri, t, preferred_element_type=F32) for t in _split3(a))


def _tri_right(a, tri):
    return sum(jnp.dot(t, tri, preferred_element_type=F32) for t in _split3(a))


def _tri_masks(n):
    r = lax.broadcasted_iota(jnp.int32, (n, n), 0)
    c = lax.broadcasted_iota(jnp.int32, (n, n), 1)
    return r >= c, r <= c


def _mod_kernel(c_ref, w_ref, b_ref, o_ref):
    s = _silu(c_ref[...]).astype(BF16)
    o_ref[...] = jnp.dot(s, w_ref[...].astype(BF16), preferred_element_type=F32) + b_ref[...]


def _modulation(cc, w_mod, b_mod):
    rows, d = cc.shape
    n = w_mod.shape[1]
    tn = WEIGHT_TN
    return pl.pallas_call(
        _mod_kernel,
        out_shape=jax.ShapeDtypeStruct((rows, n), F32),
        grid=(n // tn,),
        in_specs=[pl.BlockSpec((rows, d), lambda j: (0, 0)),
                  pl.BlockSpec((d, tn), lambda j: (0, j)),
                  pl.BlockSpec((1, tn), lambda j: (0, j))],
        out_specs=pl.BlockSpec((rows, tn), lambda j: (0, j)),
        compiler_params=_params(1),
        name="mod",
    )(cc, w_mod, b_mod.reshape(1, n))


def _wcast_kernel(wt_ref, o_ref, *, valid):
    w = wt_ref[...]
    if valid < w.shape[0]:
        w = jnp.where(lax.broadcasted_iota(jnp.int32, w.shape, 0) < valid, w, 0.0)
    o_ref[...] = w.T.astype(o_ref.dtype)


def _wcast(wt, col0, n_out, tn):
    n, d = wt.shape
    assert col0 % tn == 0 and n_out % tn == 0
    return pl.pallas_call(
        functools.partial(_wcast_kernel, valid=min(tn, n - col0)),
        out_shape=jax.ShapeDtypeStruct((d, n_out), BF16),
        grid=(n_out // tn,),
        in_specs=[pl.BlockSpec((tn, d), lambda j: (col0 // tn + j, 0))],
        out_specs=pl.BlockSpec((d, tn), lambda j: (0, j)),
        compiler_params=_params(1),
        name="wcast",
    )(wt)


def _ln_modulate(xv, shift, scale):
    mu = jnp.mean(xv, axis=-1, keepdims=True)
    xc = xv - mu
    var = jnp.mean(xc * xc, axis=-1, keepdims=True)
    return (xc * lax.rsqrt(var + LN_EPS) * (1.0 + scale) + shift).astype(BF16)


def _ln_kernel(x_ref, sh_ref, sc_ref, h_ref):
    h_ref[...] = _ln_modulate(x_ref[...], sh_ref[...], sc_ref[...])


def _inproj_kernel(xn_ref, h0_ref, sh_ref, sc_ref, w_ref, wg_ref, u_ref, g_ref, h_a, h_b, *, nsub):
    i = pl.program_id(0)
    j = pl.program_id(1)
    rs = xn_ref.shape[0]

    @pl.when((i == 0) & (j == 0))
    def _():
        h_a[...] = h0_ref[...]

    def body(h_cur, h_nxt):
        @pl.when(j == 0)
        def _():
            g_ref[...] = jnp.dot(h_cur[...], wg_ref[...], preferred_element_type=F32)

        rows = pl.ds(pl.multiple_of(jnp.minimum(j, nsub - 1) * rs, rs), rs)
        h_nxt[rows, :] = _ln_modulate(xn_ref[...], sh_ref[...], sc_ref[...])
        u_ref[...] = jnp.dot(h_cur[...], w_ref[...], preferred_element_type=F32).astype(BF16)

    pl.when(i % 2 == 0)(lambda: body(h_a, h_b))
    pl.when(i % 2 == 1)(lambda: body(h_b, h_a))


def _inproj(x2, mod3, mod_row, w_main, w_g, tm):
    rows, d = x2.shape
    nu = w_main.shape[1]
    tn = PROJ_TN if nu % PROJ_TN == 0 else PROJ_TN_SMALL
    nt, nj = rows // tm, nu // tn
    nsub = min(LN_SUBBLOCKS, 1 << (nj.bit_length() - 1))
    rs = tm // nsub
    assert nu % tn == 0 and rows % tm == 0 and tm % nsub == 0

    def mod_spec(part, row_of):
        return pl.BlockSpec((None, 1, d), lambda *ij: (row_of(ij[0]), 0, part))

    h0 = pl.pallas_call(
        _ln_kernel,
        out_shape=jax.ShapeDtypeStruct((tm, d), BF16),
        grid=(2,),
        in_specs=[pl.BlockSpec((tm // 2, d), lambda r: (r, 0)),
                  mod_spec(0, lambda r: mod_row(0)), mod_spec(1, lambda r: mod_row(0))],
        out_specs=pl.BlockSpec((tm // 2, d), lambda r: (r, 0)),
        compiler_params=_params(1),
        name="ln0",
    )(x2, mod3, mod3)

    def nxt(i):
        return jnp.minimum(i + 1, nt - 1)

    return pl.pallas_call(
        functools.partial(_inproj_kernel, nsub=nsub),
        out_shape=(jax.ShapeDtypeStruct((rows, nu), BF16),
                   jax.ShapeDtypeStruct((rows, LANES), F32)),
        grid=(nt, nj),
        in_specs=[pl.BlockSpec((rs, d), lambda i, j: (nxt(i) * nsub + jnp.minimum(j, nsub - 1), 0)),
                  pl.BlockSpec((tm, d), lambda i, j: (0, 0)),
                  mod_spec(0, lambda i: mod_row(nxt(i))), mod_spec(1, lambda i: mod_row(nxt(i))),
                  pl.BlockSpec((d, tn), lambda i, j: (0, j)),
                  pl.BlockSpec((d, LANES), lambda i, j: (0, 0))],
        out_specs=(pl.BlockSpec((tm, tn), lambda i, j: (i, j)),
                   pl.BlockSpec((tm, LANES), lambda i, j: (i, 0))),
        scratch_shapes=[pltpu.VMEM((tm, d), BF16), pltpu.VMEM((tm, d), BF16)],
        compiler_params=_params(2),
        name="inproj",
    )(x2, h0, mod3, mod3, w_main, w_g)


def _gateprep_kernel(g_ref, bias_ref, gc_ref, gr_ref, *, nh, c):
    lower, upper = _tri_masks(c)
    tril = jnp.where(lower, 1.0, 0.0).astype(BF16)
    triu = jnp.where(upper, 1.0, 0.0).astype(BF16)
    lane = lax.broadcasted_iota(jnp.int32, (c, LANES), 1)
    fwd_f = (lane >= 2 * nh) & (lane < 3 * nh)
    bwd_f = (lane >= 3 * nh) & (lane < 4 * nh)
    row = lax.broadcasted_iota(jnp.int32, (GATE_ROWS, c), 0)
    pos = lax.broadcasted_iota(jnp.int32, (GATE_ROWS, c), 1)
    fwd_r = (row >= 2 * nh) & (row < 3 * nh)
    bwd_r = (row >= 3 * nh) & (row < 4 * nh)

    for r0 in range(0, g_ref.shape[0], c):
        pre = g_ref[r0:r0 + c, :] + bias_ref[...]
        lf = _log_sigmoid(pre)
        b = jnp.where(fwd_f, _tri_left(tril, lf), jnp.where(bwd_f, _tri_left(triu, lf), 0.0))
        gc_ref[r0:r0 + c, :] = pre - pltpu.roll(b, LANES - 2 * nh, axis=1)

        pre_t = pre.T[:GATE_ROWS]
        lf_t = _log_sigmoid(pre_t)
        b_t = jnp.where(fwd_r, _tri_right(lf_t, triu), jnp.where(bwd_r, _tri_right(lf_t, tril), 0.0))
        a_t = pre_t - pltpu.roll(b_t, GATE_ROWS - 2 * nh, axis=0)
        pmax, smax = a_t, a_t
        shift = 1
        while shift < c:
            pmax = jnp.maximum(pmax, jnp.where(pos >= shift, pltpu.roll(pmax, shift, axis=1), -jnp.inf))
            smax = jnp.maximum(smax, jnp.where(pos < c - shift, pltpu.roll(smax, c - shift, axis=1), -jnp.inf))
            shift *= 2
        amax_t = pltpu.roll(jnp.where(row < nh, pmax, smax), 4 * nh, axis=0)
        gr_ref[:, r0:r0 + c] = jnp.where(row < 2 * nh, a_t, jnp.where(row < 4 * nh, b_t, amax_t))


def _gateprep(g, bias, nh, chunk):
    rows = g.shape[0]
    assert 6 * nh <= GATE_ROWS
    blk = GATE_CHUNKS * chunk if rows % (GATE_CHUNKS * chunk) == 0 else chunk
    bias_row = jnp.zeros((1, LANES), F32).at[0, :4 * nh].set(bias.reshape(-1))
    return pl.pallas_call(
        functools.partial(_gateprep_kernel, nh=nh, c=chunk),
        out_shape=(jax.ShapeDtypeStruct((rows, LANES), F32),
                   jax.ShapeDtypeStruct((GATE_ROWS, rows), F32)),
        grid=(rows // blk,),
        in_specs=[pl.BlockSpec((blk, LANES), lambda i: (i, 0)),
                  pl.BlockSpec((1, LANES), lambda i: (0, 0))],
        out_specs=(pl.BlockSpec((blk, LANES), lambda i: (i, 0)),
                   pl.BlockSpec((GATE_ROWS, blk), lambda i: (0, i))),
        compiler_params=_params(1),
        name="gateprep",
    )(g, bias_row)


def _conv_kernel(x_ref, w_ref, b_ref, o_ref, *, rows, width, taps):
    cb = x_ref.shape[-1]
    j = pl.program_id(1)
    scale = jnp.where(j >= pl.num_programs(1) // 2, ML_DH ** -0.5, 1.0).astype(F32)
    w = w_ref[...]
    bias = b_ref[...]
    col = lax.broadcasted_iota(jnp.int32, (width, cb), 0)
    first = col == 0
    last = col == width - 1

    def one_row(r):
        left = mid = right = None
        for ki in taps:
            rr = r + (ki - 1)
            valid = jnp.where((rr >= 0) & (rr < rows), 1.0, 0.0).astype(F32)
            wk = w[3 * ki:3 * ki + 3, :] * valid
            start = pl.multiple_of(jnp.clip(rr, 0, rows - 1) * width, width)
            xv = x_ref[pl.ds(start, width), :].astype(F32)
            terms = [xv * wk[kj:kj + 1, :] for kj in range(3)]
            left, mid, right = terms if left is None else (left + terms[0], mid + terms[1], right + terms[2])
        out = (mid + jnp.where(first, 0.0, pltpu.roll(left, 1, axis=0))
               + jnp.where(last, 0.0, pltpu.roll(right, width - 1, axis=0)) + bias)
        dst = pl.multiple_of(r * width, width)
        o_ref[pl.ds(dst, width), :] = (_silu(out) * scale).astype(o_ref.dtype)

    def body(r, carry):
        one_row(r)
        return carry

    lax.fori_loop(0, rows, body, 0)


def _short_conv(u, col0, n_ch, conv_w, conv_b, grid):
    bsz, t, _ = u.shape
    cb = COL_BLOCK
    rows, width, taps = (t // GRID_W, GRID_W, (0, 1, 2)) if grid else (1, t, (1,))
    return pl.pallas_call(
        functools.partial(_conv_kernel, rows=rows, width=width, taps=taps),
        out_shape=jax.ShapeDtypeStruct((bsz, t, n_ch), BF16),
        grid=(bsz, n_ch // cb),
        in_specs=[pl.BlockSpec((None, t, cb), lambda b, j: (b, 0, col0 // cb + j)),
                  pl.BlockSpec((CONV_K * CONV_K, cb), lambda b, j: (0, j)),
                  pl.BlockSpec((1, cb), lambda b, j: (0, j))],
        out_specs=pl.BlockSpec((None, t, cb), lambda b, j: (b, 0, j)),
        compiler_params=_params(2),
        name="conv_grid" if grid else "conv_seq",
    )(u, conv_w.reshape(CONV_K * CONV_K, n_ch).astype(F32), conv_b.reshape(1, n_ch).astype(F32))


def _hg_prepare(qraw, z, lb, fwd):
    lower, upper = _tri_masks(z.shape[0])
    tri = jnp.where(lower if fwd else upper, 1.0, 0.0).astype(BF16)
    f = lb + (1.0 - lb) * jax.nn.sigmoid(z)
    b2 = _tri_left(tri, jnp.log(f) * LOG2E)
    c2 = b2 - jnp.log(1.0 - f) * LOG2E
    return _silu(qraw), b2, c2


def _block_diag(a):
    r, w = a.shape
    zero = jnp.zeros((r, w // 2), a.dtype)
    return jnp.concatenate([jnp.concatenate([a[:, :w // 2], zero], axis=1),
                            jnp.concatenate([zero, a[:, w // 2:]], axis=1)], axis=0)


def _hg_scores(q_ref, b_ref, c_ref, fwd):
    c, w = q_ref.shape
    sub = HG_SUB
    nb = c // sub
    assert 2 * c == LANES and c & (c - 1) == 0
    row = lax.broadcasted_iota(jnp.int32, (sub, LANES), 0)
    col = lax.broadcasted_iota(jnp.int32, (sub, LANES), 1) & (c - 1)
    src = col & (sub - 1)
    code = jnp.where((src <= row) if fwd else (src >= row), col, -1)
    kr = lax.broadcasted_iota(jnp.int32, (w, LANES), 0)
    kl = lax.broadcasted_iota(jnp.int32, (w, LANES), 1)
    sel = jnp.where((kr >= w // 2) == (kl >= c), 1.0, 0.0).astype(BF16)

    units = []
    for j in range(nb):
        r0 = j * sub
        qb = q_ref[r0:r0 + sub, :]
        bb = b_ref[r0:r0 + sub, :]
        cb = c_ref[r0:r0 + sub, :]
        for s in range(sub):
            units.append(qb * jnp.exp2(bb - cb[s:s + 1, :]))
    red = jnp.dot(jnp.concatenate(units, axis=0).astype(BF16), sel, preferred_element_type=F32)

    blocks = []
    for j in range(nb):
        r0 = j * sub
        blk = jnp.zeros((sub, LANES), F32)
        for s in range(sub):
            u0 = (j * sub + s) * sub
            blk = jnp.where(code == r0 + s, red[u0:u0 + sub], blk)
        lo, hi = (0, r0) if fwd else (r0 + sub, c)
        if hi > lo:
            beta = b_ref[pl.ds(r0 - 1 if fwd else r0 + sub, 1), :]
            pieces = [jnp.zeros((lo, w), F32)] if lo else []
            pieces.append(jnp.exp2(beta - c_ref[lo:hi, :]))
            if hi < c:
                pieces.append(jnp.zeros((c - hi, w), F32))
            kh = jnp.concatenate(pieces, axis=0).astype(BF16)
            qh = q_ref[r0:r0 + sub, :] * jnp.exp2(b_ref[r0:r0 + sub, :] - beta)
            off = lax.dot_general(_block_diag(qh).astype(BF16), kh, _NT, preferred_element_type=F32)
            blk = blk + jnp.concatenate([off[:sub], off[sub:]], axis=1)
        blocks.append(blk)
    return jnp.concatenate(blocks, axis=0).astype(BF16)


def _hg_scores_factored(q_ref, b_ref, c_ref, fwd, sub):
    c, w = q_ref.shape
    nb = c // sub
    row = lax.broadcasted_iota(jnp.int32, (sub, LANES), 0)
    col = lax.broadcasted_iota(jnp.int32, (sub, LANES), 1) & (c - 1)
    blocks = []
    for j in range(nb):
        r0 = j * sub
        lo, hi = (0, r0 + sub) if fwd else (r0, c)
        edge = r0 - 1 if fwd else r0 + sub
        beta = b_ref[pl.ds(edge, 1), :] if 0 <= edge < c else jnp.zeros((1, w), F32)
        pieces = [jnp.zeros((lo, w), F32)] if lo else []
        pieces.append(jnp.exp2(beta - c_ref[lo:hi, :]))
        if hi < c:
            pieces.append(jnp.zeros((c - hi, w), F32))
        kh = jnp.concatenate(pieces, axis=0).astype(BF16)
        qh = q_ref[r0:r0 + sub, :] * jnp.exp2(b_ref[r0:r0 + sub, :] - beta)
        off = lax.dot_general(_block_diag(qh).astype(BF16), kh, _NT, preferred_element_type=F32)
        blk = jnp.concatenate([off[:sub], off[sub:]], axis=1)
        causal = (col <= row + r0) if fwd else (col >= row + r0)
        blocks.append(jnp.where(causal, blk, 0.0))
    return jnp.concatenate(blocks, axis=0).astype(BF16)


def _hg_state_update(st_refs, v, k_dec, dec, qe=None):
    hw = v.shape[1] // 2
    outs = []
    for h, st_ref in enumerate(st_refs):
        ls = slice(h * hw, (h + 1) * hw)
        st = st_ref[...]
        if qe is not None:
            outs.append(lax.dot_general(qe[:, ls], st.astype(BF16), _NT, preferred_element_type=F32))
        st_ref[...] = st * dec[:, ls] + lax.dot_general(v[:, ls], k_dec[:, ls], _TN, preferred_element_type=F32)
    return outs


def _hgrn2_kernel(cq, cff, cfb, ci, aq, aff, afb, ai, az, lb_ref, nw_ref, y_ref,
                  acc_f, acc_b, st_ref, q_sc, b_sc, c_sc, a_sc, qe_sc, kd_sc, dec_sc, *, slots):
    c = HG_CHUNK
    ncc = cq.shape[0] // c
    nc = aq.shape[0] // c
    hw = HG_DK
    assert nc % 4 == 0
    st_ref[...] = jnp.zeros_like(st_ref)
    accs = (acc_f, acc_b)
    lbs = []
    for d in range(2):
        hg = lb_ref[d * slots:(d + 1) * slots, :]
        e = jnp.exp(hg - jnp.max(hg, axis=0, keepdims=True))
        lbs.append(e[0:1, :] / jnp.sum(e, axis=0, keepdims=True))

    def chunk_rows(d, k, n):
        return pl.ds(pl.multiple_of((k if d == 0 else n - 1 - k) * c, c), c)

    def states(d):
        return st_ref.at[2 * d], st_ref.at[2 * d + 1]

    def ctx_body(k, carry):
        for d in range(2):
            rows = chunk_rows(d, k, ncc)
            _, b2, c2 = _hg_prepare(cq[rows, :].astype(F32), (cff, cfb)[d][rows, :].astype(F32), lbs[d], d == 0)
            end = c - 1 if d == 0 else 0
            b_end = b2[end:end + 1, :]
            _hg_state_update(states(d), ci[rows, :], jnp.exp2(b_end - c2).astype(BF16), jnp.exp2(b_end))
        return carry

    def gates(d, k, slot):
        rows = chunk_rows(d, k, nc)
        q, b2, c2 = _hg_prepare(aq[rows, :].astype(F32), (aff, afb)[d][rows, :].astype(F32), lbs[d], d == 0)
        q_sc[slot, d] = q
        b_sc[slot, d] = b2
        c_sc[slot, d] = c2

    def scores(d, slot, block):
        qr, br, cr = q_sc.at[slot, d], b_sc.at[slot, d], c_sc.at[slot, d]
        end = c - 1 if d == 0 else 0
        b_end = br[end:end + 1, :]
        qe_sc[slot, d] = (qr[...] * jnp.exp2(br[...])).astype(BF16)
        kd_sc[slot, d] = jnp.exp2(b_end - cr[...]).astype(BF16)
        dec_sc[slot, d] = jnp.exp2(b_end)
        if block is None:
            a_sc[slot, d] = _hg_scores(qr, br, cr, d == 0)
        else:
            a_sc[slot, d] = _hg_scores_factored(qr, br, cr, d == 0, block)

    def outputs(d, k, slot):
        rows = chunk_rows(d, k, nc)
        v = ai[rows, :]
        outs = _hg_state_update(states(d), v, kd_sc[slot, d], dec_sc[slot, d], qe_sc[slot, d])
        accs[d][rows, :] = (jnp.concatenate(outs, axis=1)
                            + jnp.dot(a_sc[slot, d], _block_diag(v), preferred_element_type=F32))

    def step(k, slot, block, do_out=True, do_gates=True, do_scores=True):
        for d in range(2):
            if do_gates:
                gates(d, k, slot)
            if do_scores:
                scores(d, 1 - slot, block)
            if do_out:
                outputs(d, k - 2, slot)

    def scan(block):
        def quad_body(kq, carry):
            for i in range(4):
                step(4 * kq + i, i % 2, block)
            return carry

        step(0, 0, block, do_out=False, do_scores=False)
        step(1, 1, block, do_out=False)
        step(2, 0, block)
        step(3, 1, block)
        lax.fori_loop(1, nc // 4, quad_body, 0)
        step(nc, 0, block, do_gates=False)
        step(nc + 1, 1, block, do_gates=False, do_scores=False)

    def final_body(i, carry):
        rows = pl.ds(pl.multiple_of(i * c, c), c)
        tot = acc_f[rows, :] + acc_b[rows, :]
        gate = _silu(az[rows, :].astype(F32)) * nw_ref[...]
        for h in range(2):
            ls = slice(h * hw, (h + 1) * hw)
            th = tot[:, ls]
            ms = jnp.mean(th * th, axis=-1, keepdims=True)
            y_ref[rows, ls] = (th * lax.rsqrt(ms + NORM_EPS) * gate[:, ls]).astype(y_ref.dtype)
        return carry

    lax.fori_loop(0, ncc, ctx_body, 0, unroll=True)
    unit = jnp.max(-jnp.log(jnp.minimum(lbs[0], lbs[1])) * LOG2E)
    wide = c * unit <= HG_SAFE_LOG2
    narrow = jnp.logical_and(jnp.logical_not(wide), HG_SUB * unit <= HG_SAFE_LOG2)
    pl.when(wide)(lambda: scan(c))
    pl.when(narrow)(lambda: scan(HG_SUB))
    pl.when(jnp.logical_not(jnp.logical_or(wide, narrow)))(lambda: scan(None))
    lax.fori_loop(0, nc, final_body, 0, unroll=4)


def _hgrn2(u_x, u_c, w_a, hg_lb, hg_norm):
    bsz, t, _ = u_x.shape
    tc = u_c.shape[1]
    cb = 2 * HG_DK
    c = HG_CHUNK
    assert cb == COL_BLOCK
    seg = w_a // cb
    slots = hg_lb.shape[1]
    lb2 = hg_lb.reshape(2 * slots, w_a).astype(F32)

    def col(k):
        return lambda b, p: (b, 0, k * seg + p)

    ctx_specs = [pl.BlockSpec((None, tc, cb), col(k)) for k in (0, 1, 2, 3)]
    lat_specs = [pl.BlockSpec((None, t, cb), col(k)) for k in (0, 1, 2, 3, 4)]
    return pl.pallas_call(
        functools.partial(_hgrn2_kernel, slots=slots),
        out_shape=jax.ShapeDtypeStruct((bsz, t, w_a), BF16),
        grid=(bsz, seg),
        in_specs=ctx_specs + lat_specs + [
            pl.BlockSpec((2 * slots, cb), lambda b, p: (0, p)),
            pl.BlockSpec((1, cb), lambda b, p: (0, p))],
        out_specs=pl.BlockSpec((None, t, cb), lambda b, p: (b, 0, p)),
        scratch_shapes=[pltpu.VMEM((t, cb), F32),
                        pltpu.VMEM((t, cb), F32),
                        pltpu.VMEM((4, HG_DK, HG_DK), F32),
                        pltpu.VMEM((2, 2, c, cb), F32),
                        pltpu.VMEM((2, 2, c, cb), F32),
                        pltpu.VMEM((2, 2, c, cb), F32),
                        pltpu.VMEM((2, 2, c, 2 * c), BF16),
                        pltpu.VMEM((2, 2, c, cb), BF16),
                        pltpu.VMEM((2, 2, c, cb), BF16),
                        pltpu.VMEM((2, 2, 1, cb), F32)],
        compiler_params=_params(2),
        name="hgrn2",
    )(u_c, u_c, u_c, u_c, u_x, u_x, u_x, u_x, u_x, lb2, hg_norm.reshape(1, w_a).astype(F32))


def _ml_gates(a_row, b_row, amax_row, m_prev, fwd):
    end = a_row.shape[1] - 1 if fwd else 0
    b_end = b_row[:, end:end + 1]
    big_m = jnp.maximum(m_prev, amax_row)
    m_row = b_row + big_m
    m_new = m_row[:, end:end + 1]
    w_s = jnp.exp(b_end + a_row - m_new)
    decay = jnp.exp(b_end + m_prev - m_new)
    return big_m, m_row, m_new, w_s, decay


def _ml_state_lhs(v, w_s):
    vt = v.astype(F32).T
    lhs = jnp.concatenate([vt * w_s, jnp.broadcast_to(w_s, (ML_PAD, vt.shape[1]))], axis=0)
    return vt, lhs.astype(BF16)


def _mlstm_kernel(cq, ck, cv, cgc, cgr, xq, xk, xv, xo, xz, xgc, xgr, nw_ref, y_ref,
                  acc_f, acc_b, st_ref, wt_sc, vt_sc, lhs_sc, row_sc):
    c = ML_CHUNK
    ncc = cq.shape[0] // c
    nc = xq.shape[0] // c
    dh = xq.shape[1]
    assert nc % 4 == 0
    head = pl.program_id(1)
    nh = pl.num_programs(1)
    accs = (acc_f, acc_b)
    st_ref[...] = jnp.zeros_like(st_ref)
    lane = lax.broadcasted_iota(jnp.int32, (c, LANES), 1)
    lower, upper = _tri_masks(c)

    def chunk_rows(d, k, n):
        return pl.ds(pl.multiple_of((k if d == 0 else n - 1 - k) * c, c), c)

    def gate_rows(grr, d, rows):
        slot = d * nh + head
        return (grr[pl.ds(slot, 1), rows], grr[pl.ds(2 * nh + slot, 1), rows],
                grr[pl.ds(4 * nh + slot, 1), rows])

    def ctx_body(k, ms):
        out = []
        for d in range(2):
            rows = chunk_rows(d, k, ncc)
            _, _, m_new, w_s, decay = _ml_gates(*gate_rows(cgr, d, rows), ms[d], d == 0)
            _, lhs = _ml_state_lhs(cv[rows, :], w_s)
            st_ref[d] = decay * st_ref[d] + jnp.dot(lhs, ck[rows, :], preferred_element_type=F32)
            out.append(m_new)
        return tuple(out)

    def score_stage(d, k, slot, m_prev):
        rows = chunk_rows(d, k, nc)
        kq = lax.dot_general(xk[rows, :], xq[rows, :], _NT, preferred_element_type=F32)
        big_m, m_row, m_new, w_s, decay = _ml_gates(*gate_rows(xgr, d, rows), m_prev, d == 0)
        a_col = jnp.sum(jnp.where(lane == d * nh + head, xgc[rows, :], 0.0), axis=1, keepdims=True)
        mask = upper if d == 0 else lower
        w_t = jnp.exp(jnp.where(mask, a_col - big_m, -jnp.inf)) * kq
        wt_sc[slot, d] = w_t
        vt, lhs = _ml_state_lhs(xv[rows, :], w_s)
        lhs_sc[slot, d] = lhs
        vt_sc[slot, d] = vt.astype(BF16)
        row_sc[slot, d, 0:1, :] = jnp.sum(w_t, axis=0, keepdims=True)
        row_sc[slot, d, 1:2, :] = jnp.exp(m_prev - big_m)
        row_sc[slot, d, 2:3, :] = jnp.exp(-m_row)
        row_sc[slot, d, 3:4, :] = jnp.broadcast_to(decay, (1, c))
        return m_new

    def output_stage(d, k, slot):
        rows = chunk_rows(d, k, nc)
        st = st_ref[d]
        r = lax.dot_general(st.astype(BF16), xq[rows, :], _NT, preferred_element_type=F32)
        w_inter = row_sc[slot, d, 1:2, :]
        den = w_inter * r[dh:dh + 1, :] + row_sc[slot, d, 0:1, :]
        inv = 1.0 / jnp.maximum(jnp.abs(den), row_sc[slot, d, 2:3, :])
        accs[d][:, rows] = (r[:dh, :] * (w_inter * inv)
                            + jnp.dot(vt_sc[slot, d], (wt_sc[slot, d] * inv).astype(BF16),
                                      preferred_element_type=F32))
        st_ref[d] = (row_sc[slot, d, 3:4, 0:1] * st
                     + jnp.dot(lhs_sc[slot, d], xk[rows, :], preferred_element_type=F32))

    def quad_body(kq, ms):
        for i in range(4):
            slot = i % 2
            k = 4 * kq + i
            nxt = jnp.minimum(k + 1, nc - 1)
            new_ms = []
            for d in range(2):
                new_ms.append(score_stage(d, nxt, 1 - slot, ms[d]))
                output_stage(d, k, slot)
            ms = tuple(new_ms)
        return ms

    def final_body(i, carry):
        rows = pl.ds(pl.multiple_of(i * c, c), c)
        tot = acc_f[:, rows] + acc_b[:, rows]
        mu = jnp.mean(tot, axis=0, keepdims=True)
        tc_ = tot - mu
        var = jnp.mean(tc_ * tc_, axis=0, keepdims=True)
        normed = (tc_ * lax.rsqrt(var + NORM_EPS)).T * nw_ref[...]
        gate = jax.nn.sigmoid(xo[rows, :].astype(F32)) * _silu(xz[rows, :].astype(F32))
        y_ref[rows, :] = (normed * gate).astype(y_ref.dtype)
        return carry

    ms = (jnp.zeros((1, 1), F32), jnp.zeros((1, 1), F32))
    ms = lax.fori_loop(0, ncc, ctx_body, ms)
    ms = tuple(score_stage(d, 0, 0, ms[d]) for d in range(2))
    lax.fori_loop(0, nc // 4, quad_body, ms)
    lax.fori_loop(0, nc, final_body, 0, unroll=2)


def _mlstm(qk_x, qk_c, u_x, u_c, v_col, o_col, z_col, gates_x, gates_c, ml_norm, w_b):
    bsz, t, _ = u_x.shape
    tc = u_c.shape[1]
    nh = w_b // ML_DH
    cb = ML_DH
    assert cb == COL_BLOCK and t % (4 * ML_CHUNK) == 0 and tc % ML_CHUNK == 0

    def col(k):
        return lambda b, h: (b, 0, k + h)

    def seq_specs(n, with_gates):
        specs = [pl.BlockSpec((None, n, cb), col(0)),
                 pl.BlockSpec((None, n, cb), col(nh)),
                 pl.BlockSpec((None, n, cb), col(v_col // cb))]
        if with_gates:
            specs += [pl.BlockSpec((None, n, cb), col(o_col // cb)),
                      pl.BlockSpec((None, n, cb), col(z_col // cb))]
        specs += [pl.BlockSpec((n, LANES), lambda b, h: (b, 0)),
                  pl.BlockSpec((GATE_ROWS, n), lambda b, h: (0, b))]
        return specs

    return pl.pallas_call(
        _mlstm_kernel,
        out_shape=jax.ShapeDtypeStruct((bsz, t, w_b), BF16),
        grid=(bsz, nh),
        in_specs=seq_specs(tc, False) + seq_specs(t, True) + [pl.BlockSpec((1, cb), lambda b, h: (0, h))],
        out_specs=pl.BlockSpec((None, t, cb), lambda b, h: (b, 0, h)),
        scratch_shapes=[pltpu.VMEM((cb, t), F32),
                        pltpu.VMEM((cb, t), F32),
                        pltpu.VMEM((2, ML_DH + ML_PAD, ML_DH), F32),
                        pltpu.VMEM((2, 2, ML_CHUNK, ML_CHUNK), F32),
                        pltpu.VMEM((2, 2, ML_DH, ML_CHUNK), BF16),
                        pltpu.VMEM((2, 2, ML_DH + ML_PAD, ML_CHUNK), BF16),
                        pltpu.VMEM((2, 2, 8, ML_CHUNK), F32)],
        compiler_params=_params(2),
        name="mlstm",
    )(qk_c, qk_c, u_c, gates_c[0], gates_c[1],
      qk_x, qk_x, u_x, u_x, u_x, gates_x[0], gates_x[1],
      ml_norm.reshape(1, w_b).astype(F32))


def _outproj_kernel(ya_ref, yb_ref, x_ref, gate_ref, wa_ref, wb_ref, g_ref, b_ref, o_ref, *, sub):
    for r0 in range(0, x_ref.shape[0], sub):
        rows = slice(r0, r0 + sub)
        y = (jnp.dot(ya_ref[rows, :], wa_ref[...], preferred_element_type=F32)
             + jnp.dot(yb_ref[rows, :], wb_ref[...], preferred_element_type=F32))
        r = ALPHA * x_ref[rows, :] + gate_ref[...] * y
        mu = jnp.mean(r, axis=-1, keepdims=True)
        rc = r - mu
        var = jnp.mean(rc * rc, axis=-1, keepdims=True)
        o_ref[rows, :] = rc * lax.rsqrt(var + LN_EPS) * g_ref[...] + b_ref[...]


def _outproj(y_a, y_b, x2, mod3, t, w_oa, w_ob, ln_g, ln_b):
    rows, d = x2.shape
    tm = OUT_TM
    assert rows % tm == 0 and t % tm == 0
    wa, wb = y_a.shape[1], y_b.shape[1]
    return pl.pallas_call(
        functools.partial(_outproj_kernel, sub=OUT_SUB),
        out_shape=jax.ShapeDtypeStruct((rows, d), F32),
        grid=(rows // tm,),
        in_specs=[pl.BlockSpec((tm, wa), lambda i: (i, 0)),
                  pl.BlockSpec((tm, wb), lambda i: (i, 0)),
                  pl.BlockSpec((tm, d), lambda i: (i, 0)),
                  pl.BlockSpec((None, 1, d), lambda i: ((i * tm) // t, 0, 2)),
                  pl.BlockSpec((wa, d), lambda i: (0, 0)),
                  pl.BlockSpec((wb, d), lambda i: (0, 0)),
                  pl.BlockSpec((1, d), lambda i: (0, 0)),
                  pl.BlockSpec((1, d), lambda i: (0, 0))],
        out_specs=pl.BlockSpec((tm, d), lambda i: (i, 0)),
        compiler_params=_params(1),
        name="outproj",
    )(y_a, y_b, x2, mod3, w_oa, w_ob, ln_g.reshape(1, d).astype(F32), ln_b.reshape(1, d).astype(F32))


def kernel(x, c, ctx, c_ctx, w_mod, b_mod, w_in, conv_w, conv_b, hg_lb, ml_gate_b, hg_norm_w,
           ml_norm_w, w_out, ln_g, ln_b):
    bsz, t, d = x.shape
    tc = ctx.shape[1]
    assert w_in.shape[0] == DEPTH and hg_lb.shape[1] == DEPTH + 1
    d_inner = w_out.shape[1]
    w_a = hg_lb.shape[2]
    w_b = d_inner - w_a
    nh = ml_gate_b.shape[2]
    nu = 5 * w_a + 5 * w_b
    assert nh * ML_DH == w_b and w_in.shape[2] == nu + 4 * nh and t % GRID_W == 0

    mod_rows = 8 * ((bsz + 1 + 7) // 8)
    cc = jnp.zeros((mod_rows, d), F32).at[:bsz].set(c).at[bsz].set(c_ctx)
    mod3 = _modulation(cc, w_mod[0], b_mod[0]).reshape(mod_rows, 1, 3 * d)

    w_in_t = w_in[0].T
    w_main = _wcast(w_in_t, 0, nu, WEIGHT_TN)
    w_g = _wcast(w_in_t, nu, LANES, LANES)
    tm = min(PROJ_TM, t)
    u_x, g_x = _inproj(x.reshape(bsz * t, d), mod3, lambda i: (i * tm) // t, w_main, w_g, tm)
    u_c, g_c = _inproj(ctx.reshape(bsz * tc, d), mod3, lambda i: bsz, w_main, w_g, min(PROJ_TM, bsz * tc))
    u_x = u_x.reshape(bsz, t, nu)
    u_c = u_c.reshape(bsz, tc, nu)

    gates_x = _gateprep(g_x, ml_gate_b[0], nh, ML_CHUNK)
    gates_c = _gateprep(g_c, ml_gate_b[0], nh, ML_CHUNK)

    qk_col = 5 * w_a
    qk_x = _short_conv(u_x, qk_col, 2 * w_b, conv_w[0], conv_b[0], grid=True)
    qk_c = _short_conv(u_c, qk_col, 2 * w_b, conv_w[0], conv_b[0], grid=False)

    y_a = _hgrn2(u_x, u_c, w_a, hg_lb, hg_norm_w[0])
    y_b = _mlstm(qk_x, qk_c, u_x, u_c, qk_col + 2 * w_b, qk_col + 3 * w_b, qk_col + 4 * w_b,
                 gates_x, gates_c, ml_norm_w[0], w_b)

    w_o = w_out[0].astype(BF16)
    out = _outproj(y_a.reshape(bsz * t, w_a), y_b.reshape(bsz * t, w_b), x.reshape(bsz * t, d), mod3, t,
                   w_o[:w_a], w_o[w_a:], ln_g[0], ln_b[0])
    return out.reshape(bsz, t, d)
```

```python
import functools

import jax
import jax.numpy as jnp
from jax import lax
from jax.experimental import pallas as pl
from jax.experimental.pallas import tpu as pltpu

F32 = jnp.float32
BF16 = jnp.bfloat16

LN_EPS = 1e-5
NORM_EPS = 1e-6
DEPTH = 1
ALPHA = (2 * DEPTH) ** 0.25

HG_DK = 128
ML_DH = 256
GRID_W = 64
CONV_K = 3

LANES = 128
HG_CHUNK = 64
HG_SUB = 8
HG_SAFE_LOG2 = 96.0
LOG2E = 1.4426950408889634
ML_CHUNK = 256
ML_PAD = 16
GATE_ROWS = 24
COL_BLOCK = 256
VMEM_LIMIT = 56 * 1024 * 1024

PROJ_TM = 1024
PROJ_TN = 1024
PROJ_TN_SMALL = 512
LN_SUBBLOCKS = 8
OUT_TM = 512
OUT_SUB = 256
WEIGHT_TN = 512
GATE_CHUNKS = 8

_NT = (((1,), (1,)), ((), ()))
_TN = (((0,), (0,)), ((), ()))


def _params(n_grid):
    return pltpu.CompilerParams(dimension_semantics=("arbitrary",) * n_grid,
                                vmem_limit_bytes=VMEM_LIMIT)


def _silu(a):
    return a * jax.nn.sigmoid(a)


def _log_sigmoid(a):
    return jnp.minimum(a, 0.0) - jnp.log1p(jnp.exp(-jnp.abs(a)))


def _split3(a):
    hi = a.astype(BF16)
    r1 = a - hi.astype(F32)
    mid = r1.astype(BF16)
    lo = (r1 - mid.astype(F32)).astype(BF16)
    return hi, mid, lo


def _tri_left(tri, a):
    return sum(jnp.dot(tri, t, preferred_element_type=F32) for t in _split3(a))


def _tri_right(a, tri):
    return sum(jnp.dot(t, tri, preferred_element_type=F32) for t in _split3(a))


def _tri_masks(n):
    r = lax.broadcasted_iota(jnp.int32, (n, n), 0)
    c = lax.broadcasted_iota(jnp.int32, (n, n), 1)
    return r >= c, r <= c


def _mod_kernel(c_ref, w_ref, b_ref, o_ref):
    s = _silu(c_ref[...]).astype(BF16)
    o_ref[...] = jnp.dot(s, w_ref[...].astype(BF16), preferred_element_type=F32) + b_ref[...]


def _modulation(cc, w_mod, b_mod):
    rows, d = cc.shape
    n = w_mod.shape[1]
    tn = WEIGHT_TN
    return pl.pallas_call(
        _mod_kernel,
        out_shape=jax.ShapeDtypeStruct((rows, n), F32),
        grid=(n // tn,),
        in_specs=[pl.BlockSpec((rows, d), lambda j: (0, 0)),
                  pl.BlockSpec((d, tn), lambda j: (0, j)),
                  pl.BlockSpec((1, tn), lambda j: (0, j))],
        out_specs=pl.BlockSpec((rows, tn), lambda j: (0, j)),
        compiler_params=_params(1),
        name="mod",
    )(cc, w_mod, b_mod.reshape(1, n))


def _wcast_kernel(wt_ref, o_ref, *, valid):
    w = wt_ref[...]
    if valid < w.shape[0]:
        w = jnp.where(lax.broadcasted_iota(jnp.int32, w.shape, 0) < valid, w, 0.0)
    o_ref[...] = w.T.astype(o_ref.dtype)


def _wcast(wt, col0, n_out, tn):
    n, d = wt.shape
    assert col0 % tn == 0 and n_out % tn == 0
    return pl.pallas_call(
        functools.partial(_wcast_kernel, valid=min(tn, n - col0)),
        out_shape=jax.ShapeDtypeStruct((d, n_out), BF16),
        grid=(n_out // tn,),
        in_specs=[pl.BlockSpec((tn, d), lambda j: (col0 // tn + j, 0))],
        out_specs=pl.BlockSpec((d, tn), lambda j: (0, j)),
        compiler_params=_params(1),
        name="wcast",
    )(wt)


def _ln_modulate(xv, shift, scale):
    mu = jnp.mean(xv, axis=-1, keepdims=True)
    xc = xv - mu
    var = jnp.mean(xc * xc, axis=-1, keepdims=True)
    return (xc * lax.rsqrt(var + LN_EPS) * (1.0 + scale) + shift).astype(BF16)


def _ln_kernel(x_ref, sh_ref, sc_ref, h_ref):
    h_ref[...] = _ln_modulate(x_ref[...], sh_ref[...], sc_ref[...])


def _inproj_kernel(xn_ref, h0_ref, sh_ref, sc_ref, w_ref, wg_ref, u_ref, g_ref, h_a, h_b, *, nsub):
    i = pl.program_id(0)
    j = pl.program_id(1)
    rs = xn_ref.shape[0]

    @pl.when((i == 0) & (j == 0))
    def _():
        h_a[...] = h0_ref[...]

    def body(h_cur, h_nxt):
        @pl.when(j == 0)
        def _():
            g_ref[...] = jnp.dot(h_cur[...], wg_ref[...], preferred_element_type=F32)

        rows = pl.ds(pl.multiple_of(jnp.minimum(j, nsub - 1) * rs, rs), rs)
        h_nxt[rows, :] = _ln_modulate(xn_ref[...], sh_ref[...], sc_ref[...])
        u_ref[...] = jnp.dot(h_cur[...], w_ref[...], preferred_element_type=F32).astype(BF16)

    pl.when(i % 2 == 0)(lambda: body(h_a, h_b))
    pl.when(i % 2 == 1)(lambda: body(h_b, h_a))


def _inproj(x2, mod3, mod_row, w_main, w_g, tm):
    rows, d = x2.shape
    nu = w_main.shape[1]
    tn = PROJ_TN if nu % PROJ_TN == 0 else PROJ_TN_SMALL
    nt, nj = rows // tm, nu // tn
    nsub = min(LN_SUBBLOCKS, 1 << (nj.bit_length() - 1))
    rs = tm // nsub
    assert nu % tn == 0 and rows % tm == 0 and tm % nsub == 0

    def mod_spec(part, row_of):
        return pl.BlockSpec((None, 1, d), lambda *ij: (row_of(ij[0]), 0, part))

    h0 = pl.pallas_call(
        _ln_kernel,
        out_shape=jax.ShapeDtypeStruct((tm, d), BF16),
        grid=(2,),
        in_specs=[pl.BlockSpec((tm // 2, d), lambda r: (r, 0)),
                  mod_spec(0, lambda r: mod_row(0)), mod_spec(1, lambda r: mod_row(0))],
        out_specs=pl.BlockSpec((tm // 2, d), lambda r: (r, 0)),
        compiler_params=_params(1),
        name="ln0",
    )(x2, mod3, mod3)

    def nxt(i):
        return jnp.minimum(i + 1, nt - 1)

    return pl.pallas_call(
        functools.partial(_inproj_kernel, nsub=nsub),
        out_shape=(jax.ShapeDtypeStruct((rows, nu), BF16),
                   jax.ShapeDtypeStruct((rows, LANES), F32)),
        grid=(nt, nj),
        in_specs=[pl.BlockSpec((rs, d), lambda i, j: (nxt(i) * nsub + jnp.minimum(j, nsub - 1), 0)),
                  pl.BlockSpec((tm, d), lambda i, j: (0, 0)),
                  mod_spec(0, lambda i: mod_row(nxt(i))), mod_spec(1, lambda i: mod_row(nxt(i))),
                  pl.BlockSpec((d, tn), lambda i, j: (0, j)),
                  pl.BlockSpec((d, LANES), lambda i, j: (0, 0))],
        out_specs=(pl.BlockSpec((tm, tn), lambda i, j: (i, j)),
                   pl.BlockSpec((tm, LANES), lambda i, j: (i, 0))),
        scratch_shapes=[pltpu.VMEM((tm, d), BF16), pltpu.VMEM((tm, d), BF16)],
        compiler_params=_params(2),
        name="inproj",
    )(x2, h0, mod3, mod3, w_main, w_g)


def _gateprep_kernel(g_ref, bias_ref, gc_ref, gr_ref, *, nh, c):
    lower, upper = _tri_masks(c)
    tril = jnp.where(lower, 1.0, 0.0).astype(BF16)
    triu = jnp.where(upper, 1.0, 0.0).astype(BF16)
    lane = lax.broadcasted_iota(jnp.int32, (c, LANES), 1)
    fwd_f = (lane >= 2 * nh) & (lane < 3 * nh)
    bwd_f = (lane >= 3 * nh) & (lane < 4 * nh)
    row = lax.broadcasted_iota(jnp.int32, (GATE_ROWS, c), 0)
    pos = lax.broadcasted_iota(jnp.int32, (GATE_ROWS, c), 1)
    fwd_r = (row >= 2 * nh) & (row < 3 * nh)
    bwd_r = (row >= 3 * nh) & (row < 4 * nh)

    for r0 in range(0, g_ref.shape[0], c):
        pre = g_ref[r0:r0 + c, :] + bias_ref[...]
        lf = _log_sigmoid(pre)
        b = jnp.where(fwd_f, _tri_left(tril, lf), jnp.where(bwd_f, _tri_left(triu, lf), 0.0))
        gc_ref[r0:r0 + c, :] = pre - pltpu.roll(b, LANES - 2 * nh, axis=1)

        pre_t = pre.T[:GATE_ROWS]
        lf_t = _log_sigmoid(pre_t)
        b_t = jnp.where(fwd_r, _tri_right(lf_t, triu), jnp.where(bwd_r, _tri_right(lf_t, tril), 0.0))
        a_t = pre_t - pltpu.roll(b_t, GATE_ROWS - 2 * nh, axis=0)
        pmax, smax = a_t, a_t
        shift = 1
        while shift < c:
            pmax = jnp.maximum(pmax, jnp.where(pos >= shift, pltpu.roll(pmax, shift, axis=1), -jnp.inf))
            smax = jnp.maximum(smax, jnp.where(pos < c - shift, pltpu.roll(smax, c - shift, axis=1), -jnp.inf))
            shift *= 2
        amax_t = pltpu.roll(jnp.where(row < nh, pmax, smax), 4 * nh, axis=0)
        gr_ref[:, r0:r0 + c] = jnp.where(row < 2 * nh, a_t, jnp.where(row < 4 * nh, b_t, amax_t))


def _gateprep(g, bias, nh, chunk):
    rows = g.shape[0]
    assert 6 * nh <= GATE_ROWS
    blk = next(n * chunk for n in (GATE_CHUNKS, 4, 2, 1) if rows % (n * chunk) == 0)
    bias_row = jnp.zeros((1, LANES), F32).at[0, :4 * nh].set(bias.reshape(-1))
    return pl.pallas_call(
        functools.partial(_gateprep_kernel, nh=nh, c=chunk),
        out_shape=(jax.ShapeDtypeStruct((rows, LANES), F32),
                   jax.ShapeDtypeStruct((GATE_ROWS, rows), F32)),
        grid=(rows // blk,),
        in_specs=[pl.BlockSpec((blk, LANES), lambda i: (i, 0)),
                  pl.BlockSpec((1, LANES), lambda i: (0, 0))],
        out_specs=(pl.BlockSpec((blk, LANES), lambda i: (i, 0)),
                   pl.BlockSpec((GATE_ROWS, blk), lambda i: (0, i))),
        compiler_params=_params(1),
        name="gateprep",
    )(g, bias_row)


def _conv_kernel(x_ref, w_ref, b_ref, o_ref, *, rows, width, taps):
    cb = x_ref.shape[-1]
    j = pl.program_id(1)
    scale = jnp.where(j >= pl.num_programs(1) // 2, ML_DH ** -0.5, 1.0).astype(F32)
    w = w_ref[...]
    bias = b_ref[...]
    col = lax.broadcasted_iota(jnp.int32, (width, cb), 0)
    first = col == 0
    last = col == width - 1

    def one_row(r):
        left = mid = right = None
        for ki in taps:
            rr = r + (ki - 1)
            valid = jnp.where((rr >= 0) & (rr < rows), 1.0, 0.0).astype(F32)
            wk = w[3 * ki:3 * ki + 3, :] * valid
            start = pl.multiple_of(jnp.clip(rr, 0, rows - 1) * width, width)
            xv = x_ref[pl.ds(start, width), :].astype(F32)
            terms = [xv * wk[kj:kj + 1, :] for kj in range(3)]
            left, mid, right = terms if left is None else (left + terms[0], mid + terms[1], right + terms[2])
        out = (mid + jnp.where(first, 0.0, pltpu.roll(left, 1, axis=0))
               + jnp.where(last, 0.0, pltpu.roll(right, width - 1, axis=0)) + bias)
        dst = pl.multiple_of(r * width, width)
        o_ref[pl.ds(dst, width), :] = (_silu(out) * scale).astype(o_ref.dtype)

    def body(r, carry):
        one_row(r)
        return carry

    lax.fori_loop(0, rows, body, 0)


def _short_conv(u, col0, n_ch, conv_w, conv_b, grid):
    bsz, t, _ = u.shape
    cb = COL_BLOCK
    rows, width, taps = (t // GRID_W, GRID_W, (0, 1, 2)) if grid else (1, t, (1,))
    return pl.pallas_call(
        functools.partial(_conv_kernel, rows=rows, width=width, taps=taps),
        out_shape=jax.ShapeDtypeStruct((bsz, t, n_ch), BF16),
        grid=(bsz, n_ch // cb),
        in_specs=[pl.BlockSpec((None, t, cb), lambda b, j: (b, 0, col0 // cb + j)),
                  pl.BlockSpec((CONV_K * CONV_K, cb), lambda b, j: (0, j)),
                  pl.BlockSpec((1, cb), lambda b, j: (0, j))],
        out_specs=pl.BlockSpec((None, t, cb), lambda b, j: (b, 0, j)),
        compiler_params=_params(2),
        name="conv_grid" if grid else "conv_seq",
    )(u, conv_w.reshape(CONV_K * CONV_K, n_ch).astype(F32), conv_b.reshape(1, n_ch).astype(F32))


def _hg_prepare(qraw, z, lb, fwd):
    lower, upper = _tri_masks(z.shape[0])
    tri = jnp.where(lower if fwd else upper, 1.0, 0.0).astype(BF16)
    f = lb + (1.0 - lb) * jax.nn.sigmoid(z)
    b2 = _tri_left(tri, jnp.log(f) * LOG2E)
    c2 = b2 - jnp.log(1.0 - f) * LOG2E
    return _silu(qraw), b2, c2


def _block_diag(a):
    r, w = a.shape
    zero = jnp.zeros((r, w // 2), a.dtype)
    return jnp.concatenate([jnp.concatenate([a[:, :w // 2], zero], axis=1),
                            jnp.concatenate([zero, a[:, w // 2:]], axis=1)], axis=0)


def _hg_scores(q_ref, b_ref, c_ref, fwd):
    c, w = q_ref.shape
    sub = HG_SUB
    nb = c // sub
    assert 2 * c == LANES and c & (c - 1) == 0
    row = lax.broadcasted_iota(jnp.int32, (sub, LANES), 0)
    col = lax.broadcasted_iota(jnp.int32, (sub, LANES), 1) & (c - 1)
    src = col & (sub - 1)
    code = jnp.where((src <= row) if fwd else (src >= row), col, -1)
    kr = lax.broadcasted_iota(jnp.int32, (w, LANES), 0)
    kl = lax.broadcasted_iota(jnp.int32, (w, LANES), 1)
    sel = jnp.where((kr >= w // 2) == (kl >= c), 1.0, 0.0).astype(BF16)

    units = []
    for j in range(nb):
        r0 = j * sub
        qb = q_ref[r0:r0 + sub, :]
        bb = b_ref[r0:r0 + sub, :]
        cb = c_ref[r0:r0 + sub, :]
        for s in range(sub):
            units.append(qb * jnp.exp2(bb - cb[s:s + 1, :]))
    red = jnp.dot(jnp.concatenate(units, axis=0).astype(BF16), sel, preferred_element_type=F32)

    blocks = []
    for j in range(nb):
        r0 = j * sub
        blk = jnp.zeros((sub, LANES), F32)
        for s in range(sub):
            u0 = (j * sub + s) * sub
            blk = jnp.where(code == r0 + s, red[u0:u0 + sub], blk)
        lo, hi = (0, r0) if fwd else (r0 + sub, c)
        if hi > lo:
            beta = b_ref[pl.ds(r0 - 1 if fwd else r0 + sub, 1), :]
            pieces = [jnp.zeros((lo, w), F32)] if lo else []
            pieces.append(jnp.exp2(beta - c_ref[lo:hi, :]))
            if hi < c:
                pieces.append(jnp.zeros((c - hi, w), F32))
            kh = jnp.concatenate(pieces, axis=0).astype(BF16)
            qh = q_ref[r0:r0 + sub, :] * jnp.exp2(b_ref[r0:r0 + sub, :] - beta)
            off = lax.dot_general(_block_diag(qh).astype(BF16), kh, _NT, preferred_element_type=F32)
            blk = blk + jnp.concatenate([off[:sub], off[sub:]], axis=1)
        blocks.append(blk)
    return jnp.concatenate(blocks, axis=0).astype(BF16)


def _hg_scores_factored(q_ref, b_ref, c_ref, fwd, sub):
    c, w = q_ref.shape
    nb = c // sub
    row = lax.broadcasted_iota(jnp.int32, (sub, LANES), 0)
    col = lax.broadcasted_iota(jnp.int32, (sub, LANES), 1) & (c - 1)
    blocks = []
    for j in range(nb):
        r0 = j * sub
        lo, hi = (0, r0 + sub) if fwd else (r0, c)
        edge = r0 - 1 if fwd else r0 + sub
        beta = b_ref[pl.ds(edge, 1), :] if 0 <= edge < c else jnp.zeros((1, w), F32)
        pieces = [jnp.zeros((lo, w), F32)] if lo else []
        pieces.append(jnp.exp2(beta - c_ref[lo:hi, :]))
        if hi < c:
            pieces.append(jnp.zeros((c - hi, w), F32))
        kh = jnp.concatenate(pieces, axis=0).astype(BF16)
        qh = q_ref[r0:r0 + sub, :] * jnp.exp2(b_ref[r0:r0 + sub, :] - beta)
        off = lax.dot_general(_block_diag(qh).astype(BF16), kh, _NT, preferred_element_type=F32)
        blk = jnp.concatenate([off[:sub], off[sub:]], axis=1)
        causal = (col <= row + r0) if fwd else (col >= row + r0)
        blocks.append(jnp.where(causal, blk, 0.0))
    return jnp.concatenate(blocks, axis=0).astype(BF16)


def _hg_state_update(st_refs, v, k_dec, dec, qe=None):
    hw = v.shape[1] // 2
    outs = []
    for h, st_ref in enumerate(st_refs):
        ls = slice(h * hw, (h + 1) * hw)
        st = st_ref[...]
        if qe is not None:
            outs.append(lax.dot_general(qe[:, ls], st.astype(BF16), _NT, preferred_element_type=F32))
        st_ref[...] = st * dec[:, ls] + lax.dot_general(v[:, ls], k_dec[:, ls], _TN, preferred_element_type=F32)
    return outs


def _hgrn2_kernel(cq, cff, cfb, ci, aq, aff, afb, ai, az, lb_ref, nw_ref, y_ref,
                  acc_f, acc_b, st_ref, q_sc, b_sc, c_sc, a_sc, qe_sc, kd_sc, dec_sc, *, slots):
    c = HG_CHUNK
    ncc = cq.shape[0] // c
    nc = aq.shape[0] // c
    hw = HG_DK
    assert nc % 4 == 0
    st_ref[...] = jnp.zeros_like(st_ref)
    accs = (acc_f, acc_b)
    lbs = []
    for d in range(2):
        hg = lb_ref[d * slots:(d + 1) * slots, :]
        e = jnp.exp(hg - jnp.max(hg, axis=0, keepdims=True))
        lbs.append(e[0:1, :] / jnp.sum(e, axis=0, keepdims=True))

    def chunk_rows(d, k, n):
        return pl.ds(pl.multiple_of((k if d == 0 else n - 1 - k) * c, c), c)

    def states(d):
        return st_ref.at[2 * d], st_ref.at[2 * d + 1]

    def ctx_body(k, carry):
        for d in range(2):
            rows = chunk_rows(d, k, ncc)
            _, b2, c2 = _hg_prepare(cq[rows, :].astype(F32), (cff, cfb)[d][rows, :].astype(F32), lbs[d], d == 0)
            end = c - 1 if d == 0 else 0
            b_end = b2[end:end + 1, :]
            _hg_state_update(states(d), ci[rows, :], jnp.exp2(b_end - c2).astype(BF16), jnp.exp2(b_end))
        return carry

    def gates(d, k, slot):
        rows = chunk_rows(d, k, nc)
        q, b2, c2 = _hg_prepare(aq[rows, :].astype(F32), (aff, afb)[d][rows, :].astype(F32), lbs[d], d == 0)
        q_sc[slot, d] = q
        b_sc[slot, d] = b2
        c_sc[slot, d] = c2

    def scores(d, slot, block):
        qr, br, cr = q_sc.at[slot, d], b_sc.at[slot, d], c_sc.at[slot, d]
        end = c - 1 if d == 0 else 0
        b_end = br[end:end + 1, :]
        qe_sc[slot, d] = (qr[...] * jnp.exp2(br[...])).astype(BF16)
        kd_sc[slot, d] = jnp.exp2(b_end - cr[...]).astype(BF16)
        dec_sc[slot, d] = jnp.exp2(b_end)
        if block is None:
            a_sc[slot, d] = _hg_scores(qr, br, cr, d == 0)
        else:
            a_sc[slot, d] = _hg_scores_factored(qr, br, cr, d == 0, block)

    def outputs(d, k, slot):
        rows = chunk_rows(d, k, nc)
        v = ai[rows, :]
        outs = _hg_state_update(states(d), v, kd_sc[slot, d], dec_sc[slot, d], qe_sc[slot, d])
        accs[d][rows, :] = (jnp.concatenate(outs, axis=1)
                            + jnp.dot(a_sc[slot, d], _block_diag(v), preferred_element_type=F32))

    def step(k, slot, block, do_out=True, do_gates=True, do_scores=True):
        for d in range(2):
            if do_gates:
                gates(d, k, slot)
            if do_scores:
                scores(d, 1 - slot, block)
            if do_out:
                outputs(d, k - 2, slot)

    def scan(block):
        def quad_body(kq, carry):
            for i in range(4):
                step(4 * kq + i, i % 2, block)
            return carry

        step(0, 0, block, do_out=False, do_scores=False)
        step(1, 1, block, do_out=False)
        step(2, 0, block)
        step(3, 1, block)
        lax.fori_loop(1, nc // 4, quad_body, 0)
        step(nc, 0, block, do_gates=False)
        step(nc + 1, 1, block, do_gates=False, do_scores=False)

    def final_body(i, carry):
        rows = pl.ds(pl.multiple_of(i * c, c), c)
        tot = acc_f[rows, :] + acc_b[rows, :]
        gate = _silu(az[rows, :].astype(F32)) * nw_ref[...]
        for h in range(2):
            ls = slice(h * hw, (h + 1) * hw)
            th = tot[:, ls]
            ms = jnp.mean(th * th, axis=-1, keepdims=True)
            y_ref[rows, ls] = (th * lax.rsqrt(ms + NORM_EPS) * gate[:, ls]).astype(y_ref.dtype)
        return carry

    lax.fori_loop(0, ncc, ctx_body, 0, unroll=True)
    unit = jnp.max(-jnp.log(jnp.minimum(lbs[0], lbs[1])) * LOG2E)
    wide = c * unit <= HG_SAFE_LOG2
    narrow = jnp.logical_and(jnp.logical_not(wide), HG_SUB * unit <= HG_SAFE_LOG2)
    pl.when(wide)(lambda: scan(c))
    pl.when(narrow)(lambda: scan(HG_SUB))
    pl.when(jnp.logical_not(jnp.logical_or(wide, narrow)))(lambda: scan(None))
    lax.fori_loop(0, nc, final_body, 0, unroll=8)


def _hgrn2(u_x, u_c, w_a, hg_lb, hg_norm):
    bsz, t, _ = u_x.shape
    tc = u_c.shape[1]
    cb = 2 * HG_DK
    c = HG_CHUNK
    assert cb == COL_BLOCK
    seg = w_a // cb
    slots = hg_lb.shape[1]
    lb2 = hg_lb.reshape(2 * slots, w_a).astype(F32)

    def col(k):
        return lambda b, p: (b, 0, k * seg + p)

    ctx_specs = [pl.BlockSpec((None, tc, cb), col(k)) for k in (0, 1, 2, 3)]
    lat_specs = [pl.BlockSpec((None, t, cb), col(k)) for k in (0, 1, 2, 3, 4)]
    return pl.pallas_call(
        functools.partial(_hgrn2_kernel, slots=slots),
        out_shape=jax.ShapeDtypeStruct((bsz, t, w_a), BF16),
        grid=(bsz, seg),
        in_specs=ctx_specs + lat_specs + [
            pl.BlockSpec((2 * slots, cb), lambda b, p: (0, p)),
            pl.BlockSpec((1, cb), lambda b, p: (0, p))],
        out_specs=pl.BlockSpec((None, t, cb), lambda b, p: (b, 0, p)),
        scratch_shapes=[pltpu.VMEM((t, cb), F32),
                        pltpu.VMEM((t, cb), F32),
                        pltpu.VMEM((4, HG_DK, HG_DK), F32),
                        pltpu.VMEM((2, 2, c, cb), F32),
                        pltpu.VMEM((2, 2, c, cb), F32),
                        pltpu.VMEM((2, 2, c, cb), F32),
                        pltpu.VMEM((2, 2, c, 2 * c), BF16),
                        pltpu.VMEM((2, 2, c, cb), BF16),
                        pltpu.VMEM((2, 2, c, cb), BF16),
                        pltpu.VMEM((2, 2, 1, cb), F32)],
        compiler_params=_params(2),
        name="hgrn2",
    )(u_c, u_c, u_c, u_c, u_x, u_x, u_x, u_x, u_x, lb2, hg_norm.reshape(1, w_a).astype(F32))


def _ml_gates(a_row, b_row, amax_row, m_prev, fwd):
    end = a_row.shape[1] - 1 if fwd else 0
    b_end = b_row[:, end:end + 1]
    big_m = jnp.maximum(m_prev, amax_row)
    m_row = b_row + big_m
    m_new = m_row[:, end:end + 1]
    w_s = jnp.exp(b_end + a_row - m_new)
    decay = jnp.exp(b_end + m_prev - m_new)
    return big_m, m_row, m_new, w_s, decay


def _ml_state_lhs(v, w_s):
    vt = v.astype(F32).T
    lhs = jnp.concatenate([vt * w_s, jnp.broadcast_to(w_s, (ML_PAD, vt.shape[1]))], axis=0)
    return vt, lhs.astype(BF16)


def _mlstm_kernel(cq, ck, cv, cgc, cgr, xq, xk, xv, xo, xz, xgc, xgr, nw_ref, y_ref,
                  acc_f, acc_b, st_ref, wt_sc, vt_sc, lhs_sc, row_sc):
    c = ML_CHUNK
    ncc = cq.shape[0] // c
    nc = xq.shape[0] // c
    dh = xq.shape[1]
    assert nc % 4 == 0
    head = pl.program_id(1)
    nh = pl.num_programs(1)
    accs = (acc_f, acc_b)
    st_ref[...] = jnp.zeros_like(st_ref)
    lane = lax.broadcasted_iota(jnp.int32, (c, LANES), 1)
    lower, upper = _tri_masks(c)

    def chunk_rows(d, k, n):
        return pl.ds(pl.multiple_of((k if d == 0 else n - 1 - k) * c, c), c)

    def gate_rows(grr, d, rows):
        slot = d * nh + head
        return (grr[pl.ds(slot, 1), rows], grr[pl.ds(2 * nh + slot, 1), rows],
                grr[pl.ds(4 * nh + slot, 1), rows])

    def ctx_body(k, ms):
        out = []
        for d in range(2):
            rows = chunk_rows(d, k, ncc)
            _, _, m_new, w_s, decay = _ml_gates(*gate_rows(cgr, d, rows), ms[d], d == 0)
            _, lhs = _ml_state_lhs(cv[rows, :], w_s)
            st_ref[d] = decay * st_ref[d] + jnp.dot(lhs, ck[rows, :], preferred_element_type=F32)
            out.append(m_new)
        return tuple(out)

    def score_stage(d, k, slot, m_prev):
        rows = chunk_rows(d, k, nc)
        kq = lax.dot_general(xk[rows, :], xq[rows, :], _NT, preferred_element_type=F32)
        big_m, m_row, m_new, w_s, decay = _ml_gates(*gate_rows(xgr, d, rows), m_prev, d == 0)
        a_col = jnp.sum(jnp.where(lane == d * nh + head, xgc[rows, :], 0.0), axis=1, keepdims=True)
        mask = upper if d == 0 else lower
        w_t = jnp.exp(jnp.where(mask, a_col - big_m, -jnp.inf)) * kq
        wt_sc[slot, d] = w_t
        vt, lhs = _ml_state_lhs(xv[rows, :], w_s)
        lhs_sc[slot, d] = lhs
        vt_sc[slot, d] = vt.astype(BF16)
        row_sc[slot, d, 0:1, :] = jnp.sum(w_t, axis=0, keepdims=True)
        row_sc[slot, d, 1:2, :] = jnp.exp(m_prev - big_m)
        row_sc[slot, d, 2:3, :] = jnp.exp(-m_row)
        row_sc[slot, d, 3:4, :] = jnp.broadcast_to(decay, (1, c))
        return m_new

    def output_stage(d, k, slot):
        rows = chunk_rows(d, k, nc)
        st = st_ref[d]
        r = lax.dot_general(st.astype(BF16), xq[rows, :], _NT, preferred_element_type=F32)
        w_inter = row_sc[slot, d, 1:2, :]
        den = w_inter * r[dh:dh + 1, :] + row_sc[slot, d, 0:1, :]
        inv = 1.0 / jnp.maximum(jnp.abs(den), row_sc[slot, d, 2:3, :])
        accs[d][:, rows] = (r[:dh, :] * (w_inter * inv)
                            + jnp.dot(vt_sc[slot, d], (wt_sc[slot, d] * inv).astype(BF16),
                                      preferred_element_type=F32))
        st_ref[d] = (row_sc[slot, d, 3:4, 0:1] * st
                     + jnp.dot(lhs_sc[slot, d], xk[rows, :], preferred_element_type=F32))

    def quad_body(kq, ms):
        for i in range(4):
            slot = i % 2
            k = 4 * kq + i
            nxt = jnp.minimum(k + 1, nc - 1)
            new_ms = []
            for d in range(2):
                new_ms.append(score_stage(d, nxt, 1 - slot, ms[d]))
                output_stage(d, k, slot)
            ms = tuple(new_ms)
        return ms

    def final_body(i, carry):
        rows = pl.ds(pl.multiple_of(i * c, c), c)
        tot = acc_f[:, rows] + acc_b[:, rows]
        mu = jnp.mean(tot, axis=0, keepdims=True)
        tc_ = tot - mu
        var = jnp.mean(tc_ * tc_, axis=0, keepdims=True)
        normed = (tc_ * lax.rsqrt(var + NORM_EPS)).T * nw_ref[...]
        gate = jax.nn.sigmoid(xo[rows, :].astype(F32)) * _silu(xz[rows, :].astype(F32))
        y_ref[rows, :] = (normed * gate).astype(y_ref.dtype)
        return carry

    ms = (jnp.zeros((1, 1), F32), jnp.zeros((1, 1), F32))
    ms = lax.fori_loop(0, ncc, ctx_body, ms)
    ms = tuple(score_stage(d, 0, 0, ms[d]) for d in range(2))
    lax.fori_loop(0, nc // 4, quad_body, ms)
    lax.fori_loop(0, nc, final_body, 0, unroll=4)


def _mlstm(qk_x, qk_c, u_x, u_c, v_col, o_col, z_col, gates_x, gates_c, ml_norm, w_b):
    bsz, t, _ = u_x.shape
    tc = u_c.shape[1]
    nh = w_b // ML_DH
    cb = ML_DH
    assert cb == COL_BLOCK and t % (4 * ML_CHUNK) == 0 and tc % ML_CHUNK == 0

    def col(k):
        return lambda b, h: (b, 0, k + h)

    def seq_specs(n, with_gates):
        specs = [pl.BlockSpec((None, n, cb), col(0)),
                 pl.BlockSpec((None, n, cb), col(nh)),
                 pl.BlockSpec((None, n, cb), col(v_col // cb))]
        if with_gates:
            specs += [pl.BlockSpec((None, n, cb), col(o_col // cb)),
                      pl.BlockSpec((None, n, cb), col(z_col // cb))]
        specs += [pl.BlockSpec((n, LANES), lambda b, h: (b, 0)),
                  pl.BlockSpec((GATE_ROWS, n), lambda b, h: (0, b))]
        return specs

    return pl.pallas_call(
        _mlstm_kernel,
        out_shape=jax.ShapeDtypeStruct((bsz, t, w_b), BF16),
        grid=(bsz, nh),
        in_specs=seq_specs(tc, False) + seq_specs(t, True) + [pl.BlockSpec((1, cb), lambda b, h: (0, h))],
        out_specs=pl.BlockSpec((None, t, cb), lambda b, h: (b, 0, h)),
        scratch_shapes=[pltpu.VMEM((cb, t), F32),
                        pltpu.VMEM((cb, t), F32),
                        pltpu.VMEM((2, ML_DH + ML_PAD, ML_DH), F32),
                        pltpu.VMEM((2, 2, ML_CHUNK, ML_CHUNK), F32),
                        pltpu.VMEM((2, 2, ML_DH, ML_CHUNK), BF16),
                        pltpu.VMEM((2, 2, ML_DH + ML_PAD, ML_CHUNK), BF16),
                        pltpu.VMEM((2, 2, 8, ML_CHUNK), F32)],
        compiler_params=_params(2),
        name="mlstm",
    )(qk_c, qk_c, u_c, gates_c[0], gates_c[1],
      qk_x, qk_x, u_x, u_x, u_x, gates_x[0], gates_x[1],
      ml_norm.reshape(1, w_b).astype(F32))


def _outproj_kernel(ya_ref, yb_ref, x_ref, gate_ref, wa_ref, wb_ref, g_ref, b_ref, o_ref, *, sub):
    for r0 in range(0, x_ref.shape[0], sub):
        rows = slice(r0, r0 + sub)
        y = (jnp.dot(ya_ref[rows, :], wa_ref[...], preferred_element_type=F32)
             + jnp.dot(yb_ref[rows, :], wb_ref[...], preferred_element_type=F32))
        r = ALPHA * x_ref[rows, :] + gate_ref[...] * y
        mu = jnp.mean(r, axis=-1, keepdims=True)
        rc = r - mu
        var = jnp.mean(rc * rc, axis=-1, keepdims=True)
        o_ref[rows, :] = rc * lax.rsqrt(var + LN_EPS) * g_ref[...] + b_ref[...]


def _outproj(y_a, y_b, x2, mod3, t, w_oa, w_ob, ln_g, ln_b):
    rows, d = x2.shape
    tm = OUT_TM
    assert rows % tm == 0 and t % tm == 0
    wa, wb = y_a.shape[1], y_b.shape[1]
    return pl.pallas_call(
        functools.partial(_outproj_kernel, sub=OUT_SUB),
        out_shape=jax.ShapeDtypeStruct((rows, d), F32),
        grid=(rows // tm,),
        in_specs=[pl.BlockSpec((tm, wa), lambda i: (i, 0)),
                  pl.BlockSpec((tm, wb), lambda i: (i, 0)),
                  pl.BlockSpec((tm, d), lambda i: (i, 0)),
                  pl.BlockSpec((None, 1, d), lambda i: ((i * tm) // t, 0, 2)),
                  pl.BlockSpec((wa, d), lambda i: (0, 0)),
                  pl.BlockSpec((wb, d), lambda i: (0, 0)),
                  pl.BlockSpec((1, d), lambda i: (0, 0)),
                  pl.BlockSpec((1, d), lambda i: (0, 0))],
        out_specs=pl.BlockSpec((tm, d), lambda i: (i, 0)),
        compiler_params=_params(1),
        name="outproj",
    )(y_a, y_b, x2, mod3, w_oa, w_ob, ln_g.reshape(1, d).astype(F32), ln_b.reshape(1, d).astype(F32))


def kernel(x, c, ctx, c_ctx, w_mod, b_mod, w_in, conv_w, conv_b, hg_lb, ml_gate_b, hg_norm_w,
           ml_norm_w, w_out, ln_g, ln_b):
    bsz, t, d = x.shape
    tc = ctx.shape[1]
    assert w_in.shape[0] == DEPTH and hg_lb.shape[1] == DEPTH + 1
    d_inner = w_out.shape[1]
    w_a = hg_lb.shape[2]
    w_b = d_inner - w_a
    nh = ml_gate_b.shape[2]
    nu = 5 * w_a + 5 * w_b
    assert nh * ML_DH == w_b and w_in.shape[2] == nu + 4 * nh and t % GRID_W == 0

    mod_rows = 8 * ((bsz + 1 + 7) // 8)
    cc = jnp.zeros((mod_rows, d), F32).at[:bsz].set(c).at[bsz].set(c_ctx)
    mod3 = _modulation(cc, w_mod[0], b_mod[0]).reshape(mod_rows, 1, 3 * d)

    w_in_t = w_in[0].T
    w_main = _wcast(w_in_t, 0, nu, WEIGHT_TN)
    w_g = _wcast(w_in_t, nu, LANES, LANES)
    tm = min(PROJ_TM, t)
    u_x, g_x = _inproj(x.reshape(bsz * t, d), mod3, lambda i: (i * tm) // t, w_main, w_g, tm)
    u_c, g_c = _inproj(ctx.reshape(bsz * tc, d), mod3, lambda i: bsz, w_main, w_g, min(PROJ_TM, bsz * tc))
    u_x = u_x.reshape(bsz, t, nu)
    u_c = u_c.reshape(bsz, tc, nu)

    gates_x = _gateprep(g_x, ml_gate_b[0], nh, ML_CHUNK)
    gates_c = _gateprep(g_c, ml_gate_b[0], nh, ML_CHUNK)

    qk_col = 5 * w_a
    qk_x = _short_conv(u_x, qk_col, 2 * w_b, conv_w[0], conv_b[0], grid=True)
    qk_c = _short_conv(u_c, qk_col, 2 * w_b, conv_w[0], conv_b[0], grid=False)

    y_a = _hgrn2(u_x, u_c, w_a, hg_lb, hg_norm_w[0])
    y_b = _mlstm(qk_x, qk_c, u_x, u_c, qk_col + 2 * w_b, qk_col + 3 * w_b, qk_col + 4 * w_b,
                 gates_x, gates_c, ml_norm_w[0], w_b)

    w_o = w_out[0].astype(BF16)
    out = _outproj(y_a.reshape(bsz * t, w_a), y_b.reshape(bsz * t, w_b), x.reshape(bsz * t, d), mod3, t,
                   w_o[:w_a], w_o[w_a:], ln_g[0], ln_b[0])
    return out.reshape(bsz, t, d)
```

```python
import functools

import jax
import jax.numpy as jnp
from jax import lax
from jax.experimental import pallas as pl
from jax.experimental.pallas import tpu as pltpu

F32 = jnp.float32
BF16 = jnp.bfloat16

LN_EPS = 1e-5
NORM_EPS = 1e-6
DEPTH = 1
ALPHA = (2 * DEPTH) ** 0.25

HG_DK = 128
ML_DH = 256
GRID_W = 64
CONV_K = 3

LANES = 128
HG_CHUNK = 64
HG_SUB = 8
HG_SAFE_LOG2 = 96.0
LOG2E = 1.4426950408889634
ML_CHUNK = 256
ML_PAD = 16
GATE_ROWS = 24
COL_BLOCK = 256
VMEM_LIMIT = 56 * 1024 * 1024

PROJ_TM = 1024
PROJ_TN = 1024
PROJ_TN_SMALL = 512
LN_SUBBLOCKS = 8
OUT_TM = 512
OUT_SUB = 256
WEIGHT_TN = 512
GATE_CHUNKS = 8

_NT = (((1,), (1,)), ((), ()))
_TN = (((0,), (0,)), ((), ()))


def _params(n_grid):
    return pltpu.CompilerParams(dimension_semantics=("arbitrary",) * n_grid,
                                vmem_limit_bytes=VMEM_LIMIT)


def _silu(a):
    return a * jax.nn.sigmoid(a)


def _log_sigmoid(a):
    return jnp.minimum(a, 0.0) - jnp.log1p(jnp.exp(-jnp.abs(a)))


def _split3(a):
    hi = a.astype(BF16)
    r1 = a - hi.astype(F32)
    mid = r1.astype(BF16)
    lo = (r1 - mid.astype(F32)).astype(BF16)
    return hi, mid, lo


def _tri_left(tri, a):
    return sum(jnp.dot(tri, t, preferred_element_type=F32) for t in _split3(a))


def _tri_right(a, tri):
    return sum(jnp.dot(t, tri, preferred_element_type=F32) for t in _split3(a))


def _tri_masks(n):
    r = lax.broadcasted_iota(jnp.int32, (n, n), 0)
    c = lax.broadcasted_iota(jnp.int32, (n, n), 1)
    return r >= c, r <= c


def _mod_kernel(c_ref, w_ref, b_ref, o_ref):
    s = _silu(c_ref[...]).astype(BF16)
    o_ref[...] = jnp.dot(s, w_ref[...].astype(BF16), preferred_element_type=F32) + b_ref[...]


def _modulation(cc, w_mod, b_mod):
    rows, d = cc.shape
    n = w_mod.shape[1]
    tn = WEIGHT_TN
    return pl.pallas_call(
        _mod_kernel,
        out_shape=jax.ShapeDtypeStruct((rows, n), F32),
        grid=(n // tn,),
        in_specs=[pl.BlockSpec((rows, d), lambda j: (0, 0)),
                  pl.BlockSpec((d, tn), lambda j: (0, j)),
                  pl.BlockSpec((1, tn), lambda j: (0, j))],
        out_specs=pl.BlockSpec((rows, tn), lambda j: (0, j)),
        compiler_params=_params(1),
        name="mod",
    )(cc, w_mod, b_mod.reshape(1, n))


def _wcast_kernel(wt_ref, o_ref, *, valid):
    w = wt_ref[...]
    if valid < w.shape[0]:
        w = jnp.where(lax.broadcasted_iota(jnp.int32, w.shape, 0) < valid, w, 0.0)
    o_ref[...] = w.T.astype(o_ref.dtype)


def _wcast(wt, col0, n_out, tn):
    n, d = wt.shape
    assert col0 % tn == 0 and n_out % tn == 0
    return pl.pallas_call(
        functools.partial(_wcast_kernel, valid=min(tn, n - col0)),
        out_shape=jax.ShapeDtypeStruct((d, n_out), BF16),
        grid=(n_out // tn,),
        in_specs=[pl.BlockSpec((tn, d), lambda j: (col0 // tn + j, 0))],
        out_specs=pl.BlockSpec((d, tn), lambda j: (0, j)),
        compiler_params=_params(1),
        name="wcast",
    )(wt)


def _ln_modulate(xv, shift, scale):
    mu = jnp.mean(xv, axis=-1, keepdims=True)
    xc = xv - mu
    var = jnp.mean(xc * xc, axis=-1, keepdims=True)
    return (xc * lax.rsqrt(var + LN_EPS) * (1.0 + scale) + shift).astype(BF16)


def _ln_kernel(x_ref, sh_ref, sc_ref, h_ref):
    h_ref[...] = _ln_modulate(x_ref[...], sh_ref[...], sc_ref[...])


def _inproj_kernel(xn_ref, h0_ref, sh_ref, sc_ref, w_ref, wg_ref, u_ref, g_ref, h_a, h_b, *, nsub):
    i = pl.program_id(0)
    j = pl.program_id(1)
    rs = xn_ref.shape[0]

    @pl.when((i == 0) & (j == 0))
    def _():
        h_a[...] = h0_ref[...]

    def body(h_cur, h_nxt):
        @pl.when(j == 0)
        def _():
            g_ref[...] = jnp.dot(h_cur[...], wg_ref[...], preferred_element_type=F32)

        rows = pl.ds(pl.multiple_of(jnp.minimum(j, nsub - 1) * rs, rs), rs)
        h_nxt[rows, :] = _ln_modulate(xn_ref[...], sh_ref[...], sc_ref[...])
        u_ref[...] = jnp.dot(h_cur[...], w_ref[...], preferred_element_type=F32).astype(BF16)

    pl.when(i % 2 == 0)(lambda: body(h_a, h_b))
    pl.when(i % 2 == 1)(lambda: body(h_b, h_a))


def _inproj(x2, mod3, mod_row, w_main, w_g, tm):
    rows, d = x2.shape
    nu = w_main.shape[1]
    tn = PROJ_TN if nu % PROJ_TN == 0 else PROJ_TN_SMALL
    nt, nj = rows // tm, nu // tn
    nsub = min(LN_SUBBLOCKS, 1 << (nj.bit_length() - 1))
    rs = tm // nsub
    assert nu % tn == 0 and rows % tm == 0 and tm % nsub == 0

    def mod_spec(part, row_of):
        return pl.BlockSpec((None, 1, d), lambda *ij: (row_of(ij[0]), 0, part))

    h0 = pl.pallas_call(
        _ln_kernel,
        out_shape=jax.ShapeDtypeStruct((tm, d), BF16),
        grid=(2,),
        in_specs=[pl.BlockSpec((tm // 2, d), lambda r: (r, 0)),
                  mod_spec(0, lambda r: mod_row(0)), mod_spec(1, lambda r: mod_row(0))],
        out_specs=pl.BlockSpec((tm // 2, d), lambda r: (r, 0)),
        compiler_params=_params(1),
        name="ln0",
    )(x2, mod3, mod3)

    def nxt(i):
        return jnp.minimum(i + 1, nt - 1)

    return pl.pallas_call(
        functools.partial(_inproj_kernel, nsub=nsub),
        out_shape=(jax.ShapeDtypeStruct((rows, nu), BF16),
                   jax.ShapeDtypeStruct((rows, LANES), F32)),
        grid=(nt, nj),
        in_specs=[pl.BlockSpec((rs, d), lambda i, j: (nxt(i) * nsub + jnp.minimum(j, nsub - 1), 0)),
                  pl.BlockSpec((tm, d), lambda i, j: (0, 0)),
                  mod_spec(0, lambda i: mod_row(nxt(i))), mod_spec(1, lambda i: mod_row(nxt(i))),
                  pl.BlockSpec((d, tn), lambda i, j: (0, j)),
                  pl.BlockSpec((d, LANES), lambda i, j: (0, 0))],
        out_specs=(pl.BlockSpec((tm, tn), lambda i, j: (i, j)),
                   pl.BlockSpec((tm, LANES), lambda i, j: (i, 0))),
        scratch_shapes=[pltpu.VMEM((tm, d), BF16), pltpu.VMEM((tm, d), BF16)],
        compiler_params=_params(2),
        name="inproj",
    )(x2, h0, mod3, mod3, w_main, w_g)


def _gateprep_kernel(g_ref, bias_ref, gc_ref, gr_ref, *, nh, c):
    lower, upper = _tri_masks(c)
    tril = jnp.where(lower, 1.0, 0.0).astype(BF16)
    triu = jnp.where(upper, 1.0, 0.0).astype(BF16)
    lane = lax.broadcasted_iota(jnp.int32, (c, LANES), 1)
    fwd_f = (lane >= 2 * nh) & (lane < 3 * nh)
    bwd_f = (lane >= 3 * nh) & (lane < 4 * nh)
    row = lax.broadcasted_iota(jnp.int32, (GATE_ROWS, c), 0)
    pos = lax.broadcasted_iota(jnp.int32, (GATE_ROWS, c), 1)
    fwd_r = (row >= 2 * nh) & (row < 3 * nh)
    bwd_r = (row >= 3 * nh) & (row < 4 * nh)

    for r0 in range(0, g_ref.shape[0], c):
        pre = g_ref[r0:r0 + c, :] + bias_ref[...]
        lf = _log_sigmoid(pre)
        b = jnp.where(fwd_f, _tri_left(tril, lf), jnp.where(bwd_f, _tri_left(triu, lf), 0.0))
        gc_ref[r0:r0 + c, :] = pre - pltpu.roll(b, LANES - 2 * nh, axis=1)

        pre_t = pre.T[:GATE_ROWS]
        lf_t = _log_sigmoid(pre_t)
        b_t = jnp.where(fwd_r, _tri_right(lf_t, triu), jnp.where(bwd_r, _tri_right(lf_t, tril), 0.0))
        a_t = pre_t - pltpu.roll(b_t, GATE_ROWS - 2 * nh, axis=0)
        pmax, smax = a_t, a_t
        shift = 1
        while shift < c:
            pmax = jnp.maximum(pmax, jnp.where(pos >= shift, pltpu.roll(pmax, shift, axis=1), -jnp.inf))
            smax = jnp.maximum(smax, jnp.where(pos < c - shift, pltpu.roll(smax, c - shift, axis=1), -jnp.inf))
            shift *= 2
        amax_t = pltpu.roll(jnp.where(row < nh, pmax, smax), 4 * nh, axis=0)
        gr_ref[:, r0:r0 + c] = jnp.where(row < 2 * nh, a_t, jnp.where(row < 4 * nh, b_t, amax_t))


def _gateprep(g, bias, nh, chunk):
    rows = g.shape[0]
    assert 6 * nh <= GATE_ROWS
    blk = next(n * chunk for n in (GATE_CHUNKS, 4, 2, 1) if rows % (n * chunk) == 0)
    bias_row = jnp.zeros((1, LANES), F32).at[0, :4 * nh].set(bias.reshape(-1))
    return pl.pallas_call(
        functools.partial(_gateprep_kernel, nh=nh, c=chunk),
        out_shape=(jax.ShapeDtypeStruct((rows, LANES), F32),
                   jax.ShapeDtypeStruct((GATE_ROWS, rows), F32)),
        grid=(rows // blk,),
        in_specs=[pl.BlockSpec((blk, LANES), lambda i: (i, 0)),
                  pl.BlockSpec((1, LANES), lambda i: (0, 0))],
        out_specs=(pl.BlockSpec((blk, LANES), lambda i: (i, 0)),
                   pl.BlockSpec((GATE_ROWS, blk), lambda i: (0, i))),
        compiler_params=_params(1),
        name="gateprep",
    )(g, bias_row)


def _conv_kernel(x_ref, w_ref, b_ref, o_ref, *, rows, width, taps):
    cb = x_ref.shape[-1]
    j = pl.program_id(1)
    scale = jnp.where(j >= pl.num_programs(1) // 2, ML_DH ** -0.5, 1.0).astype(F32)
    w = w_ref[...]
    bias = b_ref[...]
    col = lax.broadcasted_iota(jnp.int32, (width, cb), 0)
    first = col == 0
    last = col == width - 1

    def one_row(r):
        left = mid = right = None
        for ki in taps:
            rr = r + (ki - 1)
            valid = jnp.where((rr >= 0) & (rr < rows), 1.0, 0.0).astype(F32)
            wk = w[3 * ki:3 * ki + 3, :] * valid
            start = pl.multiple_of(jnp.clip(rr, 0, rows - 1) * width, width)
            xv = x_ref[pl.ds(start, width), :].astype(F32)
            terms = [xv * wk[kj:kj + 1, :] for kj in range(3)]
            left, mid, right = terms if left is None else (left + terms[0], mid + terms[1], right + terms[2])
        out = (mid + jnp.where(first, 0.0, pltpu.roll(left, 1, axis=0))
               + jnp.where(last, 0.0, pltpu.roll(right, width - 1, axis=0)) + bias)
        dst = pl.multiple_of(r * width, width)
        o_ref[pl.ds(dst, width), :] = (_silu(out) * scale).astype(o_ref.dtype)

    def body(r, carry):
        one_row(r)
        return carry

    lax.fori_loop(0, rows, body, 0)


def _short_conv(u, col0, n_ch, conv_w, conv_b, grid):
    bsz, t, _ = u.shape
    cb = COL_BLOCK
    rows, width, taps = (t // GRID_W, GRID_W, (0, 1, 2)) if grid else (1, t, (1,))
    return pl.pallas_call(
        functools.partial(_conv_kernel, rows=rows, width=width, taps=taps),
        out_shape=jax.ShapeDtypeStruct((bsz, t, n_ch), BF16),
        grid=(bsz, n_ch // cb),
        in_specs=[pl.BlockSpec((None, t, cb), lambda b, j: (b, 0, col0 // cb + j)),
                  pl.BlockSpec((CONV_K * CONV_K, cb), lambda b, j: (0, j)),
                  pl.BlockSpec((1, cb), lambda b, j: (0, j))],
        out_specs=pl.BlockSpec((None, t, cb), lambda b, j: (b, 0, j)),
        compiler_params=_params(2),
        name="conv_grid" if grid else "conv_seq",
    )(u, conv_w.reshape(CONV_K * CONV_K, n_ch).astype(F32), conv_b.reshape(1, n_ch).astype(F32))


def _hg_prepare(qraw, z, lb, fwd):
    lower, upper = _tri_masks(z.shape[0])
    tri = jnp.where(lower if fwd else upper, 1.0, 0.0).astype(BF16)
    f = lb + (1.0 - lb) * jax.nn.sigmoid(z)
    b2 = _tri_left(tri, jnp.log(f) * LOG2E)
    c2 = b2 - jnp.log(1.0 - f) * LOG2E
    return _silu(qraw), b2, c2


def _block_diag(a):
    r, w = a.shape
    zero = jnp.zeros((r, w // 2), a.dtype)
    return jnp.concatenate([jnp.concatenate([a[:, :w // 2], zero], axis=1),
                            jnp.concatenate([zero, a[:, w // 2:]], axis=1)], axis=0)


def _hg_scores(q_ref, b_ref, c_ref, fwd):
    c, w = q_ref.shape
    sub = HG_SUB
    nb = c // sub
    assert 2 * c == LANES and c & (c - 1) == 0
    row = lax.broadcasted_iota(jnp.int32, (sub, LANES), 0)
    col = lax.broadcasted_iota(jnp.int32, (sub, LANES), 1) & (c - 1)
    src = col & (sub - 1)
    code = jnp.where((src <= row) if fwd else (src >= row), col, -1)
    kr = lax.broadcasted_iota(jnp.int32, (w, LANES), 0)
    kl = lax.broadcasted_iota(jnp.int32, (w, LANES), 1)
    sel = jnp.where((kr >= w // 2) == (kl >= c), 1.0, 0.0).astype(BF16)

    units = []
    for j in range(nb):
        r0 = j * sub
        qb = q_ref[r0:r0 + sub, :]
        bb = b_ref[r0:r0 + sub, :]
        cb = c_ref[r0:r0 + sub, :]
        for s in range(sub):
            units.append(qb * jnp.exp2(bb - cb[s:s + 1, :]))
    red = jnp.dot(jnp.concatenate(units, axis=0).astype(BF16), sel, preferred_element_type=F32)

    blocks = []
    for j in range(nb):
        r0 = j * sub
        blk = jnp.zeros((sub, LANES), F32)
        for s in range(sub):
            u0 = (j * sub + s) * sub
            blk = jnp.where(code == r0 + s, red[u0:u0 + sub], blk)
        lo, hi = (0, r0) if fwd else (r0 + sub, c)
        if hi > lo:
            beta = b_ref[pl.ds(r0 - 1 if fwd else r0 + sub, 1), :]
            pieces = [jnp.zeros((lo, w), F32)] if lo else []
            pieces.append(jnp.exp2(beta - c_ref[lo:hi, :]))
            if hi < c:
                pieces.append(jnp.zeros((c - hi, w), F32))
            kh = jnp.concatenate(pieces, axis=0).astype(BF16)
            qh = q_ref[r0:r0 + sub, :] * jnp.exp2(b_ref[r0:r0 + sub, :] - beta)
            off = lax.dot_general(_block_diag(qh).astype(BF16), kh, _NT, preferred_element_type=F32)
            blk = blk + jnp.concatenate([off[:sub], off[sub:]], axis=1)
        blocks.append(blk)
    return jnp.concatenate(blocks, axis=0).astype(BF16)


def _hg_scores_factored(q_ref, b_ref, c_ref, fwd, sub):
    c, w = q_ref.shape
    nb = c // sub
    row = lax.broadcasted_iota(jnp.int32, (sub, LANES), 0)
    col = lax.broadcasted_iota(jnp.int32, (sub, LANES), 1) & (c - 1)
    blocks = []
    for j in range(nb):
        r0 = j * sub
        lo, hi = (0, r0 + sub) if fwd else (r0, c)
        edge = r0 - 1 if fwd else r0 + sub
        beta = b_ref[pl.ds(edge, 1), :] if 0 <= edge < c else jnp.zeros((1, w), F32)
        pieces = [jnp.zeros((lo, w), F32)] if lo else []
        pieces.append(jnp.exp2(beta - c_ref[lo:hi, :]))
        if hi < c:
            pieces.append(jnp.zeros((c - hi, w), F32))
        kh = jnp.concatenate(pieces, axis=0).astype(BF16)
        qh = q_ref[r0:r0 + sub, :] * jnp.exp2(b_ref[r0:r0 + sub, :] - beta)
        off = lax.dot_general(_block_diag(qh).astype(BF16), kh, _NT, preferred_element_type=F32)
        blk = jnp.concatenate([off[:sub], off[sub:]], axis=1)
        causal = (col <= row + r0) if fwd else (col >= row + r0)
        blocks.append(jnp.where(causal, blk, 0.0))
    return jnp.concatenate(blocks, axis=0).astype(BF16)


def _hg_state_update(st_refs, v, k_dec, dec, qe=None):
    hw = v.shape[1] // 2
    outs = []
    for h, st_ref in enumerate(st_refs):
        ls = slice(h * hw, (h + 1) * hw)
        st = st_ref[...]
        if qe is not None:
            outs.append(lax.dot_general(qe[:, ls], st.astype(BF16), _NT, preferred_element_type=F32))
        st_ref[...] = st * dec[:, ls] + lax.dot_general(v[:, ls], k_dec[:, ls], _TN, preferred_element_type=F32)
    return outs


def _hgrn2_kernel(cq, cff, cfb, ci, aq, aff, afb, ai, az, lb_ref, nw_ref, y_ref,
                  acc_f, acc_b, st_ref, q_sc, b_sc, c_sc, a_sc, qe_sc, kd_sc, dec_sc, *, slots):
    c = HG_CHUNK
    ncc = cq.shape[0] // c
    nc = aq.shape[0] // c
    hw = HG_DK
    assert nc % 8 == 0
    st_ref[...] = jnp.zeros_like(st_ref)
    accs = (acc_f, acc_b)
    lbs = []
    for d in range(2):
        hg = lb_ref[d * slots:(d + 1) * slots, :]
        e = jnp.exp(hg - jnp.max(hg, axis=0, keepdims=True))
        lbs.append(e[0:1, :] / jnp.sum(e, axis=0, keepdims=True))

    def chunk_rows(d, k, n):
        return pl.ds(pl.multiple_of((k if d == 0 else n - 1 - k) * c, c), c)

    def states(d):
        return st_ref.at[2 * d], st_ref.at[2 * d + 1]

    def ctx_body(k, carry):
        for d in range(2):
            rows = chunk_rows(d, k, ncc)
            _, b2, c2 = _hg_prepare(cq[rows, :].astype(F32), (cff, cfb)[d][rows, :].astype(F32), lbs[d], d == 0)
            end = c - 1 if d == 0 else 0
            b_end = b2[end:end + 1, :]
            _hg_state_update(states(d), ci[rows, :], jnp.exp2(b_end - c2).astype(BF16), jnp.exp2(b_end))
        return carry

    def gates(d, k, slot):
        rows = chunk_rows(d, k, nc)
        q, b2, c2 = _hg_prepare(aq[rows, :].astype(F32), (aff, afb)[d][rows, :].astype(F32), lbs[d], d == 0)
        q_sc[slot, d] = q
        b_sc[slot, d] = b2
        c_sc[slot, d] = c2

    def scores(d, slot, block):
        qr, br, cr = q_sc.at[slot, d], b_sc.at[slot, d], c_sc.at[slot, d]
        end = c - 1 if d == 0 else 0
        b_end = br[end:end + 1, :]
        qe_sc[slot, d] = (qr[...] * jnp.exp2(br[...])).astype(BF16)
        kd_sc[slot, d] = jnp.exp2(b_end - cr[...]).astype(BF16)
        dec_sc[slot, d] = jnp.exp2(b_end)
        if block is None:
            a_sc[slot, d] = _hg_scores(qr, br, cr, d == 0)
        else:
            a_sc[slot, d] = _hg_scores_factored(qr, br, cr, d == 0, block)

    def outputs(d, k, slot):
        rows = chunk_rows(d, k, nc)
        v = ai[rows, :]
        outs = _hg_state_update(states(d), v, kd_sc[slot, d], dec_sc[slot, d], qe_sc[slot, d])
        accs[d][rows, :] = (jnp.concatenate(outs, axis=1)
                            + jnp.dot(a_sc[slot, d], _block_diag(v), preferred_element_type=F32))

    def step(k, slot, block, do_out=True, do_gates=True, do_scores=True):
        for d in range(2):
            if do_gates:
                gates(d, k, slot)
            if do_scores:
                scores(d, 1 - slot, block)
            if do_out:
                outputs(d, k - 2, slot)

    def scan(block):
        def body8(kq, carry):
            for i in range(8):
                step(8 * kq + i, i % 2, block)
            return carry

        step(0, 0, block, do_out=False, do_scores=False)
        step(1, 1, block, do_out=False)
        for k in range(2, 8):
            step(k, k % 2, block)
        lax.fori_loop(1, nc // 8, body8, 0)
        step(nc, 0, block, do_gates=False)
        step(nc + 1, 1, block, do_gates=False, do_scores=False)

    def final_body(i, carry):
        rows = pl.ds(pl.multiple_of(i * c, c), c)
        tot = acc_f[rows, :] + acc_b[rows, :]
        gate = _silu(az[rows, :].astype(F32)) * nw_ref[...]
        for h in range(2):
            ls = slice(h * hw, (h + 1) * hw)
            th = tot[:, ls]
            ms = jnp.mean(th * th, axis=-1, keepdims=True)
            y_ref[rows, ls] = (th * lax.rsqrt(ms + NORM_EPS) * gate[:, ls]).astype(y_ref.dtype)
        return carry

    lax.fori_loop(0, ncc, ctx_body, 0, unroll=True)
    unit = jnp.max(-jnp.log(jnp.minimum(lbs[0], lbs[1])) * LOG2E)
    wide = c * unit <= HG_SAFE_LOG2
    narrow = jnp.logical_and(jnp.logical_not(wide), HG_SUB * unit <= HG_SAFE_LOG2)
    pl.when(wide)(lambda: scan(c))
    pl.when(narrow)(lambda: scan(HG_SUB))
    pl.when(jnp.logical_not(jnp.logical_or(wide, narrow)))(lambda: scan(None))
    lax.fori_loop(0, nc, final_body, 0, unroll=8)


def _hgrn2(u_x, u_c, w_a, hg_lb, hg_norm):
    bsz, t, _ = u_x.shape
    tc = u_c.shape[1]
    cb = 2 * HG_DK
    c = HG_CHUNK
    assert cb == COL_BLOCK
    seg = w_a // cb
    slots = hg_lb.shape[1]
    lb2 = hg_lb.reshape(2 * slots, w_a).astype(F32)

    def col(k):
        return lambda b, p: (b, 0, k * seg + p)

    ctx_specs = [pl.BlockSpec((None, tc, cb), col(k)) for k in (0, 1, 2, 3)]
    lat_specs = [pl.BlockSpec((None, t, cb), col(k)) for k in (0, 1, 2, 3, 4)]
    return pl.pallas_call(
        functools.partial(_hgrn2_kernel, slots=slots),
        out_shape=jax.ShapeDtypeStruct((bsz, t, w_a), BF16),
        grid=(bsz, seg),
        in_specs=ctx_specs + lat_specs + [
            pl.BlockSpec((2 * slots, cb), lambda b, p: (0, p)),
            pl.BlockSpec((1, cb), lambda b, p: (0, p))],
        out_specs=pl.BlockSpec((None, t, cb), lambda b, p: (b, 0, p)),
        scratch_shapes=[pltpu.VMEM((t, cb), F32),
                        pltpu.VMEM((t, cb), F32),
                        pltpu.VMEM((4, HG_DK, HG_DK), F32),
                        pltpu.VMEM((2, 2, c, cb), F32),
                        pltpu.VMEM((2, 2, c, cb), F32),
                        pltpu.VMEM((2, 2, c, cb), F32),
                        pltpu.VMEM((2, 2, c, 2 * c), BF16),
                        pltpu.VMEM((2, 2, c, cb), BF16),
                        pltpu.VMEM((2, 2, c, cb), BF16),
                        pltpu.VMEM((2, 2, 1, cb), F32)],
        compiler_params=_params(2),
        name="hgrn2",
    )(u_c, u_c, u_c, u_c, u_x, u_x, u_x, u_x, u_x, lb2, hg_norm.reshape(1, w_a).astype(F32))


def _ml_gates(a_row, b_row, amax_row, m_prev, fwd):
    end = a_row.shape[1] - 1 if fwd else 0
    b_end = b_row[:, end:end + 1]
    big_m = jnp.maximum(m_prev, amax_row)
    m_row = b_row + big_m
    m_new = m_row[:, end:end + 1]
    w_s = jnp.exp(b_end + a_row - m_new)
    decay = jnp.exp(b_end + m_prev - m_new)
    return big_m, m_row, m_new, w_s, decay


def _ml_state_lhs(v, w_s):
    vt = v.astype(F32).T
    lhs = jnp.concatenate([vt * w_s, jnp.broadcast_to(w_s, (ML_PAD, vt.shape[1]))], axis=0)
    return vt, lhs.astype(BF16)


def _mlstm_kernel(cq, ck, cv, cgc, cgr, xq, xk, xv, xo, xz, xgc, xgr, nw_ref, y_ref,
                  acc_f, acc_b, st_ref, wt_sc, vt_sc, lhs_sc, row_sc):
    c = ML_CHUNK
    ncc = cq.shape[0] // c
    nc = xq.shape[0] // c
    dh = xq.shape[1]
    assert nc % 4 == 0
    head = pl.program_id(1)
    nh = pl.num_programs(1)
    accs = (acc_f, acc_b)
    st_ref[...] = jnp.zeros_like(st_ref)
    lane = lax.broadcasted_iota(jnp.int32, (c, LANES), 1)
    lower, upper = _tri_masks(c)

    def chunk_rows(d, k, n):
        return pl.ds(pl.multiple_of((k if d == 0 else n - 1 - k) * c, c), c)

    def gate_rows(grr, d, rows):
        slot = d * nh + head
        return (grr[pl.ds(slot, 1), rows], grr[pl.ds(2 * nh + slot, 1), rows],
                grr[pl.ds(4 * nh + slot, 1), rows])

    def ctx_body(k, ms):
        out = []
        for d in range(2):
            rows = chunk_rows(d, k, ncc)
            _, _, m_new, w_s, decay = _ml_gates(*gate_rows(cgr, d, rows), ms[d], d == 0)
            _, lhs = _ml_state_lhs(cv[rows, :], w_s)
            st_ref[d] = decay * st_ref[d] + jnp.dot(lhs, ck[rows, :], preferred_element_type=F32)
            out.append(m_new)
        return tuple(out)

    def score_stage(d, k, slot, m_prev):
        rows = chunk_rows(d, k, nc)
        kq = lax.dot_general(xk[rows, :], xq[rows, :], _NT, preferred_element_type=F32)
        big_m, m_row, m_new, w_s, decay = _ml_gates(*gate_rows(xgr, d, rows), m_prev, d == 0)
        a_col = jnp.sum(jnp.where(lane == d * nh + head, xgc[rows, :], 0.0), axis=1, keepdims=True)
        mask = upper if d == 0 else lower
        w_t = jnp.exp(jnp.where(mask, a_col - big_m, -jnp.inf)) * kq
        wt_sc[slot, d] = w_t
        vt, lhs = _ml_state_lhs(xv[rows, :], w_s)
        lhs_sc[slot, d] = lhs
        vt_sc[slot, d] = vt.astype(BF16)
        row_sc[slot, d, 0:1, :] = jnp.sum(w_t, axis=0, keepdims=True)
        row_sc[slot, d, 1:2, :] = jnp.exp(m_prev - big_m)
        row_sc[slot, d, 2:3, :] = jnp.exp(-m_row)
        row_sc[slot, d, 3:4, :] = jnp.broadcast_to(decay, (1, c))
        return m_new

    def output_stage(d, k, slot):
        rows = chunk_rows(d, k, nc)
        st = st_ref[d]
        r = lax.dot_general(st.astype(BF16), xq[rows, :], _NT, preferred_element_type=F32)
        w_inter = row_sc[slot, d, 1:2, :]
        den = w_inter * r[dh:dh + 1, :] + row_sc[slot, d, 0:1, :]
        inv = 1.0 / jnp.maximum(jnp.abs(den), row_sc[slot, d, 2:3, :])
        accs[d][:, rows] = (r[:dh, :] * (w_inter * inv)
                            + jnp.dot(vt_sc[slot, d], (wt_sc[slot, d] * inv).astype(BF16),
                                      preferred_element_type=F32))
        st_ref[d] = (row_sc[slot, d, 3:4, 0:1] * st
                     + jnp.dot(lhs_sc[slot, d], xk[rows, :], preferred_element_type=F32))

    def quad_body(kq, ms):
        for i in range(4):
            slot = i % 2
            k = 4 * kq + i
            nxt = jnp.minimum(k + 1, nc - 1)
            new_ms = []
            for d in range(2):
                new_ms.append(score_stage(d, nxt, 1 - slot, ms[d]))
                output_stage(d, k, slot)
            ms = tuple(new_ms)
        return ms

    def final_body(i, carry):
        rows = pl.ds(pl.multiple_of(i * c, c), c)
        tot = acc_f[:, rows] + acc_b[:, rows]
        mu = jnp.mean(tot, axis=0, keepdims=True)
        tc_ = tot - mu
        var = jnp.mean(tc_ * tc_, axis=0, keepdims=True)
        normed = (tc_ * lax.rsqrt(var + NORM_EPS)).T * nw_ref[...]
        gate = jax.nn.sigmoid(xo[rows, :].astype(F32)) * _silu(xz[rows, :].astype(F32))
        y_ref[rows, :] = (normed * gate).astype(y_ref.dtype)
        return carry

    ms = (jnp.zeros((1, 1), F32), jnp.zeros((1, 1), F32))
    ms = lax.fori_loop(0, ncc, ctx_body, ms)
    ms = tuple(score_stage(d, 0, 0, ms[d]) for d in range(2))
    lax.fori_loop(0, nc // 4, quad_body, ms)
    lax.fori_loop(0, nc, final_body, 0, unroll=4)


def _mlstm(qk_x, qk_c, u_x, u_c, v_col, o_col, z_col, gates_x, gates_c, ml_norm, w_b):
    bsz, t, _ = u_x.shape
    tc = u_c.shape[1]
    nh = w_b // ML_DH
    cb = ML_DH
    assert cb == COL_BLOCK and t % (4 * ML_CHUNK) == 0 and tc % ML_CHUNK == 0

    def col(k):
        return lambda b, h: (b, 0, k + h)

    def seq_specs(n, with_gates):
        specs = [pl.BlockSpec((None, n, cb), col(0)),
                 pl.BlockSpec((None, n, cb), col(nh)),
                 pl.BlockSpec((None, n, cb), col(v_col // cb))]
        if with_gates:
            specs += [pl.BlockSpec((None, n, cb), col(o_col // cb)),
                      pl.BlockSpec((None, n, cb), col(z_col // cb))]
        specs += [pl.BlockSpec((n, LANES), lambda b, h: (b, 0)),
                  pl.BlockSpec((GATE_ROWS, n), lambda b, h: (0, b))]
        return specs

    return pl.pallas_call(
        _mlstm_kernel,
        out_shape=jax.ShapeDtypeStruct((bsz, t, w_b), BF16),
        grid=(bsz, nh),
        in_specs=seq_specs(tc, False) + seq_specs(t, True) + [pl.BlockSpec((1, cb), lambda b, h: (0, h))],
        out_specs=pl.BlockSpec((None, t, cb), lambda b, h: (b, 0, h)),
        scratch_shapes=[pltpu.VMEM((cb, t), F32),
                        pltpu.VMEM((cb, t), F32),
                        pltpu.VMEM((2, ML_DH + ML_PAD, ML_DH), F32),
                        pltpu.VMEM((2, 2, ML_CHUNK, ML_CHUNK), F32),
                        pltpu.VMEM((2, 2, ML_DH, ML_CHUNK), BF16),
                        pltpu.VMEM((2, 2, ML_DH + ML_PAD, ML_CHUNK), BF16),
                        pltpu.VMEM((2, 2, 8, ML_CHUNK), F32)],
        compiler_params=_params(2),
        name="mlstm",
    )(qk_c, qk_c, u_c, gates_c[0], gates_c[1],
      qk_x, qk_x, u_x, u_x, u_x, gates_x[0], gates_x[1],
      ml_norm.reshape(1, w_b).astype(F32))


def _outproj_kernel(ya_ref, yb_ref, x_ref, gate_ref, wa_ref, wb_ref, g_ref, b_ref, o_ref, *, sub):
    for r0 in range(0, x_ref.shape[0], sub):
        rows = slice(r0, r0 + sub)
        y = (jnp.dot(ya_ref[rows, :], wa_ref[...], preferred_element_type=F32)
             + jnp.dot(yb_ref[rows, :], wb_ref[...], preferred_element_type=F32))
        r = ALPHA * x_ref[rows, :] + gate_ref[...] * y
        mu = jnp.mean(r, axis=-1, keepdims=True)
        rc = r - mu
        var = jnp.mean(rc * rc, axis=-1, keepdims=True)
        o_ref[rows, :] = rc * lax.rsqrt(var + LN_EPS) * g_ref[...] + b_ref[...]


def _outproj(y_a, y_b, x2, mod3, t, w_oa, w_ob, ln_g, ln_b):
    rows, d = x2.shape
    tm = OUT_TM
    assert rows % tm == 0 and t % tm == 0
    wa, wb = y_a.shape[1], y_b.shape[1]
    return pl.pallas_call(
        functools.partial(_outproj_kernel, sub=OUT_SUB),
        out_shape=jax.ShapeDtypeStruct((rows, d), F32),
        grid=(rows // tm,),
        in_specs=[pl.BlockSpec((tm, wa), lambda i: (i, 0)),
                  pl.BlockSpec((tm, wb), lambda i: (i, 0)),
                  pl.BlockSpec((tm, d), lambda i: (i, 0)),
                  pl.BlockSpec((None, 1, d), lambda i: ((i * tm) // t, 0, 2)),
                  pl.BlockSpec((wa, d), lambda i: (0, 0)),
                  pl.BlockSpec((wb, d), lambda i: (0, 0)),
                  pl.BlockSpec((1, d), lambda i: (0, 0)),
                  pl.BlockSpec((1, d), lambda i: (0, 0))],
        out_specs=pl.BlockSpec((tm, d), lambda i: (i, 0)),
        compiler_params=_params(1),
        name="outproj",
    )(y_a, y_b, x2, mod3, w_oa, w_ob, ln_g.reshape(1, d).astype(F32), ln_b.reshape(1, d).astype(F32))


def kernel(x, c, ctx, c_ctx, w_mod, b_mod, w_in, conv_w, conv_b, hg_lb, ml_gate_b, hg_norm_w,
           ml_norm_w, w_out, ln_g, ln_b):
    bsz, t, d = x.shape
    tc = ctx.shape[1]
    assert w_in.shape[0] == DEPTH and hg_lb.shape[1] == DEPTH + 1
    d_inner = w_out.shape[1]
    w_a = hg_lb.shape[2]
    w_b = d_inner - w_a
    nh = ml_gate_b.shape[2]
    nu = 5 * w_a + 5 * w_b
    assert nh * ML_DH == w_b and w_in.shape[2] == nu + 4 * nh and t % GRID_W == 0

    mod_rows = 8 * ((bsz + 1 + 7) // 8)
    cc = jnp.zeros((mod_rows, d), F32).at[:bsz].set(c).at[bsz].set(c_ctx)
    mod3 = _modulation(cc, w_mod[0], b_mod[0]).reshape(mod_rows, 1, 3 * d)

    w_in_t = w_in[0].T
    w_main = _wcast(w_in_t, 0, nu, WEIGHT_TN)
    w_g = _wcast(w_in_t, nu, LANES, LANES)
    tm = min(PROJ_TM, t)
    u_x, g_x = _inproj(x.reshape(bsz * t, d), mod3, lambda i: (i * tm) // t, w_main, w_g, tm)
    u_c, g_c = _inproj(ctx.reshape(bsz * tc, d), mod3, lambda i: bsz, w_main, w_g, min(PROJ_TM, bsz * tc))
    u_x = u_x.reshape(bsz, t, nu)
    u_c = u_c.reshape(bsz, tc, nu)

    gates_x = _gateprep(g_x, ml_gate_b[0], nh, ML_CHUNK)
    gates_c = _gateprep(g_c, ml_gate_b[0], nh, ML_CHUNK)

    qk_col = 5 * w_a
    qk_x = _short_conv(u_x, qk_col, 2 * w_b, conv_w[0], conv_b[0], grid=True)
    qk_c = _short_conv(u_c, qk_col, 2 * w_b, conv_w[0], conv_b[0], grid=False)

    y_a = _hgrn2(u_x, u_c, w_a, hg_lb, hg_norm_w[0])
    y_b = _mlstm(qk_x, qk_c, u_x, u_c, qk_col + 2 * w_b, qk_col + 3 * w_b, qk_col + 4 * w_b,
                 gates_x, gates_c, ml_norm_w[0], w_b)

    w_o = w_out[0].astype(BF16)
    out = _outproj(y_a.reshape(bsz * t, w_a), y_b.reshape(bsz * t, w_b), x.reshape(bsz * t, d), mod3, t,
                   w_o[:w_a], w_o[w_a:], ln_g[0], ln_b[0])
    return out.reshape(bsz, t, d)
```
